```python
import jax, jax.numpy as jnp
from jax import lax
import numpy as np

D_MODEL = 1024
BATCH = 8
SEQ = 4096
DEPTH = 1

GLA_HEADS = 4
GLA_DK = D_MODEL // 8
GLA_DV = D_MODEL // 4
GLA_RANK = 16
GLA_TAU = 16.0
GLA_CHUNK = 64
SWA_HEADS = 16
SWA_KV_HEADS = 2
SWA_HEAD_DIM = 64
SWA_WINDOW = 128
ROPE_DIM = SWA_HEAD_DIM // 4
ROPE_THETA = 500000.0
D_FF = -((-8 * D_MODEL) // (3 * 256)) * 256

GLA_QK = GLA_HEADS * GLA_DK
GLA_V = GLA_HEADS * GLA_DV
SWA_Q = SWA_HEADS * SWA_HEAD_DIM
SWA_KV = SWA_KV_HEADS * SWA_HEAD_DIM
SPLIT_SIZES = (GLA_QK, GLA_QK, GLA_V, GLA_V, GLA_RANK, SWA_Q, SWA_KV, SWA_KV, D_MODEL, D_MODEL)
D_IN = sum(SPLIT_SIZES)
SPLIT_POINTS = tuple(int(s) for s in np.cumsum(SPLIT_SIZES)[:-1])

DEEPNORM_ALPHA = (2.0 * DEPTH) ** 0.25
DEEPNORM_BETA = (8.0 * DEPTH) ** -0.25
LN_EPS = 1e-5
RMS_EPS = 1e-6

kernel_name = 'hybrid_gla_swa_sink_gated_deepnorm'


def layer_norm(x, g, b):
    xf = x.astype(jnp.float32)
    mu = jnp.mean(xf, axis=-1, keepdims=True)
    var = jnp.mean(jnp.square(xf - mu), axis=-1, keepdims=True)
    y = (xf - mu) * lax.rsqrt(var + LN_EPS) * g.astype(jnp.float32) + b.astype(jnp.float32)
    return y.astype(x.dtype)


def gla_chunked(q, k, v, log_a):
    B, T, H, dk = q.shape
    dv = v.shape[-1]
    C = GLA_CHUNK
    N = T // C

    def to_chunks(t):
        return t.astype(jnp.float32).reshape(B, N, C, H, t.shape[-1]).transpose(1, 0, 3, 2, 4)

    q, k, v, log_a = map(to_chunks, (q, k, v, log_a))
    b = jnp.cumsum(log_a, axis=-2)
    b_last = b[..., -1:, :]
    q_in = q * jnp.exp(b)
    k_in = k * jnp.exp(-b)
    k_out = k * jnp.exp(b_last - b)
    causal = jnp.tril(jnp.ones((C, C), dtype=bool))
    scores = jnp.where(causal, jnp.einsum('nbhtd,nbhsd->nbhts', q_in, k_in), 0.0)
    o_intra = jnp.einsum('nbhts,nbhsv->nbhtv', scores, v)

    def step(S, inp):
        q_c, k_c, v_c, decay = inp
        o = jnp.einsum('bhtd,bhdv->bhtv', q_c, S)
        S = S * decay[..., 0, :, None] + jnp.einsum('bhsd,bhsv->bhdv', k_c, v_c)
        return S, o

    S0 = jnp.zeros((B, H, dk, dv), jnp.float32)
    _, o_inter = lax.scan(step, S0, (q_in, k_out, v, jnp.exp(b_last)))
    o = o_intra + o_inter
    return o.transpose(1, 0, 3, 2, 4).reshape(B, T, H, dv)


def partial_rope(t, positions):
    half = ROPE_DIM // 2
    inv_freq = ROPE_THETA ** (-jnp.arange(0, ROPE_DIM, 2, dtype=jnp.float32) / ROPE_DIM)
    ang = positions.astype(jnp.float32)[..., None] * inv_freq
    cos = jnp.cos(ang)[:, :, None, :]
    sin = jnp.sin(ang)[:, :, None, :]
    tf = t.astype(jnp.float32)
    x1, x2, keep = tf[..., :half], tf[..., half:ROPE_DIM], tf[..., ROPE_DIM:]
    out = jnp.concatenate([x1 * cos - x2 * sin, x1 * sin + x2 * cos, keep], axis=-1)
    return out.astype(t.dtype)


def swa_sink_attention(q, k, v, sinks):
    B, T, Hq, Dh = q.shape
    Hkv = k.shape[2]
    G = Hq // Hkv
    W = SWA_WINDOW
    N = T // W
    qb = q.astype(jnp.float32).reshape(B, N, W, Hkv, G, Dh)
    kb = k.astype(jnp.float32).reshape(B, N, W, Hkv, Dh)
    vb = v.astype(jnp.float32).reshape(B, N, W, Hkv, Dh)

    def with_prev(t):
        prev = jnp.concatenate([jnp.zeros_like(t[:, :1]), t[:, :-1]], axis=1)
        return jnp.concatenate([prev, t], axis=2)

    k2, v2 = with_prev(kb), with_prev(vb)
    scores = jnp.einsum('bnqhgd,bnkhd->bhgnqk', qb, k2) * (Dh ** -0.5)
    q_pos = jnp.arange(W)[None, :, None] + W
    k_pos = jnp.arange(2 * W)[None, None, :]
    blk = jnp.arange(N)[:, None, None]
    valid = (k_pos <= q_pos) & (k_pos > q_pos - W) & ((blk > 0) | (k_pos >= W))
    scores = jnp.where(valid, scores, -jnp.inf)
    sink = sinks.astype(jnp.float32).reshape(Hkv, G)[None, :, :, None, None, None]
    m = jnp.maximum(jnp.max(scores, axis=-1, keepdims=True), sink)
    p = jnp.exp(scores - m)
    denom = jnp.sum(p, axis=-1, keepdims=True) + jnp.exp(sink - m)
    out = jnp.einsum('bhgnqk,bnkhd->bnqhgd', p / denom, v2)
    return out.reshape(B, T, Hq, Dh).astype(q.dtype)


def hybrid_mixer(h, positions, w_in, w_decay_up, b_decay, gla_norm_g,
                 w_branch_a, w_branch_b, sinks, w_out):
    B, T, _ = h.shape
    proj = h @ w_in
    q_a, k_a, v_a, r_a, d_low, q_b, k_b, v_b, g_a, g_b = jnp.split(proj, SPLIT_POINTS, axis=-1)

    log_a = jax.nn.log_sigmoid((d_low @ w_decay_up + b_decay).astype(jnp.float32)) / GLA_TAU
    q_a = q_a.reshape(B, T, GLA_HEADS, GLA_DK) * (GLA_DK ** -0.5)
    o_a = gla_chunked(q_a, k_a.reshape(B, T, GLA_HEADS, GLA_DK),
                      v_a.reshape(B, T, GLA_HEADS, GLA_DV),
                      log_a.reshape(B, T, GLA_HEADS, GLA_DK))
    o_a = o_a * lax.rsqrt(jnp.mean(jnp.square(o_a), axis=-1, keepdims=True) + RMS_EPS)
    o_a = o_a * gla_norm_g.astype(jnp.float32)
    o_a = (o_a.astype(h.dtype) * jax.nn.silu(r_a.reshape(B, T, GLA_HEADS, GLA_DV))).reshape(B, T, GLA_V)
    y_a = o_a @ w_branch_a

    q_b = partial_rope(q_b.reshape(B, T, SWA_HEADS, SWA_HEAD_DIM), positions)
    k_b = partial_rope(k_b.reshape(B, T, SWA_KV_HEADS, SWA_HEAD_DIM), positions)
    o_b = swa_sink_attention(q_b, k_b, v_b.reshape(B, T, SWA_KV_HEADS, SWA_HEAD_DIM), sinks)
    y_b = o_b.reshape(B, T, SWA_Q) @ w_branch_b

    merged = jax.nn.sigmoid(g_a) * y_a + jax.nn.sigmoid(g_b) * y_b
    return merged @ w_out


def swiglu_ffn(h, w_gate, w_up, w_down):
    return (jax.nn.silu(h @ w_gate) * (h @ w_up)) @ w_down


def setup_inputs(seed: int = 0) -> dict:
    key = jax.random.key(seed)
    ks = jax.random.split(key, 18)
    f32 = jnp.float32
    L = DEPTH

    def nrm(k, shape, fan_in, scale=1.0):
        return jax.random.normal(k, shape, f32) * (scale * fan_in ** -0.5)

    def gain(k, shape):
        return 1.0 + 0.02 * jax.random.normal(k, shape, f32)

    def bias(k, shape, s=0.02):
        return s * jax.random.normal(k, shape, f32)

    x = jax.random.normal(ks[0], (BATCH, SEQ, D_MODEL), f32)
    offset = jax.random.randint(ks[1], (BATCH, 1), 0, 1024, dtype=jnp.int32)
    positions = offset + jnp.arange(SEQ, dtype=jnp.int32)[None, :]
    return {
        'x': x,
        'positions': positions,
        'w_in': nrm(ks[2], (L, D_MODEL, D_IN), D_MODEL),
        'w_decay_up': nrm(ks[3], (L, GLA_RANK, GLA_QK), GLA_RANK),
        'b_decay': bias(ks[4], (L, GLA_QK), 0.1),
        'gla_norm_g': gain(ks[5], (L, GLA_DV)),
        'w_branch_a': nrm(ks[6], (L, GLA_V, D_MODEL), GLA_V),
        'w_branch_b': nrm(ks[7], (L, SWA_Q, D_MODEL), SWA_Q),
        'sinks': 0.5 * jax.random.normal(ks[8], (L, SWA_HEADS), f32),
        'w_out': nrm(ks[9], (L, D_MODEL, D_MODEL), D_MODEL, DEEPNORM_BETA),
        'ln1_g': gain(ks[10], (L, D_MODEL)),
        'ln1_b': bias(ks[11], (L, D_MODEL)),
        'w_ffn_gate': nrm(ks[12], (L, D_MODEL, D_FF), D_MODEL),
        'w_ffn_up': nrm(ks[13], (L, D_MODEL, D_FF), D_MODEL),
        'w_ffn_down': nrm(ks[14], (L, D_FF, D_MODEL), D_FF, DEEPNORM_BETA),
        'ln2_g': gain(ks[15], (L, D_MODEL)),
        'ln2_b': bias(ks[16], (L, D_MODEL)),
    }


def reference(x, positions, w_in, w_decay_up, b_decay, gla_norm_g, w_branch_a, w_branch_b,
              sinks, w_out, ln1_g, ln1_b, w_ffn_gate, w_ffn_up, w_ffn_down, ln2_g, ln2_b):
    h = x
    for layer in range(DEPTH):
        mix = hybrid_mixer(h, positions, w_in[layer], w_decay_up[layer], b_decay[layer],
                           gla_norm_g[layer], w_branch_a[layer], w_branch_b[layer],
                           sinks[layer], w_out[layer])
        h = layer_norm(DEEPNORM_ALPHA * h + mix, ln1_g[layer], ln1_b[layer])
        ffn = swiglu_ffn(h, w_ffn_gate[layer], w_ffn_up[layer], w_ffn_down[layer])
        h = layer_norm(DEEPNORM_ALPHA * h + ffn, ln2_g[layer], ln2_b[layer])
    return h
```

```python
import functools

import jax
import jax.numpy as jnp
from jax import lax
from jax.experimental import pallas as pl
from jax.experimental.pallas import tpu as pltpu

D_MODEL = 1024
GLA_HEADS = 4
GLA_DK = 128
GLA_DV = 256
GLA_RANK = 16
GLA_TAU = 16.0
GLA_CHUNK = 64
SWA_HEADS = 16
SWA_KV_HEADS = 2
SWA_HEAD_DIM = 64
SWA_WINDOW = 128
ROPE_DIM = 16
ROPE_THETA = 500000.0
DEPTH = 1
DEEPNORM_ALPHA = (2.0 * DEPTH) ** 0.25
LN_EPS = 1e-5
RMS_EPS = 1e-6

GLA_QK = GLA_HEADS * GLA_DK
GLA_V = GLA_HEADS * GLA_DV
SWA_Q = SWA_HEADS * SWA_HEAD_DIM
SWA_KV = SWA_KV_HEADS * SWA_HEAD_DIM

LANES = 128
RANK_PAD = LANES
VMEM_LIMIT = 56 * 1024 * 1024

_C_QA = 0
_C_KA = _C_QA + GLA_QK
_C_VA = _C_KA + GLA_QK
_C_RA = _C_VA + GLA_V
_C_QB = _C_RA + GLA_V
_C_KVD = _C_QB + SWA_Q
_C_GA = _C_KVD + 2 * SWA_KV + RANK_PAD
_C_GB = _C_GA + D_MODEL
_C_END = _C_GB + D_MODEL

F32 = jnp.float32
BF16 = jnp.bfloat16


def _dot(a, b):
    return jnp.dot(a, b, preferred_element_type=F32)


def _dot_nt(a, b):
    return lax.dot_general(a, b, (((1,), (1,)), ((), ())), preferred_element_type=F32)


def _dot_tn(a, b):
    return lax.dot_general(a, b, (((0,), (0,)), ((), ())), preferred_element_type=F32)


def _resident(shape):
    return pl.BlockSpec(shape, lambda *_: (0,) * len(shape), pipeline_mode=pl.Buffered(1))


def _inproj_kernel(x_ref, pos_ref, invf_ref, w_ref, wup_ref, bdec_ref,
                   qa_ref, ka_ref, va_ref, ra_ref, la_ref,
                   qb_ref, kb_ref, vb_ref, ga_ref, gb_ref):
    xb = x_ref[...].astype(BF16)

    def proj(lo, hi):
        return _dot(xb, w_ref[:, lo:hi])

    qa_ref[...] = (proj(_C_QA, _C_KA) * (GLA_DK ** -0.5)).astype(BF16)
    ka_ref[...] = proj(_C_KA, _C_VA).astype(BF16)
    va_ref[...] = proj(_C_VA, _C_RA).astype(BF16)
    ra_ref[...] = proj(_C_RA, _C_QB).astype(BF16)
    ga_ref[...] = proj(_C_GA, _C_GB).astype(BF16)
    gb_ref[...] = proj(_C_GB, _C_END).astype(BF16)

    ang = pos_ref[...] * invf_ref[...]
    cos = jnp.cos(ang)
    sin = jnp.sin(ang)
    lane = lax.broadcasted_iota(jnp.int32, ang.shape, 1) & (SWA_HEAD_DIM - 1)
    half = ROPE_DIM // 2
    a_tab = jnp.where(lane < half, -sin, 0.0)
    b_tab = jnp.where(lane >= half, sin, 0.0)

    def rope(t):
        return (t * cos + pltpu.roll(t, LANES - half, 1) * a_tab
                + pltpu.roll(t, half, 1) * b_tab)

    qb = proj(_C_QB, _C_KVD)
    for c in range(SWA_Q // LANES):
        sl = slice(c * LANES, (c + 1) * LANES)
        qb_ref[:, sl] = (rope(qb[:, sl]) * (SWA_HEAD_DIM ** -0.5)).astype(BF16)

    kvd = proj(_C_KVD, _C_GA)
    kb_ref[...] = rope(kvd[:, :SWA_KV]).astype(BF16)
    vb_ref[...] = kvd[:, SWA_KV:2 * SWA_KV].astype(BF16)
    d_low = kvd[:, 2 * SWA_KV:]
    z = _dot(d_low.astype(BF16), wup_ref[...]) + bdec_ref[...]
    log_sig = jnp.minimum(z, 0.0) - jnp.log1p(jnp.exp(-jnp.abs(z)))
    la_ref[...] = log_sig * (1.0 / GLA_TAU)


def _inproj(x2, pos_b, invf, w_in_r, w_up_p, b_dec, tm):
    m = x2.shape[0]
    row = lambda c: pl.BlockSpec((tm, c), lambda i: (i, 0))
    out_cols = (GLA_QK, GLA_QK, GLA_V, GLA_V, GLA_QK, SWA_Q, SWA_KV, SWA_KV, D_MODEL, D_MODEL)
    out_dt = (BF16, BF16, BF16, BF16, F32, BF16, BF16, BF16, BF16, BF16)
    return pl.pallas_call(
        _inproj_kernel,
        grid=(m // tm,),
        in_specs=[row(D_MODEL), row(LANES), _resident((1, LANES)),
                  _resident((D_MODEL, _C_END)), _resident((RANK_PAD, GLA_QK)),
                  _resident((1, GLA_QK))],
        out_specs=[row(c) for c in out_cols],
        out_shape=[jax.ShapeDtypeStruct((m, c), d) for c, d in zip(out_cols, out_dt)],
        compiler_params=pltpu.CompilerParams(
            dimension_semantics=("arbitrary",), vmem_limit_bytes=VMEM_LIMIT),
        name="inproj",
    )(x2, pos_b, invf, w_in_r, w_up_p, b_dec)


def _gla_kernel(q_ref, k_ref, v_ref, r_ref, la_ref, g_ref, o_ref, st_ref, *, n_chunks):
    @pl.when(pl.program_id(1) == 0)
    def _():
        st_ref[...] = jnp.zeros_like(st_ref)

    c_ = GLA_CHUNK
    causal = (lax.broadcasted_iota(jnp.int32, (c_, c_), 0)
              >= lax.broadcasted_iota(jnp.int32, (c_, c_), 1))
    tri = jnp.where(causal, 1.0, 0.0).astype(BF16)
    gain = g_ref[...]

    def chunk(c, carry):
        r0 = pl.multiple_of(c * c_, c_)
        rows = pl.ds(r0, c_)
        la = la_ref[rows, :]
        hi = la.astype(BF16)
        rem = la - hi.astype(F32)
        mid = rem.astype(BF16)
        lo = (rem - mid.astype(F32)).astype(BF16)
        b = _dot(tri, hi) + _dot(tri, mid) + _dot(tri, lo)
        for h in range(GLA_HEADS):
            ks = slice(h * GLA_DK, (h + 1) * GLA_DK)
            vs = slice(h * GLA_DV, (h + 1) * GLA_DV)
            bh = b[:, ks]
            b_last = bh[c_ - 1:c_, :]
            q = q_ref[rows, ks].astype(F32)
            k = k_ref[rows, ks].astype(F32)
            v = v_ref[rows, vs]
            q_in = (q * jnp.exp(bh)).astype(BF16)
            k_in = (k * jnp.exp(-bh)).astype(BF16)
            k_out = (k * jnp.exp(b_last - bh)).astype(BF16)
            scores = jnp.where(causal, _dot_nt(q_in, k_in), 0.0).astype(BF16)
            st = st_ref[h]
            o = _dot(scores, v) + _dot_nt(q_in, st.astype(BF16))
            st_ref[h] = st * jnp.exp(b_last) + _dot_tn(v, k_out)
            o = o * lax.rsqrt(jnp.mean(o * o, axis=-1, keepdims=True) + RMS_EPS)
            r = r_ref[rows, vs].astype(F32)
            o = o * gain * (r * jax.nn.sigmoid(r))
            o_ref[rows, vs] = o.astype(BF16)
        return carry

    lax.fori_loop(0, n_chunks, chunk, 0)


def _gla(qa, ka, va, ra, la, gain, batch, seq, tt):
    m = qa.shape[0]
    nt = seq // tt
    row = lambda c: pl.BlockSpec((tt, c), lambda b, t: (b * nt + t, 0))
    return pl.pallas_call(
        functools.partial(_gla_kernel, n_chunks=tt // GLA_CHUNK),
        grid=(batch, nt),
        in_specs=[row(GLA_QK), row(GLA_QK), row(GLA_V), row(GLA_V), row(GLA_QK),
                  pl.BlockSpec((1, GLA_DV), lambda b, t: (0, 0))],
        out_specs=row(GLA_V),
        out_shape=jax.ShapeDtypeStruct((m, GLA_V), BF16),
        scratch_shapes=[pltpu.VMEM((GLA_HEADS, GLA_DV, GLA_DK), F32)],
        compiler_params=pltpu.CompilerParams(
            dimension_semantics=("arbitrary", "arbitrary"), vmem_limit_bytes=VMEM_LIMIT),
        name="gla",
    )(qa, ka, va, ra, la, gain)


def _swa_kernel(sink_ref, q_ref, kc_ref, kp_ref, vc_ref, vp_ref, o_ref):
    w = SWA_WINDOW
    dh = SWA_HEAD_DIM
    n = pl.program_id(1)
    iq = lax.broadcasted_iota(jnp.int32, (w, 2 * w), 0)
    jk = lax.broadcasted_iota(jnp.int32, (w, 2 * w), 1)
    valid = (jk > iq) & (jk <= iq + w) & ((n > 0) | (jk >= w))

    k2 = jnp.concatenate([kp_ref[...], kc_ref[...]], axis=0).astype(F32)
    v2 = jnp.concatenate([vp_ref[...], vc_ref[...]], axis=0).astype(F32)
    k2s = pltpu.roll(k2, dh, 1)
    v2s = pltpu.roll(v2, dh, 1)
    low = lax.broadcasted_iota(jnp.int32, k2.shape, 1) < dh

    pairs_per_group = SWA_HEADS // SWA_KV_HEADS // 2
    for g in range(SWA_KV_HEADS):
        k_lo, k_hi = (k2, k2s) if g == 0 else (k2s, k2)
        v_lo, v_hi = (v2, v2s) if g == 0 else (v2s, v2)
        ke = jnp.where(low, k_lo, 0.0).astype(BF16)
        ko = jnp.where(low, 0.0, k_hi).astype(BF16)
        ve = jnp.where(low, v_lo, 0.0).astype(BF16)
        vo = jnp.where(low, 0.0, v_hi).astype(BF16)
        for p in range(pairs_per_group):
            col = (g * pairs_per_group + p) * LANES
            q_pair = q_ref[:, col:col + LANES]
            acc = None
            for kk, vv, head in ((ke, ve, col // dh), (ko, vo, col // dh + 1)):
                sink = sink_ref[head]
                s = jnp.where(valid, _dot_nt(q_pair, kk), -jnp.inf)
                mx = jnp.maximum(jnp.max(s, axis=-1, keepdims=True), sink)
                pr = jnp.exp(s - mx)
                den = jnp.sum(pr, axis=-1, keepdims=True) + jnp.exp(sink - mx)
                part = _dot(pr.astype(BF16), vv) * (1.0 / den)
                acc = part if acc is None else acc + part
            o_ref[:, col:col + LANES] = acc.astype(BF16)


def _swa(sinks, qb, kb, vb, batch, seq):
    m = qb.shape[0]
    w = SWA_WINDOW
    nb = seq // w
    cur = lambda c: pl.BlockSpec((w, c), lambda b, n: (b * nb + n, 0))
    prev = lambda c: pl.BlockSpec((w, c), lambda b, n: (b * nb + jnp.maximum(n - 1, 0), 0))
    return pl.pallas_call(
        _swa_kernel,
        grid=(batch, nb),
        in_specs=[pl.BlockSpec(memory_space=pltpu.SMEM),
                  cur(SWA_Q), cur(SWA_KV), prev(SWA_KV), cur(SWA_KV), prev(SWA_KV)],
        out_specs=cur(SWA_Q),
        out_shape=jax.ShapeDtypeStruct((m, SWA_Q), BF16),
        compiler_params=pltpu.CompilerParams(
            dimension_semantics=("arbitrary", "arbitrary"), vmem_limit_bytes=VMEM_LIMIT),
        name="swa",
    )(sinks, qb, kb, kb, vb, vb)


def _layer_norm(y, g, b):
    mu = jnp.mean(y, axis=-1, keepdims=True)
    d = y - mu
    var = jnp.mean(d * d, axis=-1, keepdims=True)
    return d * lax.rsqrt(var + LN_EPS) * g + b


def _merge_kernel(x_ref, oa_ref, ob_ref, ga_ref, gb_ref, wa_ref, wb_ref, wo_ref,
                  g_ref, b_ref, h_ref):
    y_a = _dot(oa_ref[...], wa_ref[...])
    y_b = _dot(ob_ref[...], wb_ref[...])
    merged = (jax.nn.sigmoid(ga_ref[...].astype(F32)) * y_a
              + jax.nn.sigmoid(gb_ref[...].astype(F32)) * y_b)
    mix = _dot(merged.astype(BF16), wo_ref[...])
    h_ref[...] = _layer_norm(DEEPNORM_ALPHA * x_ref[...] + mix, g_ref[...], b_ref[...])


def _merge(x2, oa, ob, ga, gb, wa, wb, wo, ln_g, ln_b, tm):
    m = x2.shape[0]
    row = lambda c: pl.BlockSpec((tm, c), lambda i: (i, 0))
    sq = _resident((D_MODEL, D_MODEL))
    vec = _resident((1, D_MODEL))
    return pl.pallas_call(
        _merge_kernel,
        grid=(m // tm,),
        in_specs=[row(D_MODEL)] * 5 + [sq, sq, sq, vec, vec],
        out_specs=row(D_MODEL),
        out_shape=jax.ShapeDtypeStruct((m, D_MODEL), F32),
        compiler_params=pltpu.CompilerParams(
            dimension_semantics=("arbitrary",), vmem_limit_bytes=VMEM_LIMIT),
        name="merge",
    )(x2, oa, ob, ga, gb, wa, wb, wo, ln_g, ln_b)


def _ffn_kernel(h_ref, wg_ref, wu_ref, wd_ref, g_ref, b_ref, o_ref, *, ff_chunk):
    h = h_ref[...]
    hb = h.astype(BF16)
    d_ff = wg_ref.shape[1]
    acc = None
    for j in range(d_ff // ff_chunk):
        sl = slice(j * ff_chunk, (j + 1) * ff_chunk)
        gate = _dot(hb, wg_ref[:, sl])
        up = _dot(hb, wu_ref[:, sl])
        act = (gate * jax.nn.sigmoid(gate) * up).astype(BF16)
        part = _dot(act, wd_ref[sl, :])
        acc = part if acc is None else acc + part
    o_ref[...] = _layer_norm(DEEPNORM_ALPHA * h + acc, g_ref[...], b_ref[...])


def _ffn(h1, wg, wu, wd, ln_g, ln_b, tm, ff_chunk):
    m = h1.shape[0]
    d_ff = wg.shape[1]
    row = pl.BlockSpec((tm, D_MODEL), lambda i: (i, 0))
    vec = _resident((1, D_MODEL))
    return pl.pallas_call(
        functools.partial(_ffn_kernel, ff_chunk=ff_chunk),
        grid=(m // tm,),
        in_specs=[row, _resident((D_MODEL, d_ff)), _resident((D_MODEL, d_ff)),
                  _resident((d_ff, D_MODEL)), vec, vec],
        out_specs=row,
        out_shape=jax.ShapeDtypeStruct((m, D_MODEL), F32),
        compiler_params=pltpu.CompilerParams(
            dimension_semantics=("arbitrary",), vmem_limit_bytes=VMEM_LIMIT),
        name="ffn",
    )(h1, wg, wu, wd, ln_g, ln_b)


def _rope_inv_freq_lanes():
    half = ROPE_DIM // 2
    inv_freq = ROPE_THETA ** (-jnp.arange(0, ROPE_DIM, 2, dtype=F32) / ROPE_DIM)
    per_head = jnp.concatenate(
        [inv_freq, inv_freq, jnp.zeros((SWA_HEAD_DIM - 2 * half,), F32)])
    return jnp.tile(per_head, LANES // SWA_HEAD_DIM)[None, :]


def _layer(h2, pos_b, invf, batch, seq, w_in, w_decay_up, b_decay, gla_norm_g,
           w_branch_a, w_branch_b, sinks, w_out, ln1_g, ln1_b,
           w_ffn_gate, w_ffn_up, w_ffn_down, ln2_g, ln2_b):
    o_dl = 2 * GLA_QK + 2 * GLA_V
    o_qb = o_dl + GLA_RANK
    o_kb = o_qb + SWA_Q
    o_ga = o_kb + 2 * SWA_KV
    w_in_r = jnp.concatenate(
        [w_in[:, :o_dl], w_in[:, o_qb:o_kb], w_in[:, o_kb:o_ga], w_in[:, o_dl:o_qb],
         jnp.zeros((D_MODEL, RANK_PAD - GLA_RANK), w_in.dtype), w_in[:, o_ga:]],
        axis=1).astype(BF16)
    w_up_p = jnp.concatenate(
        [w_decay_up, jnp.zeros((RANK_PAD - GLA_RANK, GLA_QK), w_decay_up.dtype)],
        axis=0).astype(BF16)

    qa, ka, va, ra, la, qb, kb, vb, ga, gb = _inproj(
        h2, pos_b, invf, w_in_r, w_up_p, b_decay[None, :], tm=256)
    oa = _gla(qa, ka, va, ra, la, gla_norm_g[None, :], batch, seq, tt=512)
    ob = _swa(sinks, qb, kb, vb, batch, seq)
    h1 = _merge(h2, oa, ob, ga, gb, w_branch_a.astype(BF16), w_branch_b.astype(BF16),
                w_out.astype(BF16), ln1_g[None, :], ln1_b[None, :], tm=512)
    return _ffn(h1, w_ffn_gate.astype(BF16), w_ffn_up.astype(BF16), w_ffn_down.astype(BF16),
                ln2_g[None, :], ln2_b[None, :], tm=512, ff_chunk=256)


def kernel(x, positions, w_in, w_decay_up, b_decay, gla_norm_g, w_branch_a, w_branch_b, sinks,
           w_out, ln1_g, ln1_b, w_ffn_gate, w_ffn_up, w_ffn_down, ln2_g, ln2_b):
    batch, seq, d = x.shape
    m = batch * seq
    h2 = x.reshape(m, d)
    pos_b = jnp.broadcast_to(positions.reshape(m, 1).astype(F32), (m, LANES))
    invf = _rope_inv_freq_lanes()
    for layer in range(w_in.shape[0]):
        h2 = _layer(h2, pos_b, invf, batch, seq, w_in[layer], w_decay_up[layer], b_decay[layer],
                    gla_norm_g[layer], w_branch_a[layer], w_branch_b[layer], sinks[layer],
                    w_out[layer], ln1_g[layer], ln1_b[layer], w_ffn_gate[layer],
                    w_ffn_up[layer], w_ffn_down[layer], ln2_g[layer], ln2_b[layer])
    return h2.reshape(batch, seq, d)
```

```python
import functools

import jax
import jax.numpy as jnp
from jax import lax
from jax.experimental import pallas as pl
from jax.experimental.pallas import tpu as pltpu

D_MODEL = 1024
GLA_HEADS = 4
GLA_DK = 128
GLA_DV = 256
GLA_RANK = 16
GLA_TAU = 16.0
GLA_CHUNK = 64
SWA_HEADS = 16
SWA_KV_HEADS = 2
SWA_HEAD_DIM = 64
SWA_WINDOW = 128
ROPE_DIM = 16
ROPE_THETA = 500000.0
DEPTH = 1
DEEPNORM_ALPHA = (2.0 * DEPTH) ** 0.25
LN_EPS = 1e-5
RMS_EPS = 1e-6

GLA_QK = GLA_HEADS * GLA_DK
GLA_V = GLA_HEADS * GLA_DV
SWA_Q = SWA_HEADS * SWA_HEAD_DIM
SWA_KV = SWA_KV_HEADS * SWA_HEAD_DIM

LANES = 128
RANK_PAD = LANES
VMEM_LIMIT = 56 * 1024 * 1024

_C_QA = 0
_C_KA = _C_QA + GLA_QK
_C_VA = _C_KA + GLA_QK
_C_RA = _C_VA + GLA_V
_C_QB = _C_RA + GLA_V
_C_KVD = _C_QB + SWA_Q
_C_GA = _C_KVD + 2 * SWA_KV + RANK_PAD
_C_GB = _C_GA + D_MODEL
_C_END = _C_GB + D_MODEL

F32 = jnp.float32
BF16 = jnp.bfloat16


def _dot(a, b):
    return jnp.dot(a, b, preferred_element_type=F32)


def _dot_nt(a, b):
    return lax.dot_general(a, b, (((1,), (1,)), ((), ())), preferred_element_type=F32)


def _dot_tn(a, b):
    return lax.dot_general(a, b, (((0,), (0,)), ((), ())), preferred_element_type=F32)


def _resident(shape):
    return pl.BlockSpec(shape, lambda *_: (0,) * len(shape), pipeline_mode=pl.Buffered(1))


def _inproj_kernel(x_ref, pos_ref, invf_ref, w_ref, wup_ref, bdec_ref,
                   qa_ref, ka_ref, va_ref, ra_ref, la_ref,
                   qb_ref, kb_ref, vb_ref, ga_ref, gb_ref):
    xb = x_ref[...].astype(BF16)

    def proj(lo, hi):
        return _dot(xb, w_ref[:, lo:hi])

    qa_ref[...] = (proj(_C_QA, _C_KA) * (GLA_DK ** -0.5)).astype(BF16)
    ka_ref[...] = proj(_C_KA, _C_VA).astype(BF16)
    va_ref[...] = proj(_C_VA, _C_RA).astype(BF16)
    ra_ref[...] = proj(_C_RA, _C_QB).astype(BF16)
    ga_ref[...] = proj(_C_GA, _C_GB).astype(BF16)
    gb_ref[...] = proj(_C_GB, _C_END).astype(BF16)

    ang = pos_ref[...] * invf_ref[...]
    cos = jnp.cos(ang)
    sin = jnp.sin(ang)
    lane = lax.broadcasted_iota(jnp.int32, ang.shape, 1) & (SWA_HEAD_DIM - 1)
    half = ROPE_DIM // 2
    a_tab = jnp.where(lane < half, -sin, 0.0)
    b_tab = jnp.where(lane >= half, sin, 0.0)

    def rope(t):
        return (t * cos + pltpu.roll(t, LANES - half, 1) * a_tab
                + pltpu.roll(t, half, 1) * b_tab)

    qb = proj(_C_QB, _C_KVD)
    for c in range(SWA_Q // LANES):
        sl = slice(c * LANES, (c + 1) * LANES)
        qb_ref[:, sl] = (rope(qb[:, sl]) * (SWA_HEAD_DIM ** -0.5)).astype(BF16)

    kvd = proj(_C_KVD, _C_GA)
    kb_ref[...] = rope(kvd[:, :SWA_KV]).astype(BF16)
    vb_ref[...] = kvd[:, SWA_KV:2 * SWA_KV].astype(BF16)
    d_low = kvd[:, 2 * SWA_KV:]
    z = _dot(d_low.astype(BF16), wup_ref[...]) + bdec_ref[...]
    log_sig = jnp.minimum(z, 0.0) - jnp.log1p(jnp.exp(-jnp.abs(z)))
    la_ref[...] = log_sig * (1.0 / GLA_TAU)


def _inproj(x2, pos_b, invf, w_in_r, w_up_p, b_dec, tm):
    m = x2.shape[0]
    row = lambda c: pl.BlockSpec((tm, c), lambda i: (i, 0))
    out_cols = (GLA_QK, GLA_QK, GLA_V, GLA_V, GLA_QK, SWA_Q, SWA_KV, SWA_KV, D_MODEL, D_MODEL)
    out_dt = (BF16, BF16, BF16, BF16, F32, BF16, BF16, BF16, BF16, BF16)
    return pl.pallas_call(
        _inproj_kernel,
        grid=(m // tm,),
        in_specs=[row(D_MODEL), row(LANES), _resident((1, LANES)),
                  _resident((D_MODEL, _C_END)), _resident((RANK_PAD, GLA_QK)),
                  _resident((1, GLA_QK))],
        out_specs=[row(c) for c in out_cols],
        out_shape=[jax.ShapeDtypeStruct((m, c), d) for c, d in zip(out_cols, out_dt)],
        compiler_params=pltpu.CompilerParams(
            dimension_semantics=("arbitrary",), vmem_limit_bytes=VMEM_LIMIT),
        name="inproj",
    )(x2, pos_b, invf, w_in_r, w_up_p, b_dec)


def _gla_kernel(q_ref, k_ref, v_ref, r_ref, la_ref, g_ref, o_ref, st_ref, *, n_chunks):
    @pl.when(pl.program_id(1) == 0)
    def _():
        st_ref[...] = jnp.zeros_like(st_ref)

    c_ = GLA_CHUNK
    causal = (lax.broadcasted_iota(jnp.int32, (c_, c_), 0)
              >= lax.broadcasted_iota(jnp.int32, (c_, c_), 1))
    tri = jnp.where(causal, 1.0, 0.0).astype(BF16)
    gain = g_ref[...]

    def chunk(c, carry):
        r0 = pl.multiple_of(c * c_, c_)
        rows = pl.ds(r0, c_)
        la = la_ref[rows, :]
        hi = la.astype(BF16)
        rem = la - hi.astype(F32)
        mid = rem.astype(BF16)
        lo = (rem - mid.astype(F32)).astype(BF16)
        b = _dot(tri, hi) + _dot(tri, mid) + _dot(tri, lo)
        for h in range(GLA_HEADS):
            ks = slice(h * GLA_DK, (h + 1) * GLA_DK)
            vs = slice(h * GLA_DV, (h + 1) * GLA_DV)
            bh = b[:, ks]
            b_last = bh[c_ - 1:c_, :]
            q = q_ref[rows, ks].astype(F32)
            k = k_ref[rows, ks].astype(F32)
            v = v_ref[rows, vs]
            q_in = (q * jnp.exp(bh)).astype(BF16)
            k_in = (k * jnp.exp(-bh)).astype(BF16)
            k_out = (k * jnp.exp(b_last - bh)).astype(BF16)
            scores = jnp.where(causal, _dot_nt(q_in, k_in), 0.0).astype(BF16)
            st = st_ref[h]
            o = _dot(scores, v) + _dot_nt(q_in, st.astype(BF16))
            st_ref[h] = st * jnp.exp(b_last) + _dot_tn(v, k_out)
            o = o * lax.rsqrt(jnp.mean(o * o, axis=-1, keepdims=True) + RMS_EPS)
            r = r_ref[rows, vs].astype(F32)
            o = o * gain * (r * jax.nn.sigmoid(r))
            o_ref[rows, vs] = o.astype(BF16)
        return carry

    lax.fori_loop(0, n_chunks, chunk, 0, unroll=True)


def _gla(qa, ka, va, ra, la, gain, batch, seq, tt):
    m = qa.shape[0]
    nt = seq // tt
    row = lambda c: pl.BlockSpec((tt, c), lambda b, t: (b * nt + t, 0))
    return pl.pallas_call(
        functools.partial(_gla_kernel, n_chunks=tt // GLA_CHUNK),
        grid=(batch, nt),
        in_specs=[row(GLA_QK), row(GLA_QK), row(GLA_V), row(GLA_V), row(GLA_QK),
                  pl.BlockSpec((1, GLA_DV), lambda b, t: (0, 0))],
        out_specs=row(GLA_V),
        out_shape=jax.ShapeDtypeStruct((m, GLA_V), BF16),
        scratch_shapes=[pltpu.VMEM((GLA_HEADS, GLA_DV, GLA_DK), F32)],
        compiler_params=pltpu.CompilerParams(
            dimension_semantics=("arbitrary", "arbitrary"), vmem_limit_bytes=VMEM_LIMIT),
        name="gla",
    )(qa, ka, va, ra, la, gain)


def _swa_kernel(sink_ref, q_ref, kc_ref, kp_ref, vc_ref, vp_ref, o_ref):
    w = SWA_WINDOW
    dh = SWA_HEAD_DIM
    n = pl.program_id(1)
    iq = lax.broadcasted_iota(jnp.int32, (w, 2 * w), 0)
    jk = lax.broadcasted_iota(jnp.int32, (w, 2 * w), 1)
    valid = (jk > iq) & (jk <= iq + w) & ((n > 0) | (jk >= w))

    k2 = jnp.concatenate([kp_ref[...], kc_ref[...]], axis=0).astype(F32)
    v2 = jnp.concatenate([vp_ref[...], vc_ref[...]], axis=0).astype(F32)
    k2s = pltpu.roll(k2, dh, 1)
    v2s = pltpu.roll(v2, dh, 1)
    low = lax.broadcasted_iota(jnp.int32, k2.shape, 1) < dh

    pairs_per_group = SWA_HEADS // SWA_KV_HEADS // 2
    for g in range(SWA_KV_HEADS):
        k_lo, k_hi = (k2, k2s) if g == 0 else (k2s, k2)
        v_lo, v_hi = (v2, v2s) if g == 0 else (v2s, v2)
        ke = jnp.where(low, k_lo, 0.0).astype(BF16)
        ko = jnp.where(low, 0.0, k_hi).astype(BF16)
        ve = jnp.where(low, v_lo, 0.0).astype(BF16)
        vo = jnp.where(low, 0.0, v_hi).astype(BF16)
        for p in range(pairs_per_group):
            col = (g * pairs_per_group + p) * LANES
            q_pair = q_ref[:, col:col + LANES]
            acc = None
            for kk, vv, head in ((ke, ve, col // dh), (ko, vo, col // dh + 1)):
                sink = sink_ref[head]
                s = jnp.where(valid, _dot_nt(q_pair, kk), -jnp.inf)
                mx = jnp.maximum(jnp.max(s, axis=-1, keepdims=True), sink)
                pr = jnp.exp(s - mx)
                den = jnp.sum(pr, axis=-1, keepdims=True) + jnp.exp(sink - mx)
                part = _dot(pr.astype(BF16), vv) * (1.0 / den)
                acc = part if acc is None else acc + part
            o_ref[:, col:col + LANES] = acc.astype(BF16)


def _swa(sinks, qb, kb, vb, batch, seq):
    m = qb.shape[0]
    w = SWA_WINDOW
    nb = seq // w
    cur = lambda c: pl.BlockSpec((w, c), lambda b, n: (b * nb + n, 0))
    prev = lambda c: pl.BlockSpec((w, c), lambda b, n: (b * nb + jnp.maximum(n - 1, 0), 0))
    return pl.pallas_call(
        _swa_kernel,
        grid=(batch, nb),
        in_specs=[pl.BlockSpec(memory_space=pltpu.SMEM),
                  cur(SWA_Q), cur(SWA_KV), prev(SWA_KV), cur(SWA_KV), prev(SWA_KV)],
        out_specs=cur(SWA_Q),
        out_shape=jax.ShapeDtypeStruct((m, SWA_Q), BF16),
        compiler_params=pltpu.CompilerParams(
            dimension_semantics=("arbitrary", "arbitrary"), vmem_limit_bytes=VMEM_LIMIT),
        name="swa",
    )(sinks, qb, kb, kb, vb, vb)


def _layer_norm(y, g, b):
    mu = jnp.mean(y, axis=-1, keepdims=True)
    d = y - mu
    var = jnp.mean(d * d, axis=-1, keepdims=True)
    return d * lax.rsqrt(var + LN_EPS) * g + b


def _merge_kernel(x_ref, oa_ref, ob_ref, ga_ref, gb_ref, wa_ref, wb_ref, wo_ref,
                  g_ref, b_ref, h_ref):
    y_a = _dot(oa_ref[...], wa_ref[...])
    y_b = _dot(ob_ref[...], wb_ref[...])
    merged = (jax.nn.sigmoid(ga_ref[...].astype(F32)) * y_a
              + jax.nn.sigmoid(gb_ref[...].astype(F32)) * y_b)
    mix = _dot(merged.astype(BF16), wo_ref[...])
    h_ref[...] = _layer_norm(DEEPNORM_ALPHA * x_ref[...] + mix, g_ref[...], b_ref[...])


def _merge(x2, oa, ob, ga, gb, wa, wb, wo, ln_g, ln_b, tm):
    m = x2.shape[0]
    row = lambda c: pl.BlockSpec((tm, c), lambda i: (i, 0))
    sq = _resident((D_MODEL, D_MODEL))
    vec = _resident((1, D_MODEL))
    return pl.pallas_call(
        _merge_kernel,
        grid=(m // tm,),
        in_specs=[row(D_MODEL)] * 5 + [sq, sq, sq, vec, vec],
        out_specs=row(D_MODEL),
        out_shape=jax.ShapeDtypeStruct((m, D_MODEL), F32),
        compiler_params=pltpu.CompilerParams(
            dimension_semantics=("arbitrary",), vmem_limit_bytes=VMEM_LIMIT),
        name="merge",
    )(x2, oa, ob, ga, gb, wa, wb, wo, ln_g, ln_b)


def _ffn_kernel(h_ref, wg_ref, wu_ref, wd_ref, g_ref, b_ref, o_ref, *, ff_chunk):
    h = h_ref[...]
    hb = h.astype(BF16)
    d_ff = wg_ref.shape[1]
    acc = None
    for j in range(d_ff // ff_chunk):
        sl = slice(j * ff_chunk, (j + 1) * ff_chunk)
        gate = _dot(hb, wg_ref[:, sl])
        up = _dot(hb, wu_ref[:, sl])
        act = (gate * jax.nn.sigmoid(gate) * up).astype(BF16)
        part = _dot(act, wd_ref[sl, :])
        acc = part if acc is None else acc + part
    o_ref[...] = _layer_norm(DEEPNORM_ALPHA * h + acc, g_ref[...], b_ref[...])


def _ffn(h1, wg, wu, wd, ln_g, ln_b, tm, ff_chunk):
    m = h1.shape[0]
    d_ff = wg.shape[1]
    row = pl.BlockSpec((tm, D_MODEL), lambda i: (i, 0))
    vec = _resident((1, D_MODEL))
    return pl.pallas_call(
        functools.partial(_ffn_kernel, ff_chunk=ff_chunk),
        grid=(m // tm,),
        in_specs=[row, _resident((D_MODEL, d_ff)), _resident((D_MODEL, d_ff)),
                  _resident((d_ff, D_MODEL)), vec, vec],
        out_specs=row,
        out_shape=jax.ShapeDtypeStruct((m, D_MODEL), F32),
        compiler_params=pltpu.CompilerParams(
            dimension_semantics=("arbitrary",), vmem_limit_bytes=VMEM_LIMIT),
        name="ffn",
    )(h1, wg, wu, wd, ln_g, ln_b)


def _rope_inv_freq_lanes():
    half = ROPE_DIM // 2
    inv_freq = ROPE_THETA ** (-jnp.arange(0, ROPE_DIM, 2, dtype=F32) / ROPE_DIM)
    per_head = jnp.concatenate(
        [inv_freq, inv_freq, jnp.zeros((SWA_HEAD_DIM - 2 * half,), F32)])
    return jnp.tile(per_head, LANES // SWA_HEAD_DIM)[None, :]


def _layer(h2, pos_b, invf, batch, seq, w_in, w_decay_up, b_decay, gla_norm_g,
           w_branch_a, w_branch_b, sinks, w_out, ln1_g, ln1_b,
           w_ffn_gate, w_ffn_up, w_ffn_down, ln2_g, ln2_b):
    o_dl = 2 * GLA_QK + 2 * GLA_V
    o_qb = o_dl + GLA_RANK
    o_kb = o_qb + SWA_Q
    o_ga = o_kb + 2 * SWA_KV
    w_in_r = jnp.concatenate(
        [w_in[:, :o_dl], w_in[:, o_qb:o_kb], w_in[:, o_kb:o_ga], w_in[:, o_dl:o_qb],
         jnp.zeros((D_MODEL, RANK_PAD - GLA_RANK), w_in.dtype), w_in[:, o_ga:]],
        axis=1).astype(BF16)
    w_up_p = jnp.concatenate(
        [w_decay_up, jnp.zeros((RANK_PAD - GLA_RANK, GLA_QK), w_decay_up.dtype)],
        axis=0).astype(BF16)

    qa, ka, va, ra, la, qb, kb, vb, ga, gb = _inproj(
        h2, pos_b, invf, w_in_r, w_up_p, b_decay[None, :], tm=512)
    oa = _gla(qa, ka, va, ra, la, gla_norm_g[None, :], batch, seq, tt=512)
    ob = _swa(sinks, qb, kb, vb, batch, seq)
    h1 = _merge(h2, oa, ob, ga, gb, w_branch_a.astype(BF16), w_branch_b.astype(BF16),
                w_out.astype(BF16), ln1_g[None, :], ln1_b[None, :], tm=512)
    return _ffn(h1, w_ffn_gate.astype(BF16), w_ffn_up.astype(BF16), w_ffn_down.astype(BF16),
                ln2_g[None, :], ln2_b[None, :], tm=1024, ff_chunk=256)


def kernel(x, positions, w_in, w_decay_up, b_decay, gla_norm_g, w_branch_a, w_branch_b, sinks,
           w_out, ln1_g, ln1_b, w_ffn_gate, w_ffn_up, w_ffn_down, ln2_g, ln2_b):
    batch, seq, d = x.shape
    m = batch * seq
    h2 = x.reshape(m, d)
    pos_b = jnp.broadcast_to(positions.reshape(m, 1).astype(F32), (m, LANES))
    invf = _rope_inv_freq_lanes()
    for layer in range(w_in.shape[0]):
        h2 = _layer(h2, pos_b, invf, batch, seq, w_in[layer], w_decay_up[layer], b_decay[layer],
                    gla_norm_g[layer], w_branch_a[layer], w_branch_b[layer], sinks[layer],
                    w_out[layer], ln1_g[layer], ln1_b[layer], w_ffn_gate[layer],
                    w_ffn_up[layer], w_ffn_down[layer], ln2_g[layer], ln2_b[layer])
    return h2.reshape(batch, seq, d)
```

```python
import functools

import math

import jax
import jax.numpy as jnp
import numpy as np
from jax import lax
from jax.experimental import pallas as pl
from jax.experimental.pallas import tpu as pltpu

D_MODEL = 1024
GLA_HEADS = 4
GLA_DK = 128
GLA_DV = 256
GLA_RANK = 16
GLA_TAU = 16.0
GLA_CHUNK = 64
SWA_HEADS = 16
SWA_KV_HEADS = 2
SWA_HEAD_DIM = 64
SWA_WINDOW = 128
ROPE_DIM = 16
ROPE_THETA = 500000.0
DEPTH = 1
DEEPNORM_ALPHA = (2.0 * DEPTH) ** 0.25
LN_EPS = 1e-5
RMS_EPS = 1e-6

GLA_QK = GLA_HEADS * GLA_DK
GLA_V = GLA_HEADS * GLA_DV
SWA_Q = SWA_HEADS * SWA_HEAD_DIM
SWA_KV = SWA_KV_HEADS * SWA_HEAD_DIM

LANES = 128
RANK_PAD = LANES
VMEM_LIMIT = 56 * 1024 * 1024
LOG2_E = math.log2(math.e)
MASK_BIAS = -1e30

_C_QA = 0
_C_KA = _C_QA + GLA_QK
_C_VA = _C_KA + GLA_QK
_C_RA = _C_VA + GLA_V
_C_QB = _C_RA + GLA_V
_C_KVD = _C_QB + SWA_Q
_C_GA = _C_KVD + 2 * SWA_KV + RANK_PAD
_C_GB = _C_GA + D_MODEL
_C_END = _C_GB + D_MODEL

F32 = jnp.float32
BF16 = jnp.bfloat16


def _dot(a, b):
    return jnp.dot(a, b, preferred_element_type=F32)


def _dot_nt(a, b):
    return lax.dot_general(a, b, (((1,), (1,)), ((), ())), preferred_element_type=F32)


def _dot_tn(a, b):
    return lax.dot_general(a, b, (((0,), (0,)), ((), ())), preferred_element_type=F32)


def _resident(shape):
    return pl.BlockSpec(shape, lambda *_: (0,) * len(shape), pipeline_mode=pl.Buffered(1))


def _inproj_kernel(x_ref, pos_ref, invf_ref, sf_ref, w_ref, wup_ref, bdec_ref,
                   qa_ref, ka_ref, va_ref, ra_ref, la_ref,
                   qh_ref, kh_ref, ve_ref, vo_ref, ga_ref, gb_ref):
    xb = x_ref[...].astype(BF16)

    def proj(lo, hi):
        return _dot(xb, w_ref[:, lo:hi])

    qa_ref[...] = (proj(_C_QA, _C_KA) * (GLA_DK ** -0.5)).astype(BF16)
    ka_ref[...] = proj(_C_KA, _C_VA).astype(BF16)
    va_ref[...] = proj(_C_VA, _C_RA).astype(BF16)
    ra_ref[...] = proj(_C_RA, _C_QB).astype(BF16)
    ga_ref[...] = proj(_C_GA, _C_GB).astype(BF16)
    gb_ref[...] = proj(_C_GB, _C_END).astype(BF16)

    ang = pos_ref[...] * invf_ref[...]
    cos = jnp.cos(ang)
    sin = jnp.sin(ang)
    lane = lax.broadcasted_iota(jnp.int32, ang.shape, 1) & (SWA_HEAD_DIM - 1)
    half = ROPE_DIM // 2
    a_tab = jnp.where(lane < half, -sin, 0.0)
    b_tab = jnp.where(lane >= half, sin, 0.0)

    def rope(t):
        return (t * cos + pltpu.roll(t, LANES - half, 1) * a_tab
                + pltpu.roll(t, half, 1) * b_tab)

    dh = SWA_HEAD_DIM
    lane_full = lax.broadcasted_iota(jnp.int32, ang.shape, 1)
    low = lane_full < dh
    q_scale = (dh ** -0.5) * LOG2_E
    qb = proj(_C_QB, _C_KVD)
    for c in range(SWA_Q // LANES):
        t = rope(qb[:, c * LANES:(c + 1) * LANES]) * q_scale
        qh_ref[2 * c] = jnp.where(low, t, sf_ref[2 * c:2 * c + 1, :]).astype(BF16)
        qh_ref[2 * c + 1] = jnp.where(
            low, pltpu.roll(t, dh, 1), sf_ref[2 * c + 1:2 * c + 2, :]).astype(BF16)

    kvd = proj(_C_KVD, _C_GA)
    kk = rope(kvd[:, :SWA_KV])
    vv = kvd[:, SWA_KV:2 * SWA_KV]
    kk_sw = pltpu.roll(kk, dh, 1)
    vv_sw = pltpu.roll(vv, dh, 1)
    one_k = jnp.where(lane_full < dh + 2, 1.0, 0.0)
    one_e = jnp.where(lane_full == dh, 1.0, 0.0)
    one_o = jnp.where(lane_full == 0, 1.0, 0.0)
    kh_ref[0] = jnp.where(low, kk, one_k).astype(BF16)
    kh_ref[1] = jnp.where(low, kk_sw, one_k).astype(BF16)
    ve_ref[0] = jnp.where(low, vv, one_e).astype(BF16)
    ve_ref[1] = jnp.where(low, vv_sw, one_e).astype(BF16)
    vo_ref[0] = jnp.where(low, one_o, vv_sw).astype(BF16)
    vo_ref[1] = jnp.where(low, one_o, vv).astype(BF16)
    d_low = kvd[:, 2 * SWA_KV:]
    z = _dot(d_low.astype(BF16), wup_ref[...]) + bdec_ref[...]
    log_sig = jnp.minimum(z, 0.0) - jnp.log1p(jnp.exp(-jnp.abs(z)))
    la_ref[...] = log_sig * (1.0 / GLA_TAU)


def _inproj(x2, pos_b, invf, sink_feat, w_in_r, w_up_p, b_dec, tm):
    m = x2.shape[0]
    row = lambda c: pl.BlockSpec((tm, c), lambda i: (i, 0))
    row_out = lambda c, d: (row(c), jax.ShapeDtypeStruct((m, c), d))
    slab_out = lambda n: (pl.BlockSpec((n, tm, LANES), lambda i: (0, i, 0)),
                          jax.ShapeDtypeStruct((n, m, LANES), BF16))
    outs = [row_out(GLA_QK, BF16), row_out(GLA_QK, BF16), row_out(GLA_V, BF16),
            row_out(GLA_V, BF16), row_out(GLA_QK, F32),
            slab_out(SWA_HEADS), slab_out(SWA_KV_HEADS), slab_out(SWA_KV_HEADS),
            slab_out(SWA_KV_HEADS), row_out(D_MODEL, BF16), row_out(D_MODEL, BF16)]
    return pl.pallas_call(
        _inproj_kernel,
        grid=(m // tm,),
        in_specs=[row(D_MODEL), row(LANES), _resident((1, LANES)),
                  _resident((SWA_HEADS, LANES)),
                  _resident((D_MODEL, _C_END)), _resident((RANK_PAD, GLA_QK)),
                  _resident((1, GLA_QK))],
        out_specs=[o[0] for o in outs],
        out_shape=[o[1] for o in outs],
        compiler_params=pltpu.CompilerParams(
            dimension_semantics=("arbitrary",), vmem_limit_bytes=VMEM_LIMIT),
        name="inproj",
    )(x2, pos_b, invf, sink_feat, w_in_r, w_up_p, b_dec)


def _gla_kernel(q_ref, k_ref, v_ref, r_ref, la_ref, g_ref, o_ref, st_ref, *, n_chunks):
    @pl.when(pl.program_id(1) == 0)
    def _():
        st_ref[...] = jnp.zeros_like(st_ref)

    c_ = GLA_CHUNK
    causal = (lax.broadcasted_iota(jnp.int32, (c_, c_), 0)
              >= lax.broadcasted_iota(jnp.int32, (c_, c_), 1))
    tri = jnp.where(causal, 1.0, 0.0).astype(BF16)
    gain = g_ref[...]

    def chunk(c, carry):
        r0 = pl.multiple_of(c * c_, c_)
        rows = pl.ds(r0, c_)
        la = la_ref[rows, :]
        hi = la.astype(BF16)
        rem = la - hi.astype(F32)
        mid = rem.astype(BF16)
        lo = (rem - mid.astype(F32)).astype(BF16)
        b = _dot(tri, hi) + _dot(tri, mid) + _dot(tri, lo)
        for h in range(GLA_HEADS):
            ks = slice(h * GLA_DK, (h + 1) * GLA_DK)
            vs = slice(h * GLA_DV, (h + 1) * GLA_DV)
            bh = b[:, ks]
            b_last = bh[c_ - 1:c_, :]
            q = q_ref[rows, ks].astype(F32)
            k = k_ref[rows, ks].astype(F32)
            v = v_ref[rows, vs]
            q_in = (q * jnp.exp(bh)).astype(BF16)
            k_in = (k * jnp.exp(-bh)).astype(BF16)
            k_out = (k * jnp.exp(b_last - bh)).astype(BF16)
            scores = jnp.where(causal, _dot_nt(q_in, k_in), 0.0).astype(BF16)
            st = st_ref[h]
            o = _dot(scores, v) + _dot_nt(q_in, st.astype(BF16))
            st_ref[h] = st * jnp.exp(b_last) + _dot_tn(v, k_out)
            o = o * lax.rsqrt(jnp.mean(o * o, axis=-1, keepdims=True) + RMS_EPS)
            r = r_ref[rows, vs].astype(F32)
            o = o * gain * (r * jax.nn.sigmoid(r))
            o_ref[rows, vs] = o.astype(BF16)
        return carry

    lax.fori_loop(0, n_chunks, chunk, 0, unroll=True)


def _gla(qa, ka, va, ra, la, gain, batch, seq, tt):
    m = qa.shape[0]
    nt = seq // tt
    row = lambda c: pl.BlockSpec((tt, c), lambda b, t: (b * nt + t, 0))
    return pl.pallas_call(
        functools.partial(_gla_kernel, n_chunks=tt // GLA_CHUNK),
        grid=(batch, nt),
        in_specs=[row(GLA_QK), row(GLA_QK), row(GLA_V), row(GLA_V), row(GLA_QK),
                  pl.BlockSpec((1, GLA_DV), lambda b, t: (0, 0))],
        out_specs=row(GLA_V),
        out_shape=jax.ShapeDtypeStruct((m, GLA_V), BF16),
        scratch_shapes=[pltpu.VMEM((GLA_HEADS, GLA_DV, GLA_DK), F32)],
        compiler_params=pltpu.CompilerParams(
            dimension_semantics=("arbitrary", "arbitrary"), vmem_limit_bytes=VMEM_LIMIT),
        name="gla",
    )(qa, ka, va, ra, la, gain)


def _swa_kernel(q_ref, kc_ref, kp_ref, vec_ref, vep_ref, voc_ref, vop_ref, mask_ref, o_ref,
                *, blocks_per_step):
    w = SWA_WINDOW
    dh = SWA_HEAD_DIM
    per_group = SWA_HEADS // SWA_KV_HEADS
    first = pl.program_id(1) == 0
    row = lax.broadcasted_iota(jnp.int32, (w, LANES), 0)
    lane = lax.broadcasted_iota(jnp.int32, (w, LANES), 1)
    ident = jnp.where(row == lane, 1.0, 0.0).astype(BF16)
    sink_row = row == 0
    low = lane < dh
    zero = jnp.zeros((w, LANES), BF16)
    one_e = jnp.where(lane == dh, 1.0, 0.0).astype(BF16)
    one_o = jnp.where(lane == 0, 1.0, 0.0).astype(BF16)
    order = list(range(0, per_group, 2)) + list(range(1, per_group, 2))

    scores = []
    values = []
    for qb in range(blocks_per_step):
        rows = slice(qb * w, (qb + 1) * w)
        prev_rows = slice((qb - 1) * w, qb * w)
        mask = mask_ref[1] if qb else jnp.where(first, mask_ref[0], mask_ref[1])
        for g in range(SWA_KV_HEADS):
            k_prev = kc_ref[g, prev_rows] if qb else kp_ref[g]
            ve_prev = vec_ref[g, prev_rows] if qb else vep_ref[g]
            vo_prev = voc_ref[g, prev_rows] if qb else vop_ref[g]
            k2 = jnp.concatenate([jnp.where(sink_row, zero, k_prev), kc_ref[g, rows]], axis=0)
            kx = jnp.concatenate([k2, mask], axis=1)
            ve = jnp.concatenate([jnp.where(sink_row, one_e, ve_prev), vec_ref[g, rows]], axis=0)
            vo = jnp.concatenate([jnp.where(sink_row, one_o, vo_prev), voc_ref[g, rows]], axis=0)
            for part, vv in ((order[:per_group // 2], ve), (order[per_group // 2:], vo)):
                qs = jnp.concatenate(
                    [jnp.concatenate([q_ref[g * per_group + h, rows], ident], axis=1)
                     for h in part], axis=0)
                scores.append(_dot_nt(qs, kx))
                values.append(vv)
    pv = []
    for s, vv in zip(scores, values):
        p = jnp.exp2(s - jnp.max(s, axis=-1, keepdims=True)).astype(BF16)
        pv.append(_dot(p, vv))
    for qb in range(blocks_per_step):
        rows = slice(qb * w, (qb + 1) * w)
        for g in range(SWA_KV_HEADS):
            pv_e = pv[(qb * SWA_KV_HEADS + g) * 2]
            pv_o = pv[(qb * SWA_KV_HEADS + g) * 2 + 1]
            for c in range(per_group // 2):
                e = pv_e[c * w:(c + 1) * w]
                o = pv_o[c * w:(c + 1) * w]
                out = jnp.where(low, e * (1.0 / e[:, dh:dh + 1]), o * (1.0 / o[:, 0:1]))
                col = (g * (per_group // 2) + c) * LANES
                o_ref[rows, col:col + LANES] = out.astype(BF16)


def _swa_mask():
    w = SWA_WINDOW
    j = np.arange(2 * w)[:, None]
    i = np.arange(w)[None, :]
    band = (j > i) & (j <= i + w)
    later = band | (j == 0)
    first = (band & (j >= w)) | (j == 0)
    both = np.stack([first, later])
    return jnp.asarray(np.where(both, 0.0, MASK_BIAS), dtype=BF16)


def _swa(qh, kh, ve, vo, batch, seq, blocks_per_step):
    m = kh.shape[1]
    w = SWA_WINDOW
    tq = blocks_per_step * w
    steps = seq // tq
    kv = SWA_KV_HEADS
    cur = lambda n_: pl.BlockSpec((n_, tq, LANES), lambda b, t: (0, b * steps + t, 0))
    prev = lambda n_: pl.BlockSpec(
        (n_, w, LANES),
        lambda b, t: (0, b * (seq // w) + jnp.maximum(t * blocks_per_step - 1, 0), 0))
    return pl.pallas_call(
        functools.partial(_swa_kernel, blocks_per_step=blocks_per_step),
        grid=(batch, steps),
        in_specs=[cur(SWA_HEADS), cur(kv), prev(kv), cur(kv), prev(kv), cur(kv), prev(kv),
                  _resident((2, 2 * w, w))],
        out_specs=pl.BlockSpec((tq, SWA_Q), lambda b, t: (b * steps + t, 0)),
        out_shape=jax.ShapeDtypeStruct((m, SWA_Q), BF16),
        compiler_params=pltpu.CompilerParams(
            dimension_semantics=("arbitrary", "arbitrary"), vmem_limit_bytes=VMEM_LIMIT),
        name="swa",
    )(qh, kh, kh, ve, ve, vo, vo, _swa_mask())


def _layer_norm(y, g, b):
    mu = jnp.mean(y, axis=-1, keepdims=True)
    d = y - mu
    var = jnp.mean(d * d, axis=-1, keepdims=True)
    return d * lax.rsqrt(var + LN_EPS) * g + b


def _merge_kernel(x_ref, oa_ref, ob_ref, ga_ref, gb_ref, wa_ref, wb_ref, wo_ref,
                  g_ref, b_ref, h_ref):
    y_a = _dot(oa_ref[...], wa_ref[...])
    y_b = _dot(ob_ref[...], wb_ref[...])
    merged = (jax.nn.sigmoid(ga_ref[...].astype(F32)) * y_a
              + jax.nn.sigmoid(gb_ref[...].astype(F32)) * y_b)
    mix = _dot(merged.astype(BF16), wo_ref[...])
    h_ref[...] = _layer_norm(DEEPNORM_ALPHA * x_ref[...] + mix, g_ref[...], b_ref[...])


def _merge(x2, oa, ob, ga, gb, wa, wb, wo, ln_g, ln_b, tm):
    m = x2.shape[0]
    row = lambda c: pl.BlockSpec((tm, c), lambda i: (i, 0))
    sq = _resident((D_MODEL, D_MODEL))
    vec = _resident((1, D_MODEL))
    return pl.pallas_call(
        _merge_kernel,
        grid=(m // tm,),
        in_specs=[row(D_MODEL)] * 5 + [sq, sq, sq, vec, vec],
        out_specs=row(D_MODEL),
        out_shape=jax.ShapeDtypeStruct((m, D_MODEL), F32),
        compiler_params=pltpu.CompilerParams(
            dimension_semantics=("arbitrary",), vmem_limit_bytes=VMEM_LIMIT),
        name="merge",
    )(x2, oa, ob, ga, gb, wa, wb, wo, ln_g, ln_b)


def _ffn_kernel(h_ref, wg_ref, wu_ref, wd_ref, g_ref, b_ref, o_ref, *, ff_chunk):
    h = h_ref[...]
    hb = h.astype(BF16)
    d_ff = wg_ref.shape[1]
    acc = None
    for j in range(d_ff // ff_chunk):
        sl = slice(j * ff_chunk, (j + 1) * ff_chunk)
        gate = _dot(hb, wg_ref[:, sl])
        up = _dot(hb, wu_ref[:, sl])
        act = (gate * jax.nn.sigmoid(gate) * up).astype(BF16)
        part = _dot(act, wd_ref[sl, :])
        acc = part if acc is None else acc + part
    o_ref[...] = _layer_norm(DEEPNORM_ALPHA * h + acc, g_ref[...], b_ref[...])


def _ffn(h1, wg, wu, wd, ln_g, ln_b, tm, ff_chunk):
    m = h1.shape[0]
    d_ff = wg.shape[1]
    row = pl.BlockSpec((tm, D_MODEL), lambda i: (i, 0))
    vec = _resident((1, D_MODEL))
    return pl.pallas_call(
        functools.partial(_ffn_kernel, ff_chunk=ff_chunk),
        grid=(m // tm,),
        in_specs=[row, _resident((D_MODEL, d_ff)), _resident((D_MODEL, d_ff)),
                  _resident((d_ff, D_MODEL)), vec, vec],
        out_specs=row,
        out_shape=jax.ShapeDtypeStruct((m, D_MODEL), F32),
        compiler_params=pltpu.CompilerParams(
            dimension_semantics=("arbitrary",), vmem_limit_bytes=VMEM_LIMIT),
        name="ffn",
    )(h1, wg, wu, wd, ln_g, ln_b)


def _rope_inv_freq_lanes():
    half = ROPE_DIM // 2
    inv_freq = ROPE_THETA ** (-jnp.arange(0, ROPE_DIM, 2, dtype=F32) / ROPE_DIM)
    per_head = jnp.concatenate(
        [inv_freq, inv_freq, jnp.zeros((SWA_HEAD_DIM - 2 * half,), F32)])
    return jnp.tile(per_head, LANES // SWA_HEAD_DIM)[None, :]


def _layer(h2, pos_b, invf, batch, seq, w_in, w_decay_up, b_decay, gla_norm_g,
           w_branch_a, w_branch_b, sinks, w_out, ln1_g, ln1_b,
           w_ffn_gate, w_ffn_up, w_ffn_down, ln2_g, ln2_b):
    o_dl = 2 * GLA_QK + 2 * GLA_V
    o_qb = o_dl + GLA_RANK
    o_kb = o_qb + SWA_Q
    o_ga = o_kb + 2 * SWA_KV
    w_in_r = jnp.concatenate(
        [w_in[:, :o_dl], w_in[:, o_qb:o_kb], w_in[:, o_kb:o_ga], w_in[:, o_dl:o_qb],
         jnp.zeros((D_MODEL, RANK_PAD - GLA_RANK), w_in.dtype), w_in[:, o_ga:]],
        axis=1).astype(BF16)
    w_up_p = jnp.concatenate(
        [w_decay_up, jnp.zeros((RANK_PAD - GLA_RANK, GLA_QK), w_decay_up.dtype)],
        axis=0).astype(BF16)

    neg_sink = -sinks.astype(F32) * LOG2_E
    sink_hi = neg_sink.astype(BF16).astype(F32)
    sink_feat = jnp.zeros((SWA_HEADS, LANES), F32)
    sink_feat = sink_feat.at[:, SWA_HEAD_DIM].set(sink_hi)
    sink_feat = sink_feat.at[:, SWA_HEAD_DIM + 1].set(neg_sink - sink_hi)

    qa, ka, va, ra, la, qh, kh, ve, vo, ga, gb = _inproj(
        h2, pos_b, invf, sink_feat, w_in_r, w_up_p, b_decay[None, :], tm=512)
    oa = _gla(qa, ka, va, ra, la, gla_norm_g[None, :], batch, seq, tt=512)
    ob = _swa(qh, kh, ve, vo, batch, seq, blocks_per_step=4)
    h1 = _merge(h2, oa, ob, ga, gb, w_branch_a.astype(BF16), w_branch_b.astype(BF16),
                w_out.astype(BF16), ln1_g[None, :], ln1_b[None, :], tm=512)
    return _ffn(h1, w_ffn_gate.astype(BF16), w_ffn_up.astype(BF16), w_ffn_down.astype(BF16),
                ln2_g[None, :], ln2_b[None, :], tm=1024, ff_chunk=256)


def kernel(x, positions, w_in, w_decay_up, b_decay, gla_norm_g, w_branch_a, w_branch_b, sinks,
           w_out, ln1_g, ln1_b, w_ffn_gate, w_ffn_up, w_ffn_down, ln2_g, ln2_b):
    batch, seq, d = x.shape
    m = batch * seq
    h2 = x.reshape(m, d)
    pos_b = jnp.broadcast_to(positions.reshape(m, 1).astype(F32), (m, LANES))
    invf = _rope_inv_freq_lanes()
    for layer in range(w_in.shape[0]):
        h2 = _layer(h2, pos_b, invf, batch, seq, w_in[layer], w_decay_up[layer], b_decay[layer],
                    gla_norm_g[layer], w_branch_a[layer], w_branch_b[layer], sinks[layer],
                    w_out[layer], ln1_g[layer], ln1_b[layer], w_ffn_gate[layer],
                    w_ffn_up[layer], w_ffn_down[layer], ln2_g[layer], ln2_b[layer])
    return h2.reshape(batch, seq, d)
```

```python
import functools

import math

import jax
import jax.numpy as jnp
import numpy as np
from jax import lax
from jax.experimental import pallas as pl
from jax.experimental.pallas import tpu as pltpu

D_MODEL = 1024
GLA_HEADS = 4
GLA_DK = 128
GLA_DV = 256
GLA_RANK = 16
GLA_TAU = 16.0
GLA_CHUNK = 64
SWA_HEADS = 16
SWA_KV_HEADS = 2
SWA_HEAD_DIM = 64
SWA_WINDOW = 128
ROPE_DIM = 16
ROPE_THETA = 500000.0
DEPTH = 1
DEEPNORM_ALPHA = (2.0 * DEPTH) ** 0.25
LN_EPS = 1e-5
RMS_EPS = 1e-6

GLA_QK = GLA_HEADS * GLA_DK
GLA_V = GLA_HEADS * GLA_DV
SWA_Q = SWA_HEADS * SWA_HEAD_DIM
SWA_KV = SWA_KV_HEADS * SWA_HEAD_DIM

LANES = 128
RANK_PAD = LANES
VMEM_LIMIT = 56 * 1024 * 1024
LOG2_E = math.log2(math.e)
MASK_BIAS = -1e30

_C_QA = 0
_C_KA = _C_QA + GLA_QK
_C_VA = _C_KA + GLA_QK
_C_RA = _C_VA + GLA_V
_C_QB = _C_RA + GLA_V
_C_KVD = _C_QB + SWA_Q
_C_GA = _C_KVD + 2 * SWA_KV + RANK_PAD
_C_GB = _C_GA + D_MODEL
_C_END = _C_GB + D_MODEL

F32 = jnp.float32
BF16 = jnp.bfloat16


def _dot(a, b):
    return jnp.dot(a, b, preferred_element_type=F32)


def _dot_nt(a, b):
    return lax.dot_general(a, b, (((1,), (1,)), ((), ())), preferred_element_type=F32)


def _dot_tn(a, b):
    return lax.dot_general(a, b, (((0,), (0,)), ((), ())), preferred_element_type=F32)


def _resident(shape):
    return pl.BlockSpec(shape, lambda *_: (0,) * len(shape), pipeline_mode=pl.Buffered(1))


def _inproj_kernel(x_ref, pos_ref, invf_ref, sf_ref, w_ref, wup_ref, bdec_ref,
                   qa_ref, ka_ref, va_ref, ra_ref, la_ref,
                   qh_ref, kh_ref, ve_ref, vo_ref, ga_ref, gb_ref):
    xb = x_ref[...].astype(BF16)

    def proj(lo, hi):
        return _dot(xb, w_ref[:, lo:hi])

    qa_ref[...] = (proj(_C_QA, _C_KA) * (GLA_DK ** -0.5)).astype(BF16)
    ka_ref[...] = proj(_C_KA, _C_VA).astype(BF16)
    va_ref[...] = proj(_C_VA, _C_RA).astype(BF16)
    ra_ref[...] = proj(_C_RA, _C_QB).astype(BF16)
    ga_ref[...] = proj(_C_GA, _C_GB).astype(BF16)
    gb_ref[...] = proj(_C_GB, _C_END).astype(BF16)

    ang = pos_ref[...] * invf_ref[...]
    cos = jnp.cos(ang)
    sin = jnp.sin(ang)
    lane = lax.broadcasted_iota(jnp.int32, ang.shape, 1) & (SWA_HEAD_DIM - 1)
    half = ROPE_DIM // 2
    a_tab = jnp.where(lane < half, -sin, 0.0)
    b_tab = jnp.where(lane >= half, sin, 0.0)

    def rope(t):
        return (t * cos + pltpu.roll(t, LANES - half, 1) * a_tab
                + pltpu.roll(t, half, 1) * b_tab)

    dh = SWA_HEAD_DIM
    lane_full = lax.broadcasted_iota(jnp.int32, ang.shape, 1)
    low = lane_full < dh
    q_scale = (dh ** -0.5) * LOG2_E
    qb = proj(_C_QB, _C_KVD)
    for c in range(SWA_Q // LANES):
        t = rope(qb[:, c * LANES:(c + 1) * LANES]) * q_scale
        qh_ref[2 * c] = jnp.where(low, t, sf_ref[2 * c:2 * c + 1, :]).astype(BF16)
        qh_ref[2 * c + 1] = jnp.where(
            low, pltpu.roll(t, dh, 1), sf_ref[2 * c + 1:2 * c + 2, :]).astype(BF16)

    kvd = proj(_C_KVD, _C_GA)
    kk = rope(kvd[:, :SWA_KV])
    vv = kvd[:, SWA_KV:2 * SWA_KV]
    kk_sw = pltpu.roll(kk, dh, 1)
    vv_sw = pltpu.roll(vv, dh, 1)
    one_k = jnp.where(lane_full < dh + 2, 1.0, 0.0)
    one_e = jnp.where(lane_full == dh, 1.0, 0.0)
    one_o = jnp.where(lane_full == 0, 1.0, 0.0)
    kh_ref[0] = jnp.where(low, kk, one_k).astype(BF16)
    kh_ref[1] = jnp.where(low, kk_sw, one_k).astype(BF16)
    ve_ref[0] = jnp.where(low, vv, one_e).astype(BF16)
    ve_ref[1] = jnp.where(low, vv_sw, one_e).astype(BF16)
    vo_ref[0] = jnp.where(low, one_o, vv_sw).astype(BF16)
    vo_ref[1] = jnp.where(low, one_o, vv).astype(BF16)
    d_low = kvd[:, 2 * SWA_KV:]
    z = _dot(d_low.astype(BF16), wup_ref[...]) + bdec_ref[...]
    log_sig = jnp.minimum(z, 0.0) - jnp.log1p(jnp.exp(-jnp.abs(z)))
    la_ref[...] = log_sig * (LOG2_E / GLA_TAU)


def _inproj(x2, pos_b, invf, sink_feat, w_in_r, w_up_p, b_dec, tm):
    m = x2.shape[0]
    row = lambda c: pl.BlockSpec((tm, c), lambda i: (i, 0))
    row_out = lambda c, d: (row(c), jax.ShapeDtypeStruct((m, c), d))
    slab_out = lambda n: (pl.BlockSpec((n, tm, LANES), lambda i: (0, i, 0)),
                          jax.ShapeDtypeStruct((n, m, LANES), BF16))
    outs = [row_out(GLA_QK, BF16), row_out(GLA_QK, BF16), row_out(GLA_V, BF16),
            row_out(GLA_V, BF16), row_out(GLA_QK, F32),
            slab_out(SWA_HEADS), slab_out(SWA_KV_HEADS), slab_out(SWA_KV_HEADS),
            slab_out(SWA_KV_HEADS), row_out(D_MODEL, BF16), row_out(D_MODEL, BF16)]
    return pl.pallas_call(
        _inproj_kernel,
        grid=(m // tm,),
        in_specs=[row(D_MODEL), row(LANES), _resident((1, LANES)),
                  _resident((SWA_HEADS, LANES)),
                  _resident((D_MODEL, _C_END)), _resident((RANK_PAD, GLA_QK)),
                  _resident((1, GLA_QK))],
        out_specs=[o[0] for o in outs],
        out_shape=[o[1] for o in outs],
        compiler_params=pltpu.CompilerParams(
            dimension_semantics=("arbitrary",), vmem_limit_bytes=VMEM_LIMIT),
        name="inproj",
    )(x2, pos_b, invf, sink_feat, w_in_r, w_up_p, b_dec)


def _gla_kernel(q_ref, k_ref, v_ref, r_ref, la_ref, g_ref, o_ref, st_ref, *, n_chunks):
    @pl.when(pl.program_id(1) == 0)
    def _():
        st_ref[...] = jnp.zeros_like(st_ref)

    c_ = GLA_CHUNK
    causal = (lax.broadcasted_iota(jnp.int32, (c_, c_), 0)
              >= lax.broadcasted_iota(jnp.int32, (c_, c_), 1))
    tri = jnp.where(causal, 1.0, 0.0).astype(BF16)
    gain = g_ref[...]

    items =[(c, h) for c in range(n_chunks) for h in range(GLA_HEADS)]
    rows = lambda c: slice(c * c_, (c + 1) * c_)
    ks = lambda h: slice(h * GLA_DK, (h + 1) * GLA_DK)
    vs = lambda h: slice(h * GLA_DV, (h + 1) * GLA_DV)

    b = []
    for c in range(n_chunks):
        la = la_ref[rows(c), :]
        hi = la.astype(BF16)
        rem = la - hi.astype(F32)
        mid = rem.astype(BF16)
        lo = (rem - mid.astype(F32)).astype(BF16)
        b.append(_dot(tri, hi) + _dot(tri, mid) + _dot(tri, lo))

    q_in, scores, contrib, decay = {}, {}, {}, {}
    for c, h in items:
        bh = b[c][:, ks(h)]
        b_last = bh[c_ - 1:c_, :]
        q = q_ref[rows(c), ks(h)].astype(F32)
        k = k_ref[rows(c), ks(h)].astype(F32)
        q_in[c, h] = (q * jnp.exp2(bh)).astype(BF16)
        k_in = (k * jnp.exp2(-bh)).astype(BF16)
        k_out = (k * jnp.exp2(b_last - bh)).astype(BF16)
        scores[c, h] = _dot_nt(q_in[c, h], k_in)
        contrib[c, h] = _dot_tn(k_out, v_ref[rows(c), vs(h)])
        dec = jnp.exp2(jnp.broadcast_to(b_last, (GLA_DK, GLA_DK)).T)
        decay[c, h] = jnp.concatenate([dec] * (GLA_DV // GLA_DK), axis=1)

    o_intra = {}
    for c, h in items:
        masked = jnp.where(causal, scores[c, h], 0.0).astype(BF16)
        o_intra[c, h] = _dot(masked, v_ref[rows(c), vs(h)])

    st = [st_ref[h] for h in range(GLA_HEADS)]
    for c, h in items:
        o = o_intra[c, h] + _dot(q_in[c, h], st[h].astype(BF16))
        st[h] = st[h] * decay[c, h] + contrib[c, h]
        o = o * lax.rsqrt(jnp.mean(o * o, axis=-1, keepdims=True) + RMS_EPS)
        r = r_ref[rows(c), vs(h)].astype(F32)
        o = o * gain * (r * jax.nn.sigmoid(r))
        o_ref[rows(c), vs(h)] = o.astype(BF16)
    for h in range(GLA_HEADS):
        st_ref[h] = st[h]


def _gla(qa, ka, va, ra, la, gain, batch, seq, tt):
    m = qa.shape[0]
    nt = seq // tt
    row = lambda c: pl.BlockSpec((tt, c), lambda b, t: (b * nt + t, 0))
    return pl.pallas_call(
        functools.partial(_gla_kernel, n_chunks=tt // GLA_CHUNK),
        grid=(batch, nt),
        in_specs=[row(GLA_QK), row(GLA_QK), row(GLA_V), row(GLA_V), row(GLA_QK),
                  pl.BlockSpec((1, GLA_DV), lambda b, t: (0, 0))],
        out_specs=row(GLA_V),
        out_shape=jax.ShapeDtypeStruct((m, GLA_V), BF16),
        scratch_shapes=[pltpu.VMEM((GLA_HEADS, GLA_DK, GLA_DV), F32)],
        compiler_params=pltpu.CompilerParams(
            dimension_semantics=("arbitrary", "arbitrary"), vmem_limit_bytes=VMEM_LIMIT),
        name="gla",
    )(qa, ka, va, ra, la, gain)


def _swa_kernel(q_ref, kc_ref, kp_ref, vec_ref, vep_ref, voc_ref, vop_ref, mask_ref, o_ref,
                *, blocks_per_step):
    w = SWA_WINDOW
    dh = SWA_HEAD_DIM
    per_group = SWA_HEADS // SWA_KV_HEADS
    first = pl.program_id(1) == 0
    row = lax.broadcasted_iota(jnp.int32, (w, LANES), 0)
    lane = lax.broadcasted_iota(jnp.int32, (w, LANES), 1)
    ident = jnp.where(row == lane, 1.0, 0.0).astype(BF16)
    sink_row = row == 0
    low = lane < dh
    zero = jnp.zeros((w, LANES), BF16)
    one_e = jnp.where(lane == dh, 1.0, 0.0).astype(BF16)
    one_o = jnp.where(lane == 0, 1.0, 0.0).astype(BF16)
    order = list(range(0, per_group, 2)) + list(range(1, per_group, 2))

    scores = []
    values = []
    for qb in range(blocks_per_step):
        rows = slice(qb * w, (qb + 1) * w)
        prev_rows = slice((qb - 1) * w, qb * w)
        mask = mask_ref[1] if qb else jnp.where(first, mask_ref[0], mask_ref[1])
        for g in range(SWA_KV_HEADS):
            k_prev = kc_ref[g, prev_rows] if qb else kp_ref[g]
            ve_prev = vec_ref[g, prev_rows] if qb else vep_ref[g]
            vo_prev = voc_ref[g, prev_rows] if qb else vop_ref[g]
            k2 = jnp.concatenate([jnp.where(sink_row, zero, k_prev), kc_ref[g, rows]], axis=0)
            kx = jnp.concatenate([k2, mask], axis=1)
            ve = jnp.concatenate([jnp.where(sink_row, one_e, ve_prev), vec_ref[g, rows]], axis=0)
            vo = jnp.concatenate([jnp.where(sink_row, one_o, vo_prev), voc_ref[g, rows]], axis=0)
            for part, vv in ((order[:per_group // 2], ve), (order[per_group // 2:], vo)):
                qs = jnp.concatenate(
                    [jnp.concatenate([q_ref[g * per_group + h, rows], ident], axis=1)
                     for h in part], axis=0)
                scores.append(_dot_nt(qs, kx))
                values.append(vv)
    pv = []
    for s, vv in zip(scores, values):
        p = jnp.exp2(s - jnp.max(s, axis=-1, keepdims=True)).astype(BF16)
        pv.append(_dot(p, vv))
    for qb in range(blocks_per_step):
        rows = slice(qb * w, (qb + 1) * w)
        for g in range(SWA_KV_HEADS):
            pv_e = pv[(qb * SWA_KV_HEADS + g) * 2]
            pv_o = pv[(qb * SWA_KV_HEADS + g) * 2 + 1]
            for c in range(per_group // 2):
                e = pv_e[c * w:(c + 1) * w]
                o = pv_o[c * w:(c + 1) * w]
                out = jnp.where(low, e * (1.0 / e[:, dh:dh + 1]), o * (1.0 / o[:, 0:1]))
                col = (g * (per_group // 2) + c) * LANES
                o_ref[rows, col:col + LANES] = out.astype(BF16)


def _swa_mask():
    w = SWA_WINDOW
    j = np.arange(2 * w)[:, None]
    i = np.arange(w)[None, :]
    band = (j > i) & (j <= i + w)
    later = band | (j == 0)
    first = (band & (j >= w)) | (j == 0)
    both = np.stack([first, later])
    return jnp.asarray(np.where(both, 0.0, MASK_BIAS), dtype=BF16)


def _swa(qh, kh, ve, vo, batch, seq, blocks_per_step):
    m = kh.shape[1]
    w = SWA_WINDOW
    tq = blocks_per_step * w
    steps = seq // tq
    kv = SWA_KV_HEADS
    cur = lambda n_: pl.BlockSpec((n_, tq, LANES), lambda b, t: (0, b * steps + t, 0))
    prev = lambda n_: pl.BlockSpec(
        (n_, w, LANES),
        lambda b, t: (0, b * (seq // w) + jnp.maximum(t * blocks_per_step - 1, 0), 0))
    return pl.pallas_call(
        functools.partial(_swa_kernel, blocks_per_step=blocks_per_step),
        grid=(batch, steps),
        in_specs=[cur(SWA_HEADS), cur(kv), prev(kv), cur(kv), prev(kv), cur(kv), prev(kv),
                  _resident((2, 2 * w, w))],
        out_specs=pl.BlockSpec((tq, SWA_Q), lambda b, t: (b * steps + t, 0)),
        out_shape=jax.ShapeDtypeStruct((m, SWA_Q), BF16),
        compiler_params=pltpu.CompilerParams(
            dimension_semantics=("arbitrary", "arbitrary"), vmem_limit_bytes=VMEM_LIMIT),
        name="swa",
    )(qh, kh, kh, ve, ve, vo, vo, _swa_mask())


def _layer_norm(y, g, b):
    mu = jnp.mean(y, axis=-1, keepdims=True)
    d = y - mu
    var = jnp.mean(d * d, axis=-1, keepdims=True)
    return d * lax.rsqrt(var + LN_EPS) * g + b


def _merge_kernel(x_ref, oa_ref, ob_ref, ga_ref, gb_ref, wa_ref, wb_ref, wo_ref,
                  g_ref, b_ref, h_ref):
    y_a = _dot(oa_ref[...], wa_ref[...])
    y_b = _dot(ob_ref[...], wb_ref[...])
    merged = (jax.nn.sigmoid(ga_ref[...].astype(F32)) * y_a
              + jax.nn.sigmoid(gb_ref[...].astype(F32)) * y_b)
    mix = _dot(merged.astype(BF16), wo_ref[...])
    h_ref[...] = _layer_norm(DEEPNORM_ALPHA * x_ref[...] + mix, g_ref[...], b_ref[...])


def _merge(x2, oa, ob, ga, gb, wa, wb, wo, ln_g, ln_b, tm):
    m = x2.shape[0]
    row = lambda c: pl.BlockSpec((tm, c), lambda i: (i, 0))
    sq = _resident((D_MODEL, D_MODEL))
    vec = _resident((1, D_MODEL))
    return pl.pallas_call(
        _merge_kernel,
        grid=(m // tm,),
        in_specs=[row(D_MODEL)] * 5 + [sq, sq, sq, vec, vec],
        out_specs=row(D_MODEL),
        out_shape=jax.ShapeDtypeStruct((m, D_MODEL), F32),
        compiler_params=pltpu.CompilerParams(
            dimension_semantics=("arbitrary",), vmem_limit_bytes=VMEM_LIMIT),
        name="merge",
    )(x2, oa, ob, ga, gb, wa, wb, wo, ln_g, ln_b)


def _ffn_kernel(h_ref, wg_ref, wu_ref, wd_ref, g_ref, b_ref, o_ref, *, ff_chunk):
    h = h_ref[...]
    hb = h.astype(BF16)
    d_ff = wg_ref.shape[1]
    acc = None
    for j in range(d_ff // ff_chunk):
        sl = slice(j * ff_chunk, (j + 1) * ff_chunk)
        gate = _dot(hb, wg_ref[:, sl])
        up = _dot(hb, wu_ref[:, sl])
        act = (gate * jax.nn.sigmoid(gate) * up).astype(BF16)
        part = _dot(act, wd_ref[sl, :])
        acc = part if acc is None else acc + part
    o_ref[...] = _layer_norm(DEEPNORM_ALPHA * h + acc, g_ref[...], b_ref[...])


def _ffn(h1, wg, wu, wd, ln_g, ln_b, tm, ff_chunk):
    m = h1.shape[0]
    d_ff = wg.shape[1]
    row = pl.BlockSpec((tm, D_MODEL), lambda i: (i, 0))
    vec = _resident((1, D_MODEL))
    return pl.pallas_call(
        functools.partial(_ffn_kernel, ff_chunk=ff_chunk),
        grid=(m // tm,),
        in_specs=[row, _resident((D_MODEL, d_ff)), _resident((D_MODEL, d_ff)),
                  _resident((d_ff, D_MODEL)), vec, vec],
        out_specs=row,
        out_shape=jax.ShapeDtypeStruct((m, D_MODEL), F32),
        compiler_params=pltpu.CompilerParams(
            dimension_semantics=("arbitrary",), vmem_limit_bytes=VMEM_LIMIT),
        name="ffn",
    )(h1, wg, wu, wd, ln_g, ln_b)


def _rope_inv_freq_lanes():
    half = ROPE_DIM // 2
    inv_freq = ROPE_THETA ** (-jnp.arange(0, ROPE_DIM, 2, dtype=F32) / ROPE_DIM)
    per_head = jnp.concatenate(
        [inv_freq, inv_freq, jnp.zeros((SWA_HEAD_DIM - 2 * half,), F32)])
    return jnp.tile(per_head, LANES // SWA_HEAD_DIM)[None, :]


def _layer(h2, pos_b, invf, batch, seq, w_in, w_decay_up, b_decay, gla_norm_g,
           w_branch_a, w_branch_b, sinks, w_out, ln1_g, ln1_b,
           w_ffn_gate, w_ffn_up, w_ffn_down, ln2_g, ln2_b):
    o_dl = 2 * GLA_QK + 2 * GLA_V
    o_qb = o_dl + GLA_RANK
    o_kb = o_qb + SWA_Q
    o_ga = o_kb + 2 * SWA_KV
    w_in_r = jnp.concatenate(
        [w_in[:, :o_dl], w_in[:, o_qb:o_kb], w_in[:, o_kb:o_ga], w_in[:, o_dl:o_qb],
         jnp.zeros((D_MODEL, RANK_PAD - GLA_RANK), w_in.dtype), w_in[:, o_ga:]],
        axis=1).astype(BF16)
    w_up_p = jnp.concatenate(
        [w_decay_up, jnp.zeros((RANK_PAD - GLA_RANK, GLA_QK), w_decay_up.dtype)],
        axis=0).astype(BF16)

    neg_sink = -sinks.astype(F32) * LOG2_E
    sink_hi = neg_sink.astype(BF16).astype(F32)
    sink_feat = jnp.zeros((SWA_HEADS, LANES), F32)
    sink_feat = sink_feat.at[:, SWA_HEAD_DIM].set(sink_hi)
    sink_feat = sink_feat.at[:, SWA_HEAD_DIM + 1].set(neg_sink - sink_hi)

    qa, ka, va, ra, la, qh, kh, ve, vo, ga, gb = _inproj(
        h2, pos_b, invf, sink_feat, w_in_r, w_up_p, b_decay[None, :], tm=512)
    oa = _gla(qa, ka, va, ra, la, gla_norm_g[None, :], batch, seq, tt=512)
    ob = _swa(qh, kh, ve, vo, batch, seq, blocks_per_step=4)
    h1 = _merge(h2, oa, ob, ga, gb, w_branch_a.astype(BF16), w_branch_b.astype(BF16),
                w_out.astype(BF16), ln1_g[None, :], ln1_b[None, :], tm=512)
    return _ffn(h1, w_ffn_gate.astype(BF16), w_ffn_up.astype(BF16), w_ffn_down.astype(BF16),
                ln2_g[None, :], ln2_b[None, :], tm=1024, ff_chunk=256)


def kernel(x, positions, w_in, w_decay_up, b_decay, gla_norm_g, w_branch_a, w_branch_b, sinks,
           w_out, ln1_g, ln1_b, w_ffn_gate, w_ffn_up, w_ffn_down, ln2_g, ln2_b):
    batch, seq, d = x.shape
    m = batch * seq
    h2 = x.reshape(m, d)
    pos_b = jnp.broadcast_to(positions.reshape(m, 1).astype(F32), (m, LANES))
    invf = _rope_inv_freq_lanes()
    for layer in range(w_in.shape[0]):
        h2 = _layer(h2, pos_b, invf, batch, seq, w_in[layer], w_decay_up[layer], b_decay[layer],
                    gla_norm_g[layer], w_branch_a[layer], w_branch_b[layer], sinks[layer],
                    w_out[layer], ln1_g[layer], ln1_b[layer], w_ffn_gate[layer],
                    w_ffn_up[layer], w_ffn_down[layer], ln2_g[layer], ln2_b[layer])
    return h2.reshape(batch, seq, d)
```

```python
import functools

import math

import jax
import jax.numpy as jnp
import numpy as np
from jax import lax
from jax.experimental import pallas as pl
from jax.experimental.pallas import tpu as pltpu

D_MODEL = 1024
GLA_HEADS = 4
GLA_DK = 128
GLA_DV = 256
GLA_RANK = 16
GLA_TAU = 16.0
GLA_CHUNK = 64
SWA_HEADS = 16
SWA_KV_HEADS = 2
SWA_HEAD_DIM = 64
SWA_WINDOW = 128
ROPE_DIM = 16
ROPE_THETA = 500000.0
DEPTH = 1
DEEPNORM_ALPHA = (2.0 * DEPTH) ** 0.25
LN_EPS = 1e-5
RMS_EPS = 1e-6

GLA_QK = GLA_HEADS * GLA_DK
GLA_V = GLA_HEADS * GLA_DV
SWA_Q = SWA_HEADS * SWA_HEAD_DIM
SWA_KV = SWA_KV_HEADS * SWA_HEAD_DIM

LANES = 128
MXU_COLS = 256
RANK_PAD = LANES
VMEM_LIMIT = 56 * 1024 * 1024
MERGE_SPLIT = 2
FFN_SPLIT = 4
LOG2_E = math.log2(math.e)
MASK_BIAS = -1e30

_C_QA = 0
_C_KA = _C_QA + GLA_QK
_C_VA = _C_KA + GLA_QK
_C_RA = _C_VA + GLA_V
_C_QB = _C_RA + GLA_V
_C_KVD = _C_QB + SWA_Q
_C_GA = _C_KVD + 2 * SWA_KV + RANK_PAD
_C_GB = _C_GA + D_MODEL
_C_END = _C_GB + D_MODEL

F32 = jnp.float32
BF16 = jnp.bfloat16


def _dot(a, b):
    return jnp.dot(a, b, preferred_element_type=F32)


def _dot_nt(a, b):
    return lax.dot_general(a, b, (((1,), (1,)), ((), ())), preferred_element_type=F32)


def _dot_tn(a, b):
    return lax.dot_general(a, b, (((0,), (0,)), ((), ())), preferred_element_type=F32)


def _resident(shape):
    return pl.BlockSpec(shape, lambda *_: (0,) * len(shape), pipeline_mode=pl.Buffered(1))


def _inproj_kernel(x_ref, pos_ref, invf_ref, spread_ref, sf_ref, w_ref, wup_ref, bdec_ref,
                   qa_ref, ka_ref, va_ref, ra_ref, la_ref,
                   qh_ref, kh_ref, ve_ref, vo_ref, ga_ref, gb_ref):
    xb = x_ref[...].astype(BF16)

    def proj(lo, hi):
        return _dot(xb, w_ref[:, lo:hi])

    tm = x_ref.shape[0]
    half = ROPE_DIM // 2
    ang = jnp.tile(invf_ref[...], (1, tm // LANES)) * pos_ref[...]
    one_row = jnp.where(lax.broadcasted_iota(jnp.int32, (2 * half, tm), 0) == 0, 1.0, 0.0)
    cs = jnp.concatenate([jnp.cos(ang), jnp.sin(ang), one_row], axis=0)
    cs_hi = cs.astype(BF16)
    cs_lo = (cs - cs_hi.astype(F32)).astype(BF16)
    tabs = _dot_tn(jnp.concatenate([cs_hi, cs_lo], axis=0), spread_ref[...])
    cos, a_tab, b_tab = tabs[:, :LANES], tabs[:, LANES:2 * LANES], tabs[:, 2 * LANES:]

    def rope(t):
        return (t * cos + pltpu.roll(t, LANES - half, 1) * a_tab
                + pltpu.roll(t, half, 1) * b_tab)

    dh = SWA_HEAD_DIM
    lane_full = lax.broadcasted_iota(jnp.int32, (tm, LANES), 1)
    low = lane_full < dh
    q_scale = (dh ** -0.5) * LOG2_E

    def swa_queries(i):
        qb = proj(_C_QB + i * MXU_COLS, _C_QB + (i + 1) * MXU_COLS)
        for c in range(2 * i, 2 * i + 2):
            t = rope(qb[:, (c - 2 * i) * LANES:(c - 2 * i + 1) * LANES]) * q_scale
            qh_ref[2 * c] = jnp.where(low, t, sf_ref[2 * c:2 * c + 1, :]).astype(BF16)
            qh_ref[2 * c + 1] = jnp.where(
                low, pltpu.roll(t, dh, 1), sf_ref[2 * c + 1:2 * c + 2, :]).astype(BF16)

    def swa_keys_values_and_decay():
        kvd = proj(_C_KVD, _C_GA)
        kk = rope(kvd[:, :SWA_KV])
        vv = kvd[:, SWA_KV:2 * SWA_KV]
        kk_sw = pltpu.roll(kk, dh, 1)
        vv_sw = pltpu.roll(vv, dh, 1)
        one_k = jnp.where(lane_full < dh + 2, 1.0, 0.0)
        one_e = jnp.where(lane_full == dh, 1.0, 0.0)
        one_o = jnp.where(lane_full == 0, 1.0, 0.0)
        kh_ref[0] = jnp.where(low, kk, one_k).astype(BF16)
        kh_ref[1] = jnp.where(low, kk_sw, one_k).astype(BF16)
        ve_ref[0] = jnp.where(low, vv, one_e).astype(BF16)
        ve_ref[1] = jnp.where(low, vv_sw, one_e).astype(BF16)
        vo_ref[0] = jnp.where(low, one_o, vv_sw).astype(BF16)
        vo_ref[1] = jnp.where(low, one_o, vv).astype(BF16)
        return kvd[:, 2 * SWA_KV:].astype(BF16)

    def log_decay(d_low, h):
        cols = slice(h * GLA_DK, (h + 1) * GLA_DK)
        z = _dot(d_low, wup_ref[:, cols]) + bdec_ref[:, cols]
        log_sig = jnp.minimum(z, 0.0) - jnp.log1p(jnp.exp(-jnp.abs(z)))
        la_ref[:, cols] = log_sig * (LOG2_E / GLA_TAU)

    def plain(ref, col0, i, scale=None):
        y = proj(col0 + i * MXU_COLS, col0 + (i + 1) * MXU_COLS)
        if scale is not None:
            y = y * scale
        ref[:, i * MXU_COLS:(i + 1) * MXU_COLS] = y.astype(BF16)

    plain_tiles = (
        [(qa_ref, _C_QA, i, GLA_DK ** -0.5) for i in range(GLA_QK // MXU_COLS)]
        + [(ka_ref, _C_KA, i, None) for i in range(GLA_QK // MXU_COLS)]
        + [(va_ref, _C_VA, i, None) for i in range(GLA_V // MXU_COLS)]
        + [(ra_ref, _C_RA, i, None) for i in range(GLA_V // MXU_COLS)]
        + [(ga_ref, _C_GA, i, None) for i in range(D_MODEL // MXU_COLS)]
        + [(gb_ref, _C_GB, i, None) for i in range(D_MODEL // MXU_COLS)])
    d_low = swa_keys_values_and_decay()
    heavy = []
    for i in range(max(SWA_Q // MXU_COLS, GLA_HEADS)):
        if i < SWA_Q // MXU_COLS:
            heavy.append(functools.partial(swa_queries, i))
        if i < GLA_HEADS:
            heavy.append(functools.partial(log_decay, d_low, i))
    per_heavy = len(plain_tiles) // len(heavy)
    for n, task in enumerate(heavy):
        task()
        first = n * per_heavy
        last = len(plain_tiles) if n == len(heavy) - 1 else first + per_heavy
        for args in plain_tiles[first:last]:
            plain(*args)


def _rope_spread():
    half = ROPE_DIM // 2
    rows = 4 * half
    e = np.zeros((rows, 3 * LANES), np.float32)
    for lane in range(LANES):
        j = lane % SWA_HEAD_DIM
        if j < half:
            e[j, lane] = 1.0
            e[half + j, LANES + lane] = -1.0
        elif j < 2 * half:
            e[j - half, lane] = 1.0
            e[j, 2 * LANES + lane] = 1.0
        else:
            e[2 * half, lane] = 1.0
    return jnp.asarray(np.concatenate([e, e], axis=0), dtype=BF16)


def _inproj(x2, pos_row, invf, sink_feat, w_in_r, w_up_p, b_dec, tm):
    m = x2.shape[0]
    half = ROPE_DIM // 2
    row = lambda c: pl.BlockSpec((tm, c), lambda i: (i, 0))
    row_out = lambda c, d: (row(c), jax.ShapeDtypeStruct((m, c), d))
    slab_out = lambda n: (pl.BlockSpec((n, tm, LANES), lambda i: (0, i, 0)),
                          jax.ShapeDtypeStruct((n, m, LANES), BF16))
    outs = [row_out(GLA_QK, BF16), row_out(GLA_QK, BF16), row_out(GLA_V, BF16),
            row_out(GLA_V, BF16), row_out(GLA_QK, F32),
            slab_out(SWA_HEADS), slab_out(SWA_KV_HEADS), slab_out(SWA_KV_HEADS),
            slab_out(SWA_KV_HEADS), row_out(D_MODEL, BF16), row_out(D_MODEL, BF16)]
    return pl.pallas_call(
        _inproj_kernel,
        grid=(m // tm,),
        in_specs=[row(D_MODEL), pl.BlockSpec((1, tm), lambda i: (0, i)),
                  _resident((half, LANES)), _resident((8 * half, 3 * LANES)),
                  _resident((SWA_HEADS, LANES)),
                  _resident((D_MODEL, _C_END)), _resident((RANK_PAD, GLA_QK)),
                  _resident((1, GLA_QK))],
        out_specs=[o[0] for o in outs],
        out_shape=[o[1] for o in outs],
        compiler_params=pltpu.CompilerParams(
            dimension_semantics=("arbitrary",), vmem_limit_bytes=VMEM_LIMIT),
        name="inproj",
    )(x2, pos_row, invf, _rope_spread(), sink_feat, w_in_r, w_up_p, b_dec)


def _gla_kernel(q_ref, k_ref, v_ref, r_ref, la_ref, g_ref, o_ref, st_ref, *, n_chunks):
    @pl.when(pl.program_id(1) == 0)
    def _():
        st_ref[...] = jnp.zeros_like(st_ref)

    c_ = GLA_CHUNK
    causal = (lax.broadcasted_iota(jnp.int32, (c_, c_), 0)
              >= lax.broadcasted_iota(jnp.int32, (c_, c_), 1))
    tri = jnp.where(causal, 1.0, 0.0).astype(BF16)
    gain = g_ref[...]

    items =[(c, h) for c in range(n_chunks) for h in range(GLA_HEADS)]
    rows = lambda c: slice(c * c_, (c + 1) * c_)
    ks = lambda h: slice(h * GLA_DK, (h + 1) * GLA_DK)
    vs = lambda h: slice(h * GLA_DV, (h + 1) * GLA_DV)

    b = []
    for c in range(n_chunks):
        la = la_ref[rows(c), :]
        hi = la.astype(BF16)
        rem = la - hi.astype(F32)
        mid = rem.astype(BF16)
        lo = (rem - mid.astype(F32)).astype(BF16)
        b.append(_dot(tri, hi) + _dot(tri, mid) + _dot(tri, lo))

    q_in, scores, contrib, decay = {}, {}, {}, {}
    for c, h in items:
        bh = b[c][:, ks(h)]
        b_last = bh[c_ - 1:c_, :]
        q = q_ref[rows(c), ks(h)].astype(F32)
        k = k_ref[rows(c), ks(h)].astype(F32)
        q_in[c, h] = (q * jnp.exp2(bh)).astype(BF16)
        k_in = (k * jnp.exp2(-bh)).astype(BF16)
        k_out = (k * jnp.exp2(b_last - bh)).astype(BF16)
        scores[c, h] = _dot_nt(q_in[c, h], k_in)
        contrib[c, h] = _dot_tn(k_out, v_ref[rows(c), vs(h)])
        dec = jnp.exp2(jnp.broadcast_to(b_last, (GLA_DK, GLA_DK)).T)
        decay[c, h] = jnp.concatenate([dec] * (GLA_DV // GLA_DK), axis=1)

    o_intra = {}
    for c, h in items:
        masked = jnp.where(causal, scores[c, h], 0.0).astype(BF16)
        o_intra[c, h] = _dot(masked, v_ref[rows(c), vs(h)])

    st = [st_ref[h] for h in range(GLA_HEADS)]
    for c, h in items:
        o = o_intra[c, h] + _dot(q_in[c, h], st[h].astype(BF16))
        st[h] = st[h] * decay[c, h] + contrib[c, h]
        o = o * lax.rsqrt(jnp.mean(o * o, axis=-1, keepdims=True) + RMS_EPS)
        r = r_ref[rows(c), vs(h)].astype(F32)
        o = o * gain * (r * jax.nn.sigmoid(r))
        o_ref[rows(c), vs(h)] = o.astype(BF16)
    for h in range(GLA_HEADS):
        st_ref[h] = st[h]


def _gla(qa, ka, va, ra, la, gain, batch, seq, tt):
    m = qa.shape[0]
    nt = seq // tt
    row = lambda c: pl.BlockSpec((tt, c), lambda b, t: (b * nt + t, 0))
    return pl.pallas_call(
        functools.partial(_gla_kernel, n_chunks=tt // GLA_CHUNK),
        grid=(batch, nt),
        in_specs=[row(GLA_QK), row(GLA_QK), row(GLA_V), row(GLA_V), row(GLA_QK),
                  pl.BlockSpec((1, GLA_DV), lambda b, t: (0, 0))],
        out_specs=row(GLA_V),
        out_shape=jax.ShapeDtypeStruct((m, GLA_V), BF16),
        scratch_shapes=[pltpu.VMEM((GLA_HEADS, GLA_DK, GLA_DV), F32)],
        compiler_params=pltpu.CompilerParams(
            dimension_semantics=("arbitrary", "arbitrary"), vmem_limit_bytes=VMEM_LIMIT),
        name="gla",
    )(qa, ka, va, ra, la, gain)


def _swa_kernel(q_ref, kc_ref, kp_ref, vec_ref, vep_ref, voc_ref, vop_ref, mask_ref, o_ref,
                *, blocks_per_step):
    w = SWA_WINDOW
    dh = SWA_HEAD_DIM
    per_group = SWA_HEADS // SWA_KV_HEADS
    first = pl.program_id(1) == 0
    row = lax.broadcasted_iota(jnp.int32, (w, LANES), 0)
    lane = lax.broadcasted_iota(jnp.int32, (w, LANES), 1)
    ident = jnp.where(row == lane, 1.0, 0.0).astype(BF16)
    sink_row = row == 0
    low = lane < dh
    zero = jnp.zeros((w, LANES), BF16)
    one_e = jnp.where(lane == dh, 1.0, 0.0).astype(BF16)
    one_o = jnp.where(lane == 0, 1.0, 0.0).astype(BF16)
    order = list(range(0, per_group, 2)) + list(range(1, per_group, 2))

    scores = []
    values = []
    for qb in range(blocks_per_step):
        rows = slice(qb * w, (qb + 1) * w)
        prev_rows = slice((qb - 1) * w, qb * w)
        mask = mask_ref[1] if qb else jnp.where(first, mask_ref[0], mask_ref[1])
        for g in range(SWA_KV_HEADS):
            k_prev = kc_ref[g, prev_rows] if qb else kp_ref[g]
            ve_prev = vec_ref[g, prev_rows] if qb else vep_ref[g]
            vo_prev = voc_ref[g, prev_rows] if qb else vop_ref[g]
            k2 = jnp.concatenate([jnp.where(sink_row, zero, k_prev), kc_ref[g, rows]], axis=0)
            kx = jnp.concatenate([k2, mask], axis=1)
            ve = jnp.concatenate([jnp.where(sink_row, one_e, ve_prev), vec_ref[g, rows]], axis=0)
            vo = jnp.concatenate([jnp.where(sink_row, one_o, vo_prev), voc_ref[g, rows]], axis=0)
            for part, vv in ((order[:per_group // 2], ve), (order[per_group // 2:], vo)):
                qs = jnp.concatenate(
                    [jnp.concatenate([q_ref[g * per_group + h, rows], ident], axis=1)
                     for h in part], axis=0)
                scores.append(_dot_nt(qs, kx))
                values.append(vv)
    pv = []
    for s, vv in zip(scores, values):
        p = jnp.exp2(s - jnp.max(s, axis=-1, keepdims=True)).astype(BF16)
        pv.append(_dot(p, vv))
    for qb in range(blocks_per_step):
        rows = slice(qb * w, (qb + 1) * w)
        for g in range(SWA_KV_HEADS):
            pv_e = pv[(qb * SWA_KV_HEADS + g) * 2]
            pv_o = pv[(qb * SWA_KV_HEADS + g) * 2 + 1]
            for c in range(per_group // 2):
                e = pv_e[c * w:(c + 1) * w]
                o = pv_o[c * w:(c + 1) * w]
                out = jnp.where(low, e * (1.0 / e[:, dh:dh + 1]), o * (1.0 / o[:, 0:1]))
                col = (g * (per_group // 2) + c) * LANES
                o_ref[rows, col:col + LANES] = out.astype(BF16)


def _swa_mask():
    w = SWA_WINDOW
    j = np.arange(2 * w)[:, None]
    i = np.arange(w)[None, :]
    band = (j > i) & (j <= i + w)
    later = band | (j == 0)
    first = (band & (j >= w)) | (j == 0)
    both = np.stack([first, later])
    return jnp.asarray(np.where(both, 0.0, MASK_BIAS), dtype=BF16)


def _swa(qh, kh, ve, vo, batch, seq, blocks_per_step):
    m = kh.shape[1]
    w = SWA_WINDOW
    tq = blocks_per_step * w
    steps = seq // tq
    kv = SWA_KV_HEADS
    cur = lambda n_: pl.BlockSpec((n_, tq, LANES), lambda b, t: (0, b * steps + t, 0))
    prev = lambda n_: pl.BlockSpec(
        (n_, w, LANES),
        lambda b, t: (0, b * (seq // w) + jnp.maximum(t * blocks_per_step - 1, 0), 0))
    return pl.pallas_call(
        functools.partial(_swa_kernel, blocks_per_step=blocks_per_step),
        grid=(batch, steps),
        in_specs=[cur(SWA_HEADS), cur(kv), prev(kv), cur(kv), prev(kv), cur(kv), prev(kv),
                  _resident((2, 2 * w, w))],
        out_specs=pl.BlockSpec((tq, SWA_Q), lambda b, t: (b * steps + t, 0)),
        out_shape=jax.ShapeDtypeStruct((m, SWA_Q), BF16),
        compiler_params=pltpu.CompilerParams(
            dimension_semantics=("arbitrary", "arbitrary"), vmem_limit_bytes=VMEM_LIMIT),
        name="swa",
    )(qh, kh, kh, ve, ve, vo, vo, _swa_mask())


def _layer_norm(y, g, b):
    mu = jnp.mean(y, axis=-1, keepdims=True)
    d = y - mu
    var = jnp.mean(d * d, axis=-1, keepdims=True)
    return d * lax.rsqrt(var + LN_EPS) * g + b


def _merge_kernel(x_ref, oa_ref, ob_ref, ga_ref, gb_ref, wa_ref, wb_ref, wo_ref,
                  g_ref, b_ref, h_ref):
    tm = x_ref.shape[0]
    parts = [slice(i * (tm // MERGE_SPLIT), (i + 1) * (tm // MERGE_SPLIT))
             for i in range(MERGE_SPLIT)]
    branches = [(_dot(oa_ref[rs, :], wa_ref[...]), _dot(ob_ref[rs, :], wb_ref[...]))
                for rs in parts]
    for rs, (y_a, y_b) in zip(parts, branches):
        merged = (jax.nn.sigmoid(ga_ref[rs, :].astype(F32)) * y_a
                  + jax.nn.sigmoid(gb_ref[rs, :].astype(F32)) * y_b)
        mix = _dot(merged.astype(BF16), wo_ref[...])
        h_ref[rs, :] = _layer_norm(DEEPNORM_ALPHA * x_ref[rs, :] + mix, g_ref[...], b_ref[...])


def _merge(x2, oa, ob, ga, gb, wa, wb, wo, ln_g, ln_b, tm):
    m = x2.shape[0]
    row = lambda c: pl.BlockSpec((tm, c), lambda i: (i, 0))
    sq = _resident((D_MODEL, D_MODEL))
    vec = _resident((1, D_MODEL))
    return pl.pallas_call(
        _merge_kernel,
        grid=(m // tm,),
        in_specs=[row(D_MODEL)] * 5 + [sq, sq, sq, vec, vec],
        out_specs=row(D_MODEL),
        out_shape=jax.ShapeDtypeStruct((m, D_MODEL), F32),
        compiler_params=pltpu.CompilerParams(
            dimension_semantics=("arbitrary",), vmem_limit_bytes=VMEM_LIMIT),
        name="merge",
    )(x2, oa, ob, ga, gb, wa, wb, wo, ln_g, ln_b)


def _ffn_kernel(h_ref, wg_ref, wu_ref, wd_ref, g_ref, b_ref, o_ref, *, ff_chunk):
    d_ff = wg_ref.shape[1]
    n_chunks = d_ff // ff_chunk
    sl = lambda j: slice(j * ff_chunk, (j + 1) * ff_chunk)
    tm = h_ref.shape[0]
    for i in range(FFN_SPLIT):
        rs = slice(i * (tm // FFN_SPLIT), (i + 1) * (tm // FFN_SPLIT))
        h = h_ref[rs, :]
        hb = h.astype(BF16)

        def gate_up(j):
            return _dot(hb, wg_ref[:, sl(j)]), _dot(hb, wu_ref[:, sl(j)])

        acc = None
        gu = gate_up(0)
        for j in range(n_chunks):
            gate, up = gu
            if j + 1 < n_chunks:
                gu = gate_up(j + 1)
            act = (gate * jax.nn.sigmoid(gate) * up).astype(BF16)
            part = _dot(act, wd_ref[sl(j), :])
            acc = part if acc is None else acc + part
        o_ref[rs, :] = _layer_norm(DEEPNORM_ALPHA * h + acc, g_ref[...], b_ref[...])


def _ffn(h1, wg, wu, wd, ln_g, ln_b, tm, ff_chunk):
    m = h1.shape[0]
    d_ff = wg.shape[1]
    row = pl.BlockSpec((tm, D_MODEL), lambda i: (i, 0))
    vec = _resident((1, D_MODEL))
    return pl.pallas_call(
        functools.partial(_ffn_kernel, ff_chunk=ff_chunk),
        grid=(m // tm,),
        in_specs=[row, _resident((D_MODEL, d_ff)), _resident((D_MODEL, d_ff)),
                  _resident((d_ff, D_MODEL)), vec, vec],
        out_specs=row,
        out_shape=jax.ShapeDtypeStruct((m, D_MODEL), F32),
        compiler_params=pltpu.CompilerParams(
            dimension_semantics=("arbitrary",), vmem_limit_bytes=VMEM_LIMIT),
        name="ffn",
    )(h1, wg, wu, wd, ln_g, ln_b)


def _rope_inv_freq():
    inv_freq = ROPE_THETA ** (-jnp.arange(0, ROPE_DIM, 2, dtype=F32) / ROPE_DIM)
    return jnp.broadcast_to(inv_freq[:, None], (ROPE_DIM // 2, LANES))


def _layer(h2, pos_row, invf, batch, seq, w_in, w_decay_up, b_decay, gla_norm_g,
           w_branch_a, w_branch_b, sinks, w_out, ln1_g, ln1_b,
           w_ffn_gate, w_ffn_up, w_ffn_down, ln2_g, ln2_b):
    o_dl = 2 * GLA_QK + 2 * GLA_V
    o_qb = o_dl + GLA_RANK
    o_kb = o_qb + SWA_Q
    o_ga = o_kb + 2 * SWA_KV
    w_in_r = jnp.concatenate(
        [w_in[:, :o_dl], w_in[:, o_qb:o_kb], w_in[:, o_kb:o_ga], w_in[:, o_dl:o_qb],
         jnp.zeros((D_MODEL, RANK_PAD - GLA_RANK), w_in.dtype), w_in[:, o_ga:]],
        axis=1).astype(BF16)
    w_up_p = jnp.concatenate(
        [w_decay_up, jnp.zeros((RANK_PAD - GLA_RANK, GLA_QK), w_decay_up.dtype)],
        axis=0).astype(BF16)

    neg_sink = -sinks.astype(F32) * LOG2_E
    sink_hi = neg_sink.astype(BF16).astype(F32)
    sink_feat = jnp.zeros((SWA_HEADS, LANES), F32)
    sink_feat = sink_feat.at[:, SWA_HEAD_DIM].set(sink_hi)
    sink_feat = sink_feat.at[:, SWA_HEAD_DIM + 1].set(neg_sink - sink_hi)

    qa, ka, va, ra, la, qh, kh, ve, vo, ga, gb = _inproj(
        h2, pos_row, invf, sink_feat, w_in_r, w_up_p, b_decay[None, :], tm=512)
    oa = _gla(qa, ka, va, ra, la, gla_norm_g[None, :], batch, seq, tt=512)
    ob = _swa(qh, kh, ve, vo, batch, seq, blocks_per_step=4)
    h1 = _merge(h2, oa, ob, ga, gb, w_branch_a.astype(BF16), w_branch_b.astype(BF16),
                w_out.astype(BF16), ln1_g[None, :], ln1_b[None, :], tm=512)
    return _ffn(h1, w_ffn_gate.astype(BF16), w_ffn_up.astype(BF16), w_ffn_down.astype(BF16),
                ln2_g[None, :], ln2_b[None, :], tm=1024, ff_chunk=256)


def kernel(x, positions, w_in, w_decay_up, b_decay, gla_norm_g, w_branch_a, w_branch_b, sinks,
           w_out, ln1_g, ln1_b, w_ffn_gate, w_ffn_up, w_ffn_down, ln2_g, ln2_b):
    batch, seq, d = x.shape
    m = batch * seq
    h2 = x.reshape(m, d)
    pos_row = positions.reshape(1, m).astype(F32)
    invf = _rope_inv_freq()
    for layer in range(w_in.shape[0]):
        h2 = _layer(h2, pos_row, invf, batch, seq, w_in[layer], w_decay_up[layer], b_decay[layer],
                    gla_norm_g[layer], w_branch_a[layer], w_branch_b[layer], sinks[layer],
                    w_out[layer], ln1_g[layer], ln1_b[layer], w_ffn_gate[layer],
                    w_ffn_up[layer], w_ffn_down[layer], ln2_g[layer], ln2_b[layer])
    return h2.reshape(batch, seq, d)
```

```python
import functools

import math

import jax
import jax.numpy as jnp
import numpy as np
from jax import lax
from jax.experimental import pallas as pl
from jax.experimental.pallas import tpu as pltpu

D_MODEL = 1024
GLA_HEADS = 4
GLA_DK = 128
GLA_DV = 256
GLA_RANK = 16
GLA_TAU = 16.0
GLA_CHUNK = 64
SWA_HEADS = 16
SWA_KV_HEADS = 2
SWA_HEAD_DIM = 64
SWA_WINDOW = 128
ROPE_DIM = 16
ROPE_THETA = 500000.0
DEPTH = 1
DEEPNORM_ALPHA = (2.0 * DEPTH) ** 0.25
LN_EPS = 1e-5
RMS_EPS = 1e-6

GLA_QK = GLA_HEADS * GLA_DK
GLA_V = GLA_HEADS * GLA_DV
SWA_Q = SWA_HEADS * SWA_HEAD_DIM
SWA_KV = SWA_KV_HEADS * SWA_HEAD_DIM

LANES = 128
MXU_COLS = 256
RANK_PAD = LANES
VMEM_LIMIT = 56 * 1024 * 1024
MERGE_SPLIT = 4
FFN_SPLIT = 4
LOG2_E = math.log2(math.e)
MASK_BIAS = -1e30

_C_QA = 0
_C_KA = _C_QA + GLA_QK
_C_VA = _C_KA + GLA_QK
_C_RA = _C_VA + GLA_V
_C_QB = _C_RA + GLA_V
_C_KV = _C_QB + SWA_Q
_C_GA = _C_KV + 2 * SWA_KV
_C_GB = _C_GA + D_MODEL
_C_DL = _C_GB + D_MODEL
_C_END = _C_DL + RANK_PAD
_D_IN = 2 * GLA_QK + 2 * GLA_V + GLA_RANK + SWA_Q + 2 * SWA_KV + 2 * D_MODEL
_O_DL = 2 * GLA_QK + 2 * GLA_V

F32 = jnp.float32
BF16 = jnp.bfloat16


def _dot(a, b):
    return jnp.dot(a, b, preferred_element_type=F32)


def _dot_nt(a, b):
    return lax.dot_general(a, b, (((1,), (1,)), ((), ())), preferred_element_type=F32)


def _dot_tn(a, b):
    return lax.dot_general(a, b, (((0,), (0,)), ((), ())), preferred_element_type=F32)


def _resident(shape):
    return pl.BlockSpec(shape, lambda *_: (0,) * len(shape), pipeline_mode=pl.Buffered(1))


def _win_prep_kernel(w_ref, o_ref, tail_ref):
    shift = LANES - GLA_RANK
    n_slabs = (_D_IN - _O_DL) // LANES
    o_ref[:, :_O_DL] = w_ref[:, :_O_DL].astype(BF16)
    tail_ref[...] = jnp.zeros_like(tail_ref)
    tail_ref[:, :GLA_RANK] = w_ref[:, _O_DL + n_slabs * LANES:]
    lane = lax.broadcasted_iota(jnp.int32, tail_ref.shape, 1)

    def slab(i):
        if i == n_slabs:
            return tail_ref[...]
        return w_ref[:, _O_DL + i * LANES:_O_DL + (i + 1) * LANES]

    first = slab(0)
    rolled = pltpu.roll(first, shift, 1)
    for i in range(n_slabs):
        rolled_next = pltpu.roll(slab(i + 1), shift, 1)
        o_ref[:, _O_DL + i * LANES:_O_DL + (i + 1) * LANES] = jnp.where(
            lane < shift, rolled, rolled_next).astype(BF16)
        rolled = rolled_next
    o_ref[:, _C_DL:] = jnp.where(lane < GLA_RANK, first, 0.0).astype(BF16)


def _win_prep(w_in):
    rb = LANES
    assert _C_DL == _O_DL + ((_D_IN - _O_DL) // LANES) * LANES
    return pl.pallas_call(
        _win_prep_kernel,
        grid=(D_MODEL // rb,),
        in_specs=[pl.BlockSpec((rb, _D_IN), lambda i: (i, 0))],
        out_specs=pl.BlockSpec((rb, _C_END), lambda i: (i, 0)),
        out_shape=jax.ShapeDtypeStruct((D_MODEL, _C_END), BF16),
        scratch_shapes=[pltpu.VMEM((rb, LANES), F32)],
        compiler_params=pltpu.CompilerParams(dimension_semantics=("arbitrary",)),
        name="win_prep",
    )(w_in)


def _inproj_kernel(x_ref, pos_ref, invf_ref, spread_ref, sf_ref, w_ref, wup_ref, bdec_ref,
                   qa_ref, ka_ref, va_ref, ra_ref, la_ref,
                   qh_ref, kh_ref, ve_ref, vo_ref, ga_ref, gb_ref):
    xb = x_ref[...].astype(BF16)

    def proj(lo, hi):
        return _dot(xb, w_ref[:, lo:hi])

    tm = x_ref.shape[0]
    half = ROPE_DIM // 2
    ang = jnp.tile(invf_ref[...], (1, tm // LANES)) * pos_ref[...]
    one_row = jnp.where(lax.broadcasted_iota(jnp.int32, (2 * half, tm), 0) == 0, 1.0, 0.0)
    cs = jnp.concatenate([jnp.cos(ang), jnp.sin(ang), one_row], axis=0)
    cs_hi = cs.astype(BF16)
    cs_lo = (cs - cs_hi.astype(F32)).astype(BF16)
    tabs = _dot_tn(jnp.concatenate([cs_hi, cs_lo], axis=0), spread_ref[...])
    cos, a_tab, b_tab = tabs[:, :LANES], tabs[:, LANES:2 * LANES], tabs[:, 2 * LANES:]

    def rope(t):
        return (t * cos + pltpu.roll(t, LANES - half, 1) * a_tab
                + pltpu.roll(t, half, 1) * b_tab)

    dh = SWA_HEAD_DIM
    lane_full = lax.broadcasted_iota(jnp.int32, (tm, LANES), 1)
    low = lane_full < dh
    q_scale = (dh ** -0.5) * LOG2_E

    def swa_queries(i):
        qb = proj(_C_QB + i * MXU_COLS, _C_QB + (i + 1) * MXU_COLS)
        for c in range(2 * i, 2 * i + 2):
            t = rope(qb[:, (c - 2 * i) * LANES:(c - 2 * i + 1) * LANES]) * q_scale
            qh_ref[2 * c] = jnp.where(low, t, sf_ref[2 * c:2 * c + 1, :]).astype(BF16)
            qh_ref[2 * c + 1] = jnp.where(
                low, pltpu.roll(t, dh, 1), sf_ref[2 * c + 1:2 * c + 2, :]).astype(BF16)

    def swa_keys_values_and_decay():
        kv = proj(_C_KV, _C_GA)
        kk = rope(kv[:, :SWA_KV])
        vv = kv[:, SWA_KV:]
        kk_sw = pltpu.roll(kk, dh, 1)
        vv_sw = pltpu.roll(vv, dh, 1)
        one_k = jnp.where(lane_full < dh + 2, 1.0, 0.0)
        one_e = jnp.where(lane_full == dh, 1.0, 0.0)
        one_o = jnp.where(lane_full == 0, 1.0, 0.0)
        kh_ref[0] = jnp.where(low, kk, one_k).astype(BF16)
        kh_ref[1] = jnp.where(low, kk_sw, one_k).astype(BF16)
        ve_ref[0] = jnp.where(low, vv, one_e).astype(BF16)
        ve_ref[1] = jnp.where(low, vv_sw, one_e).astype(BF16)
        vo_ref[0] = jnp.where(low, one_o, vv_sw).astype(BF16)
        vo_ref[1] = jnp.where(low, one_o, vv).astype(BF16)
        return proj(_C_DL, _C_END).astype(BF16)

    def log_decay(d_low, h):
        cols = slice(h * GLA_DK, (h + 1) * GLA_DK)
        z = _dot(d_low, wup_ref[:, cols]) + bdec_ref[:, cols]
        log_sig = jnp.minimum(z, 0.0) - jnp.log1p(jnp.exp(-jnp.abs(z)))
        la_ref[:, cols] = log_sig * (LOG2_E / GLA_TAU)

    def plain(ref, col0, i, scale=None):
        y = proj(col0 + i * MXU_COLS, col0 + (i + 1) * MXU_COLS)
        if scale is not None:
            y = y * scale
        ref[:, i * MXU_COLS:(i + 1) * MXU_COLS] = y.astype(BF16)

    plain_tiles = (
        [(qa_ref, _C_QA, i, GLA_DK ** -0.5) for i in range(GLA_QK // MXU_COLS)]
        + [(ka_ref, _C_KA, i, None) for i in range(GLA_QK // MXU_COLS)]
        + [(va_ref, _C_VA, i, None) for i in range(GLA_V // MXU_COLS)]
        + [(ra_ref, _C_RA, i, None) for i in range(GLA_V // MXU_COLS)]
        + [(ga_ref, _C_GA, i, None) for i in range(D_MODEL // MXU_COLS)]
        + [(gb_ref, _C_GB, i, None) for i in range(D_MODEL // MXU_COLS)])
    d_low = swa_keys_values_and_decay()
    heavy = []
    for i in range(max(SWA_Q // MXU_COLS, GLA_HEADS)):
        if i < SWA_Q // MXU_COLS:
            heavy.append(functools.partial(swa_queries, i))
        if i < GLA_HEADS:
            heavy.append(functools.partial(log_decay, d_low, i))
    per_heavy = len(plain_tiles) // len(heavy)
    for n, task in enumerate(heavy):
        task()
        first = n * per_heavy
        last = len(plain_tiles) if n == len(heavy) - 1 else first + per_heavy
        for args in plain_tiles[first:last]:
            plain(*args)


def _rope_spread():
    half = ROPE_DIM // 2
    rows = 4 * half
    e = np.zeros((rows, 3 * LANES), np.float32)
    for lane in range(LANES):
        j = lane % SWA_HEAD_DIM
        if j < half:
            e[j, lane] = 1.0
            e[half + j, LANES + lane] = -1.0
        elif j < 2 * half:
            e[j - half, lane] = 1.0
            e[j, 2 * LANES + lane] = 1.0
        else:
            e[2 * half, lane] = 1.0
    return jnp.asarray(np.concatenate([e, e], axis=0), dtype=BF16)


def _inproj(x2, pos_row, invf, sink_feat, w_in_r, w_up_p, b_dec, tm):
    m = x2.shape[0]
    half = ROPE_DIM // 2
    row = lambda c: pl.BlockSpec((tm, c), lambda i: (i, 0))
    row_out = lambda c, d: (row(c), jax.ShapeDtypeStruct((m, c), d))
    slab_out = lambda n: (pl.BlockSpec((n, tm, LANES), lambda i: (0, i, 0)),
                          jax.ShapeDtypeStruct((n, m, LANES), BF16))
    outs = [row_out(GLA_QK, BF16), row_out(GLA_QK, BF16), row_out(GLA_V, BF16),
            row_out(GLA_V, BF16), row_out(GLA_QK, F32),
            slab_out(SWA_HEADS), slab_out(SWA_KV_HEADS), slab_out(SWA_KV_HEADS),
            slab_out(SWA_KV_HEADS), row_out(D_MODEL, BF16), row_out(D_MODEL, BF16)]
    return pl.pallas_call(
        _inproj_kernel,
        grid=(m // tm,),
        in_specs=[row(D_MODEL), pl.BlockSpec((1, tm), lambda i: (0, i)),
                  _resident((half, LANES)), _resident((8 * half, 3 * LANES)),
                  _resident((SWA_HEADS, LANES)),
                  _resident((D_MODEL, _C_END)), _resident((RANK_PAD, GLA_QK)),
                  _resident((1, GLA_QK))],
        out_specs=[o[0] for o in outs],
        out_shape=[o[1] for o in outs],
        compiler_params=pltpu.CompilerParams(
            dimension_semantics=("arbitrary",), vmem_limit_bytes=VMEM_LIMIT),
        name="inproj",
    )(x2, pos_row, invf, _rope_spread(), sink_feat, w_in_r, w_up_p, b_dec)


def _gla_kernel(q_ref, k_ref, v_ref, r_ref, la_ref, g_ref, o_ref, st_ref, *, n_chunks):
    @pl.when(pl.program_id(1) == 0)
    def _():
        st_ref[...] = jnp.zeros_like(st_ref)

    c_ = GLA_CHUNK
    causal = (lax.broadcasted_iota(jnp.int32, (c_, c_), 0)
              >= lax.broadcasted_iota(jnp.int32, (c_, c_), 1))
    tri = jnp.where(causal, 1.0, 0.0).astype(BF16)
    gain = g_ref[...]

    items =[(c, h) for c in range(n_chunks) for h in range(GLA_HEADS)]
    rows = lambda c: slice(c * c_, (c + 1) * c_)
    ks = lambda h: slice(h * GLA_DK, (h + 1) * GLA_DK)
    vs = lambda h: slice(h * GLA_DV, (h + 1) * GLA_DV)

    b = []
    for c in range(n_chunks):
        la = la_ref[rows(c), :]
        hi = la.astype(BF16)
        rem = la - hi.astype(F32)
        mid = rem.astype(BF16)
        lo = (rem - mid.astype(F32)).astype(BF16)
        b.append(_dot(tri, hi) + _dot(tri, mid) + _dot(tri, lo))

    q_in, scores, contrib, decay = {}, {}, {}, {}
    for c, h in items:
        bh = b[c][:, ks(h)]
        b_last = bh[c_ - 1:c_, :]
        q = q_ref[rows(c), ks(h)].astype(F32)
        k = k_ref[rows(c), ks(h)].astype(F32)
        q_in[c, h] = (q * jnp.exp2(bh)).astype(BF16)
        k_in = (k * jnp.exp2(-bh)).astype(BF16)
        k_out = (k * jnp.exp2(b_last - bh)).astype(BF16)
        scores[c, h] = _dot_nt(q_in[c, h], k_in)
        contrib[c, h] = _dot_tn(k_out, v_ref[rows(c), vs(h)])
        dec = jnp.exp2(jnp.broadcast_to(b_last, (GLA_DK, GLA_DK)).T)
        decay[c, h] = jnp.concatenate([dec] * (GLA_DV // GLA_DK), axis=1)

    o_intra = {}
    for c, h in items:
        masked = jnp.where(causal, scores[c, h], 0.0).astype(BF16)
        o_intra[c, h] = _dot(masked, v_ref[rows(c), vs(h)])

    st = [st_ref[h] for h in range(GLA_HEADS)]
    for c, h in items:
        o = o_intra[c, h] + _dot(q_in[c, h], st[h].astype(BF16))
        st[h] = st[h] * decay[c, h] + contrib[c, h]
        o = o * lax.rsqrt(jnp.mean(o * o, axis=-1, keepdims=True) + RMS_EPS)
        r = r_ref[rows(c), vs(h)].astype(F32)
        o = o * gain * (r * jax.nn.sigmoid(r))
        o_ref[rows(c), vs(h)] = o.astype(BF16)
    for h in range(GLA_HEADS):
        st_ref[h] = st[h]


def _gla(qa, ka, va, ra, la, gain, batch, seq, tt):
    m = qa.shape[0]
    nt = seq // tt
    row = lambda c: pl.BlockSpec((tt, c), lambda b, t: (b * nt + t, 0))
    return pl.pallas_call(
        functools.partial(_gla_kernel, n_chunks=tt // GLA_CHUNK),
        grid=(batch, nt),
        in_specs=[row(GLA_QK), row(GLA_QK), row(GLA_V), row(GLA_V), row(GLA_QK),
                  pl.BlockSpec((1, GLA_DV), lambda b, t: (0, 0))],
        out_specs=row(GLA_V),
        out_shape=jax.ShapeDtypeStruct((m, GLA_V), BF16),
        scratch_shapes=[pltpu.VMEM((GLA_HEADS, GLA_DK, GLA_DV), F32)],
        compiler_params=pltpu.CompilerParams(
            dimension_semantics=("arbitrary", "arbitrary"), vmem_limit_bytes=VMEM_LIMIT),
        name="gla",
    )(qa, ka, va, ra, la, gain)


def _swa_kernel(q_ref, kc_ref, kp_ref, vec_ref, vep_ref, voc_ref, vop_ref, mask_ref, o_ref,
                *, blocks_per_step):
    w = SWA_WINDOW
    dh = SWA_HEAD_DIM
    per_group = SWA_HEADS // SWA_KV_HEADS
    first = pl.program_id(1) == 0
    row = lax.broadcasted_iota(jnp.int32, (w, LANES), 0)
    lane = lax.broadcasted_iota(jnp.int32, (w, LANES), 1)
    ident = jnp.where(row == lane, 1.0, 0.0).astype(BF16)
    sink_row = row == 0
    low = lane < dh
    zero = jnp.zeros((w, LANES), BF16)
    one_e = jnp.where(lane == dh, 1.0, 0.0).astype(BF16)
    one_o = jnp.where(lane == 0, 1.0, 0.0).astype(BF16)
    order = list(range(0, per_group, 2)) + list(range(1, per_group, 2))

    scores = []
    values = []
    for qb in range(blocks_per_step):
        rows = slice(qb * w, (qb + 1) * w)
        prev_rows = slice((qb - 1) * w, qb * w)
        mask = mask_ref[1] if qb else jnp.where(first, mask_ref[0], mask_ref[1])
        for g in range(SWA_KV_HEADS):
            k_prev = kc_ref[g, prev_rows] if qb else kp_ref[g]
            ve_prev = vec_ref[g, prev_rows] if qb else vep_ref[g]
            vo_prev = voc_ref[g, prev_rows] if qb else vop_ref[g]
            k2 = jnp.concatenate([jnp.where(sink_row, zero, k_prev), kc_ref[g, rows]], axis=0)
            kx = jnp.concatenate([k2, mask], axis=1)
            ve = jnp.concatenate([jnp.where(sink_row, one_e, ve_prev), vec_ref[g, rows]], axis=0)
            vo = jnp.concatenate([jnp.where(sink_row, one_o, vo_prev), voc_ref[g, rows]], axis=0)
            for part, vv in ((order[:per_group // 2], ve), (order[per_group // 2:], vo)):
                qs = jnp.concatenate(
                    [jnp.concatenate([q_ref[g * per_group + h, rows], ident], axis=1)
                     for h in part], axis=0)
                scores.append(_dot_nt(qs, kx))
                values.append(vv)
    pv = []
    for s, vv in zip(scores, values):
        p = jnp.exp2(s - jnp.max(s, axis=-1, keepdims=True)).astype(BF16)
        pv.append(_dot(p, vv))
    for qb in range(blocks_per_step):
        rows = slice(qb * w, (qb + 1) * w)
        for g in range(SWA_KV_HEADS):
            pv_e = pv[(qb * SWA_KV_HEADS + g) * 2]
            pv_o = pv[(qb * SWA_KV_HEADS + g) * 2 + 1]
            for c in range(per_group // 2):
                e = pv_e[c * w:(c + 1) * w]
                o = pv_o[c * w:(c + 1) * w]
                out = jnp.where(low, e * (1.0 / e[:, dh:dh + 1]), o * (1.0 / o[:, 0:1]))
                col = (g * (per_group // 2) + c) * LANES
                o_ref[rows, col:col + LANES] = out.astype(BF16)


def _swa_mask():
    w = SWA_WINDOW
    j = np.arange(2 * w)[:, None]
    i = np.arange(w)[None, :]
    band = (j > i) & (j <= i + w)
    later = band | (j == 0)
    first = (band & (j >= w)) | (j == 0)
    both = np.stack([first, later])
    return jnp.asarray(np.where(both, 0.0, MASK_BIAS), dtype=BF16)


def _swa(qh, kh, ve, vo, batch, seq, blocks_per_step):
    m = kh.shape[1]
    w = SWA_WINDOW
    tq = blocks_per_step * w
    steps = seq // tq
    kv = SWA_KV_HEADS
    cur = lambda n_: pl.BlockSpec((n_, tq, LANES), lambda b, t: (0, b * steps + t, 0))
    prev = lambda n_: pl.BlockSpec(
        (n_, w, LANES),
        lambda b, t: (0, b * (seq // w) + jnp.maximum(t * blocks_per_step - 1, 0), 0))
    return pl.pallas_call(
        functools.partial(_swa_kernel, blocks_per_step=blocks_per_step),
        grid=(batch, steps),
        in_specs=[cur(SWA_HEADS), cur(kv), prev(kv), cur(kv), prev(kv), cur(kv), prev(kv),
                  _resident((2, 2 * w, w))],
        out_specs=pl.BlockSpec((tq, SWA_Q), lambda b, t: (b * steps + t, 0)),
        out_shape=jax.ShapeDtypeStruct((m, SWA_Q), BF16),
        compiler_params=pltpu.CompilerParams(
            dimension_semantics=("arbitrary", "arbitrary"), vmem_limit_bytes=VMEM_LIMIT),
        name="swa",
    )(qh, kh, kh, ve, ve, vo, vo, _swa_mask())


def _layer_norm(y, g, b):
    mu = jnp.mean(y, axis=-1, keepdims=True)
    d = y - mu
    var = jnp.mean(d * d, axis=-1, keepdims=True)
    return d * lax.rsqrt(var + LN_EPS) * g + b


def _merge_kernel(x_ref, oa_ref, ob_ref, ga_ref, gb_ref, wa_ref, wb_ref, wo_ref,
                  g_ref, b_ref, h_ref):
    tm = x_ref.shape[0]
    parts = [slice(i * (tm // MERGE_SPLIT), (i + 1) * (tm // MERGE_SPLIT))
             for i in range(MERGE_SPLIT)]
    branches = [(_dot(oa_ref[rs, :], wa_ref[...]), _dot(ob_ref[rs, :], wb_ref[...]))
                for rs in parts]
    for rs, (y_a, y_b) in zip(parts, branches):
        merged = (jax.nn.sigmoid(ga_ref[rs, :].astype(F32)) * y_a
                  + jax.nn.sigmoid(gb_ref[rs, :].astype(F32)) * y_b)
        mix = _dot(merged.astype(BF16), wo_ref[...])
        h_ref[rs, :] = _layer_norm(DEEPNORM_ALPHA * x_ref[rs, :] + mix, g_ref[...], b_ref[...])


def _merge(x2, oa, ob, ga, gb, wa, wb, wo, ln_g, ln_b, tm):
    m = x2.shape[0]
    row = lambda c: pl.BlockSpec((tm, c), lambda i: (i, 0))
    sq = _resident((D_MODEL, D_MODEL))
    vec = _resident((1, D_MODEL))
    return pl.pallas_call(
        _merge_kernel,
        grid=(m // tm,),
        in_specs=[row(D_MODEL)] * 5 + [sq, sq, sq, vec, vec],
        out_specs=row(D_MODEL),
        out_shape=jax.ShapeDtypeStruct((m, D_MODEL), F32),
        compiler_params=pltpu.CompilerParams(
            dimension_semantics=("arbitrary",), vmem_limit_bytes=VMEM_LIMIT),
        name="merge",
    )(x2, oa, ob, ga, gb, wa, wb, wo, ln_g, ln_b)


def _ffn_kernel(h_ref, wg_ref, wu_ref, wd_ref, g_ref, b_ref, o_ref, *, ff_chunk):
    d_ff = wg_ref.shape[1]
    n_chunks = d_ff // ff_chunk
    sl = lambda j: slice(j * ff_chunk, (j + 1) * ff_chunk)
    tm = h_ref.shape[0]
    for i in range(FFN_SPLIT):
        rs = slice(i * (tm // FFN_SPLIT), (i + 1) * (tm // FFN_SPLIT))
        h = h_ref[rs, :]
        hb = h.astype(BF16)

        def gate_up(j):
            return _dot(hb, wg_ref[:, sl(j)]), _dot(hb, wu_ref[:, sl(j)])

        acc = None
        gu = gate_up(0)
        for j in range(n_chunks):
            gate, up = gu
            if j + 1 < n_chunks:
                gu = gate_up(j + 1)
            act = (gate * jax.nn.sigmoid(gate) * up).astype(BF16)
            part = _dot(act, wd_ref[sl(j), :])
            acc = part if acc is None else acc + part
        o_ref[rs, :] = _layer_norm(DEEPNORM_ALPHA * h + acc, g_ref[...], b_ref[...])


def _ffn(h1, wg, wu, wd, ln_g, ln_b, tm, ff_chunk):
    m = h1.shape[0]
    d_ff = wg.shape[1]
    row = pl.BlockSpec((tm, D_MODEL), lambda i: (i, 0))
    vec = _resident((1, D_MODEL))
    return pl.pallas_call(
        functools.partial(_ffn_kernel, ff_chunk=ff_chunk),
        grid=(m // tm,),
        in_specs=[row, _resident((D_MODEL, d_ff)), _resident((D_MODEL, d_ff)),
                  _resident((d_ff, D_MODEL)), vec, vec],
        out_specs=row,
        out_shape=jax.ShapeDtypeStruct((m, D_MODEL), F32),
        compiler_params=pltpu.CompilerParams(
            dimension_semantics=("arbitrary",), vmem_limit_bytes=VMEM_LIMIT),
        name="ffn",
    )(h1, wg, wu, wd, ln_g, ln_b)


def _rope_inv_freq():
    inv_freq = ROPE_THETA ** (-jnp.arange(0, ROPE_DIM, 2, dtype=F32) / ROPE_DIM)
    return jnp.broadcast_to(inv_freq[:, None], (ROPE_DIM // 2, LANES))


def _layer(h2, pos_row, invf, batch, seq, w_in, w_decay_up, b_decay, gla_norm_g,
           w_branch_a, w_branch_b, sinks, w_out, ln1_g, ln1_b,
           w_ffn_gate, w_ffn_up, w_ffn_down, ln2_g, ln2_b):
    w_in_r = _win_prep(w_in)
    w_up_p = jnp.concatenate(
        [w_decay_up, jnp.zeros((RANK_PAD - GLA_RANK, GLA_QK), w_decay_up.dtype)],
        axis=0).astype(BF16)

    neg_sink = -sinks.astype(F32) * LOG2_E
    sink_hi = neg_sink.astype(BF16).astype(F32)
    sink_feat = jnp.zeros((SWA_HEADS, LANES), F32)
    sink_feat = sink_feat.at[:, SWA_HEAD_DIM].set(sink_hi)
    sink_feat = sink_feat.at[:, SWA_HEAD_DIM + 1].set(neg_sink - sink_hi)

    qa, ka, va, ra, la, qh, kh, ve, vo, ga, gb = _inproj(
        h2, pos_row, invf, sink_feat, w_in_r, w_up_p, b_decay[None, :], tm=512)
    oa = _gla(qa, ka, va, ra, la, gla_norm_g[None, :], batch, seq, tt=512)
    ob = _swa(qh, kh, ve, vo, batch, seq, blocks_per_step=4)
    h1 = _merge(h2, oa, ob, ga, gb, w_branch_a.astype(BF16), w_branch_b.astype(BF16),
                w_out.astype(BF16), ln1_g[None, :], ln1_b[None, :], tm=1024)
    return _ffn(h1, w_ffn_gate.astype(BF16), w_ffn_up.astype(BF16), w_ffn_down.astype(BF16),
                ln2_g[None, :], ln2_b[None, :], tm=1024, ff_chunk=256)


def kernel(x, positions, w_in, w_decay_up, b_decay, gla_norm_g, w_branch_a, w_branch_b, sinks,
           w_out, ln1_g, ln1_b, w_ffn_gate, w_ffn_up, w_ffn_down, ln2_g, ln2_b):
    batch, seq, d = x.shape
    m = batch * seq
    h2 = x.reshape(m, d)
    pos_row = positions.reshape(1, m).astype(F32)
    invf = _rope_inv_freq()
    for layer in range(w_in.shape[0]):
        h2 = _layer(h2, pos_row, invf, batch, seq, w_in[layer], w_decay_up[layer], b_decay[layer],
                    gla_norm_g[layer], w_branch_a[layer], w_branch_b[layer], sinks[layer],
                    w_out[layer], ln1_g[layer], ln1_b[layer], w_ffn_gate[layer],
                    w_ffn_up[layer], w_ffn_down[layer], ln2_g[layer], ln2_b[layer])
    return h2.reshape(batch, seq, d)
```

```python
import functools

import math

import jax
import jax.numpy as jnp
import numpy as np
from jax import lax
from jax.experimental import pallas as pl
from jax.experimental.pallas import tpu as pltpu

D_MODEL = 1024
GLA_HEADS = 4
GLA_DK = 128
GLA_DV = 256
GLA_RANK = 16
GLA_TAU = 16.0
GLA_CHUNK = 64
SWA_HEADS = 16
SWA_KV_HEADS = 2
SWA_HEAD_DIM = 64
SWA_WINDOW = 128
ROPE_DIM = 16
ROPE_THETA = 500000.0
DEPTH = 1
DEEPNORM_ALPHA = (2.0 * DEPTH) ** 0.25
LN_EPS = 1e-5
RMS_EPS = 1e-6

GLA_QK = GLA_HEADS * GLA_DK
GLA_V = GLA_HEADS * GLA_DV
SWA_Q = SWA_HEADS * SWA_HEAD_DIM
SWA_KV = SWA_KV_HEADS * SWA_HEAD_DIM

LANES = 128
MXU_COLS = 256
RANK_PAD = LANES
VMEM_LIMIT = 56 * 1024 * 1024
MERGE_SPLIT = 4
FFN_SPLIT = 4
LOG2_E = math.log2(math.e)
MASK_BIAS = -1e30

_C_QA = 0
_C_KA = _C_QA + GLA_QK
_C_VA = _C_KA + GLA_QK
_C_RA = _C_VA + GLA_V
_C_DL = _C_RA + GLA_V
_C_QB = _C_DL + GLA_RANK
_C_KV = _C_QB + SWA_Q
_C_GA = _C_KV + 2 * SWA_KV
_C_GB = _C_GA + D_MODEL
_C_END = _C_GB + D_MODEL

F32 = jnp.float32
BF16 = jnp.bfloat16


def _dot(a, b):
    return jnp.dot(a, b, preferred_element_type=F32)


def _dot_nt(a, b):
    return lax.dot_general(a, b, (((1,), (1,)), ((), ())), preferred_element_type=F32)


def _dot_tn(a, b):
    return lax.dot_general(a, b, (((0,), (0,)), ((), ())), preferred_element_type=F32)


def _resident(shape):
    return pl.BlockSpec(shape, lambda *_: (0,) * len(shape), pipeline_mode=pl.Buffered(1))


def _inproj_kernel(x_ref, pos_ref, invf_ref, spread_ref, sf_ref, w_ref, wup_ref, bdec_ref,
                   qa_ref, ka_ref, va_ref, ra_ref, la_ref,
                   qh_ref, kh_ref, ve_ref, vo_ref, ga_ref, gb_ref):
    xb = x_ref[...].astype(BF16)

    def proj(lo, hi):
        return _dot_nt(xb, w_ref[lo:hi, :])

    tm = x_ref.shape[0]
    half = ROPE_DIM // 2
    ang = jnp.tile(invf_ref[...], (1, tm // LANES)) * pos_ref[...]
    one_row = jnp.where(lax.broadcasted_iota(jnp.int32, (2 * half, tm), 0) == 0, 1.0, 0.0)
    cs = jnp.concatenate([jnp.cos(ang), jnp.sin(ang), one_row], axis=0)
    cs_hi = cs.astype(BF16)
    cs_lo = (cs - cs_hi.astype(F32)).astype(BF16)
    tabs = _dot_tn(jnp.concatenate([cs_hi, cs_lo], axis=0), spread_ref[...])
    cos, a_tab, b_tab = tabs[:, :LANES], tabs[:, LANES:2 * LANES], tabs[:, 2 * LANES:]

    def rope(t):
        return (t * cos + pltpu.roll(t, LANES - half, 1) * a_tab
                + pltpu.roll(t, half, 1) * b_tab)

    dh = SWA_HEAD_DIM
    lane_full = lax.broadcasted_iota(jnp.int32, (tm, LANES), 1)
    low = lane_full < dh
    q_scale = (dh ** -0.5) * LOG2_E

    def swa_queries(i):
        qb = proj(_C_QB + i * MXU_COLS, _C_QB + (i + 1) * MXU_COLS)
        for c in range(2 * i, 2 * i + 2):
            t = rope(qb[:, (c - 2 * i) * LANES:(c - 2 * i + 1) * LANES]) * q_scale
            qh_ref[2 * c] = jnp.where(low, t, sf_ref[2 * c:2 * c + 1, :]).astype(BF16)
            qh_ref[2 * c + 1] = jnp.where(
                low, pltpu.roll(t, dh, 1), sf_ref[2 * c + 1:2 * c + 2, :]).astype(BF16)

    def swa_keys_values_and_decay():
        kv = proj(_C_KV, _C_GA)
        kk = rope(kv[:, :SWA_KV])
        vv = kv[:, SWA_KV:]
        kk_sw = pltpu.roll(kk, dh, 1)
        vv_sw = pltpu.roll(vv, dh, 1)
        one_k = jnp.where(lane_full < dh + 2, 1.0, 0.0)
        one_e = jnp.where(lane_full == dh, 1.0, 0.0)
        one_o = jnp.where(lane_full == 0, 1.0, 0.0)
        kh_ref[0] = jnp.where(low, kk, one_k).astype(BF16)
        kh_ref[1] = jnp.where(low, kk_sw, one_k).astype(BF16)
        ve_ref[0] = jnp.where(low, vv, one_e).astype(BF16)
        ve_ref[1] = jnp.where(low, vv_sw, one_e).astype(BF16)
        vo_ref[0] = jnp.where(low, one_o, vv_sw).astype(BF16)
        vo_ref[1] = jnp.where(low, one_o, vv).astype(BF16)
        return proj(_C_DL, _C_DL + RANK_PAD).astype(BF16)

    def log_decay(d_low, h):
        cols = slice(h * GLA_DK, (h + 1) * GLA_DK)
        z = _dot(d_low, wup_ref[:, cols]) + bdec_ref[:, cols]
        log_sig = jnp.minimum(z, 0.0) - jnp.log1p(jnp.exp(-jnp.abs(z)))
        la_ref[:, cols] = log_sig * (LOG2_E / GLA_TAU)

    def plain(ref, col0, i, scale=None):
        y = proj(col0 + i * MXU_COLS, col0 + (i + 1) * MXU_COLS)
        if scale is not None:
            y = y * scale
        ref[:, i * MXU_COLS:(i + 1) * MXU_COLS] = y.astype(BF16)

    plain_tiles = (
        [(qa_ref, _C_QA, i, GLA_DK ** -0.5) for i in range(GLA_QK // MXU_COLS)]
        + [(ka_ref, _C_KA, i, None) for i in range(GLA_QK // MXU_COLS)]
        + [(va_ref, _C_VA, i, None) for i in range(GLA_V // MXU_COLS)]
        + [(ra_ref, _C_RA, i, None) for i in range(GLA_V // MXU_COLS)]
        + [(ga_ref, _C_GA, i, None) for i in range(D_MODEL // MXU_COLS)]
        + [(gb_ref, _C_GB, i, None) for i in range(D_MODEL // MXU_COLS)])
    d_low = swa_keys_values_and_decay()
    heavy = []
    for i in range(max(SWA_Q // MXU_COLS, GLA_HEADS)):
        if i < SWA_Q // MXU_COLS:
            heavy.append(functools.partial(swa_queries, i))
        if i < GLA_HEADS:
            heavy.append(functools.partial(log_decay, d_low, i))
    per_heavy = len(plain_tiles) // len(heavy)
    for n, task in enumerate(heavy):
        task()
        first = n * per_heavy
        last = len(plain_tiles) if n == len(heavy) - 1 else first + per_heavy
        for args in plain_tiles[first:last]:
            plain(*args)


def _rope_spread():
    half = ROPE_DIM // 2
    rows = 4 * half
    e = np.zeros((rows, 3 * LANES), np.float32)
    for lane in range(LANES):
        j = lane % SWA_HEAD_DIM
        if j < half:
            e[j, lane] = 1.0
            e[half + j, LANES + lane] = -1.0
        elif j < 2 * half:
            e[j - half, lane] = 1.0
            e[j, 2 * LANES + lane] = 1.0
        else:
            e[2 * half, lane] = 1.0
    return jnp.asarray(np.concatenate([e, e], axis=0), dtype=BF16)


def _inproj(x2, pos_row, invf, sink_feat, w_in_r, w_up_p, b_dec, tm):
    m = x2.shape[0]
    half = ROPE_DIM // 2
    row = lambda c: pl.BlockSpec((tm, c), lambda i: (i, 0))
    row_out = lambda c, d: (row(c), jax.ShapeDtypeStruct((m, c), d))
    slab_out = lambda n: (pl.BlockSpec((n, tm, LANES), lambda i: (0, i, 0)),
                          jax.ShapeDtypeStruct((n, m, LANES), BF16))
    outs = [row_out(GLA_QK, BF16), row_out(GLA_QK, BF16), row_out(GLA_V, BF16),
            row_out(GLA_V, BF16), row_out(GLA_QK, F32),
            slab_out(SWA_HEADS), slab_out(SWA_KV_HEADS), slab_out(SWA_KV_HEADS),
            slab_out(SWA_KV_HEADS), row_out(D_MODEL, BF16), row_out(D_MODEL, BF16)]
    return pl.pallas_call(
        _inproj_kernel,
        grid=(m // tm,),
        in_specs=[row(D_MODEL), pl.BlockSpec((1, tm), lambda i: (0, i)),
                  _resident((half, LANES)), _resident((8 * half, 3 * LANES)),
                  _resident((SWA_HEADS, LANES)),
                  _resident((_C_END, D_MODEL)), _resident((RANK_PAD, GLA_QK)),
                  _resident((1, GLA_QK))],
        out_specs=[o[0] for o in outs],
        out_shape=[o[1] for o in outs],
        compiler_params=pltpu.CompilerParams(
            dimension_semantics=("arbitrary",), vmem_limit_bytes=VMEM_LIMIT),
        name="inproj",
    )(x2, pos_row, invf, _rope_spread(), sink_feat, w_in_r, w_up_p, b_dec)


def _gla_kernel(q_ref, k_ref, v_ref, r_ref, la_ref, g_ref, o_ref, st_ref, *, n_chunks):
    @pl.when(pl.program_id(1) == 0)
    def _():
        st_ref[...] = jnp.zeros_like(st_ref)

    c_ = GLA_CHUNK
    causal = (lax.broadcasted_iota(jnp.int32, (c_, c_), 0)
              >= lax.broadcasted_iota(jnp.int32, (c_, c_), 1))
    tri = jnp.where(causal, 1.0, 0.0).astype(BF16)
    gain = g_ref[...]

    items =[(c, h) for c in range(n_chunks) for h in range(GLA_HEADS)]
    rows = lambda c: slice(c * c_, (c + 1) * c_)
    ks = lambda h: slice(h * GLA_DK, (h + 1) * GLA_DK)
    vs = lambda h: slice(h * GLA_DV, (h + 1) * GLA_DV)

    b = []
    for c in range(n_chunks):
        la = la_ref[rows(c), :]
        hi = la.astype(BF16)
        rem = la - hi.astype(F32)
        mid = rem.astype(BF16)
        lo = (rem - mid.astype(F32)).astype(BF16)
        b.append(_dot(tri, hi) + _dot(tri, mid) + _dot(tri, lo))

    q_in, scores, contrib, decay = {}, {}, {}, {}
    for c, h in items:
        bh = b[c][:, ks(h)]
        b_last = bh[c_ - 1:c_, :]
        q = q_ref[rows(c), ks(h)].astype(F32)
        k = k_ref[rows(c), ks(h)].astype(F32)
        q_in[c, h] = (q * jnp.exp2(bh)).astype(BF16)
        k_in = (k * jnp.exp2(-bh)).astype(BF16)
        k_out = (k * jnp.exp2(b_last - bh)).astype(BF16)
        scores[c, h] = _dot_nt(q_in[c, h], k_in)
        contrib[c, h] = _dot_tn(k_out, v_ref[rows(c), vs(h)])
        dec = jnp.exp2(jnp.broadcast_to(b_last, (GLA_DK, GLA_DK)).T)
        decay[c, h] = jnp.concatenate([dec] * (GLA_DV // GLA_DK), axis=1)

    o_intra = {}
    for c, h in items:
        masked = jnp.where(causal, scores[c, h], 0.0).astype(BF16)
        o_intra[c, h] = _dot(masked, v_ref[rows(c), vs(h)])

    st = [st_ref[h] for h in range(GLA_HEADS)]
    for c, h in items:
        o = o_intra[c, h] + _dot(q_in[c, h], st[h].astype(BF16))
        st[h] = st[h] * decay[c, h] + contrib[c, h]
        o = o * lax.rsqrt(jnp.mean(o * o, axis=-1, keepdims=True) + RMS_EPS)
        r = r_ref[rows(c), vs(h)].astype(F32)
        o = o * gain * (r * jax.nn.sigmoid(r))
        o_ref[rows(c), vs(h)] = o.astype(BF16)
    for h in range(GLA_HEADS):
        st_ref[h] = st[h]


def _gla(qa, ka, va, ra, la, gain, batch, seq, tt):
    m = qa.shape[0]
    nt = seq // tt
    row = lambda c: pl.BlockSpec((tt, c), lambda b, t: (b * nt + t, 0))
    return pl.pallas_call(
        functools.partial(_gla_kernel, n_chunks=tt // GLA_CHUNK),
        grid=(batch, nt),
        in_specs=[row(GLA_QK), row(GLA_QK), row(GLA_V), row(GLA_V), row(GLA_QK),
                  pl.BlockSpec((1, GLA_DV), lambda b, t: (0, 0))],
        out_specs=row(GLA_V),
        out_shape=jax.ShapeDtypeStruct((m, GLA_V), BF16),
        scratch_shapes=[pltpu.VMEM((GLA_HEADS, GLA_DK, GLA_DV), F32)],
        compiler_params=pltpu.CompilerParams(
            dimension_semantics=("arbitrary", "arbitrary"), vmem_limit_bytes=VMEM_LIMIT),
        name="gla",
    )(qa, ka, va, ra, la, gain)


def _swa_kernel(q_ref, kc_ref, kp_ref, vec_ref, vep_ref, voc_ref, vop_ref, mask_ref, o_ref,
                *, blocks_per_step):
    w = SWA_WINDOW
    dh = SWA_HEAD_DIM
    per_group = SWA_HEADS // SWA_KV_HEADS
    first = pl.program_id(1) == 0
    row = lax.broadcasted_iota(jnp.int32, (w, LANES), 0)
    lane = lax.broadcasted_iota(jnp.int32, (w, LANES), 1)
    ident = jnp.where(row == lane, 1.0, 0.0).astype(BF16)
    sink_row = row == 0
    low = lane < dh
    zero = jnp.zeros((w, LANES), BF16)
    one_e = jnp.where(lane == dh, 1.0, 0.0).astype(BF16)
    one_o = jnp.where(lane == 0, 1.0, 0.0).astype(BF16)
    order = list(range(0, per_group, 2)) + list(range(1, per_group, 2))

    scores = []
    values = []
    for qb in range(blocks_per_step):
        rows = slice(qb * w, (qb + 1) * w)
        prev_rows = slice((qb - 1) * w, qb * w)
        mask = mask_ref[1] if qb else jnp.where(first, mask_ref[0], mask_ref[1])
        for g in range(SWA_KV_HEADS):
            k_prev = kc_ref[g, prev_rows] if qb else kp_ref[g]
            ve_prev = vec_ref[g, prev_rows] if qb else vep_ref[g]
            vo_prev = voc_ref[g, prev_rows] if qb else vop_ref[g]
            k2 = jnp.concatenate([jnp.where(sink_row, zero, k_prev), kc_ref[g, rows]], axis=0)
            kx = jnp.concatenate([k2, mask], axis=1)
            ve = jnp.concatenate([jnp.where(sink_row, one_e, ve_prev), vec_ref[g, rows]], axis=0)
            vo = jnp.concatenate([jnp.where(sink_row, one_o, vo_prev), voc_ref[g, rows]], axis=0)
            for part, vv in ((order[:per_group // 2], ve), (order[per_group // 2:], vo)):
                qs = jnp.concatenate(
                    [jnp.concatenate([q_ref[g * per_group + h, rows], ident], axis=1)
                     for h in part], axis=0)
                scores.append(_dot_nt(qs, kx))
                values.append(vv)
    pv = []
    for s, vv in zip(scores, values):
        p = jnp.exp2(s - jnp.max(s, axis=-1, keepdims=True)).astype(BF16)
        pv.append(_dot(p, vv))
    for qb in range(blocks_per_step):
        rows = slice(qb * w, (qb + 1) * w)
        for g in range(SWA_KV_HEADS):
            pv_e = pv[(qb * SWA_KV_HEADS + g) * 2]
            pv_o = pv[(qb * SWA_KV_HEADS + g) * 2 + 1]
            for c in range(per_group // 2):
                e = pv_e[c * w:(c + 1) * w]
                o = pv_o[c * w:(c + 1) * w]
                out = jnp.where(low, e * (1.0 / e[:, dh:dh + 1]), o * (1.0 / o[:, 0:1]))
                col = (g * (per_group // 2) + c) * LANES
                o_ref[rows, col:col + LANES] = out.astype(BF16)


def _swa_mask():
    w = SWA_WINDOW
    j = np.arange(2 * w)[:, None]
    i = np.arange(w)[None, :]
    band = (j > i) & (j <= i + w)
    later = band | (j == 0)
    first = (band & (j >= w)) | (j == 0)
    both = np.stack([first, later])
    return jnp.asarray(np.where(both, 0.0, MASK_BIAS), dtype=BF16)


def _swa(qh, kh, ve, vo, batch, seq, blocks_per_step):
    m = kh.shape[1]
    w = SWA_WINDOW
    tq = blocks_per_step * w
    steps = seq // tq
    kv = SWA_KV_HEADS
    cur = lambda n_: pl.BlockSpec((n_, tq, LANES), lambda b, t: (0, b * steps + t, 0))
    prev = lambda n_: pl.BlockSpec(
        (n_, w, LANES),
        lambda b, t: (0, b * (seq // w) + jnp.maximum(t * blocks_per_step - 1, 0), 0))
    return pl.pallas_call(
        functools.partial(_swa_kernel, blocks_per_step=blocks_per_step),
        grid=(batch, steps),
        in_specs=[cur(SWA_HEADS), cur(kv), prev(kv), cur(kv), prev(kv), cur(kv), prev(kv),
                  _resident((2, 2 * w, w))],
        out_specs=pl.BlockSpec((tq, SWA_Q), lambda b, t: (b * steps + t, 0)),
        out_shape=jax.ShapeDtypeStruct((m, SWA_Q), BF16),
        compiler_params=pltpu.CompilerParams(
            dimension_semantics=("arbitrary", "arbitrary"), vmem_limit_bytes=VMEM_LIMIT),
        name="swa",
    )(qh, kh, kh, ve, ve, vo, vo, _swa_mask())


def _layer_norm(y, g, b):
    mu = jnp.mean(y, axis=-1, keepdims=True)
    d = y - mu
    var = jnp.mean(d * d, axis=-1, keepdims=True)
    return d * lax.rsqrt(var + LN_EPS) * g + b


def _merge_kernel(x_ref, oa_ref, ob_ref, ga_ref, gb_ref, wa_ref, wb_ref, wo_ref,
                  g_ref, b_ref, h_ref):
    tm = x_ref.shape[0]
    parts = [slice(i * (tm // MERGE_SPLIT), (i + 1) * (tm // MERGE_SPLIT))
             for i in range(MERGE_SPLIT)]
    branches = [(_dot(oa_ref[rs, :], wa_ref[...]), _dot(ob_ref[rs, :], wb_ref[...]))
                for rs in parts]
    for rs, (y_a, y_b) in zip(parts, branches):
        merged = (jax.nn.sigmoid(ga_ref[rs, :].astype(F32)) * y_a
                  + jax.nn.sigmoid(gb_ref[rs, :].astype(F32)) * y_b)
        mix = _dot(merged.astype(BF16), wo_ref[...])
        h_ref[rs, :] = _layer_norm(DEEPNORM_ALPHA * x_ref[rs, :] + mix, g_ref[...], b_ref[...])


def _merge(x2, oa, ob, ga, gb, wa, wb, wo, ln_g, ln_b, tm):
    m = x2.shape[0]
    row = lambda c: pl.BlockSpec((tm, c), lambda i: (i, 0))
    sq = _resident((D_MODEL, D_MODEL))
    vec = _resident((1, D_MODEL))
    return pl.pallas_call(
        _merge_kernel,
        grid=(m // tm,),
        in_specs=[row(D_MODEL)] * 5 + [sq, sq, sq, vec, vec],
        out_specs=row(D_MODEL),
        out_shape=jax.ShapeDtypeStruct((m, D_MODEL), F32),
        compiler_params=pltpu.CompilerParams(
            dimension_semantics=("arbitrary",), vmem_limit_bytes=VMEM_LIMIT),
        name="merge",
    )(x2, oa, ob, ga, gb, wa, wb, wo, ln_g, ln_b)


def _ffn_kernel(h_ref, wg_ref, wu_ref, wd_ref, g_ref, b_ref, o_ref, *, ff_chunk):
    d_ff = wg_ref.shape[1]
    n_chunks = d_ff // ff_chunk
    sl = lambda j: slice(j * ff_chunk, (j + 1) * ff_chunk)
    tm = h_ref.shape[0]
    for i in range(FFN_SPLIT):
        rs = slice(i * (tm // FFN_SPLIT), (i + 1) * (tm // FFN_SPLIT))
        h = h_ref[rs, :]
        hb = h.astype(BF16)

        def gate_up(j):
            return _dot(hb, wg_ref[:, sl(j)]), _dot(hb, wu_ref[:, sl(j)])

        acc = None
        gu = gate_up(0)
        for j in range(n_chunks):
            gate, up = gu
            if j + 1 < n_chunks:
                gu = gate_up(j + 1)
            act = (gate * jax.nn.sigmoid(gate) * up).astype(BF16)
            part = _dot(act, wd_ref[sl(j), :])
            acc = part if acc is None else acc + part
        o_ref[rs, :] = _layer_norm(DEEPNORM_ALPHA * h + acc, g_ref[...], b_ref[...])


def _ffn(h1, wg, wu, wd, ln_g, ln_b, tm, ff_chunk):
    m = h1.shape[0]
    d_ff = wg.shape[1]
    row = pl.BlockSpec((tm, D_MODEL), lambda i: (i, 0))
    vec = _resident((1, D_MODEL))
    return pl.pallas_call(
        functools.partial(_ffn_kernel, ff_chunk=ff_chunk),
        grid=(m // tm,),
        in_specs=[row, _resident((D_MODEL, d_ff)), _resident((D_MODEL, d_ff)),
                  _resident((d_ff, D_MODEL)), vec, vec],
        out_specs=row,
        out_shape=jax.ShapeDtypeStruct((m, D_MODEL), F32),
        compiler_params=pltpu.CompilerParams(
            dimension_semantics=("arbitrary",), vmem_limit_bytes=VMEM_LIMIT),
        name="ffn",
    )(h1, wg, wu, wd, ln_g, ln_b)


def _rope_inv_freq():
    inv_freq = ROPE_THETA ** (-jnp.arange(0, ROPE_DIM, 2, dtype=F32) / ROPE_DIM)
    return jnp.broadcast_to(inv_freq[:, None], (ROPE_DIM // 2, LANES))


def _layer(h2, pos_row, invf, batch, seq, w_in, w_decay_up, b_decay, gla_norm_g,
           w_branch_a, w_branch_b, sinks, w_out, ln1_g, ln1_b,
           w_ffn_gate, w_ffn_up, w_ffn_down, ln2_g, ln2_b):
    w_in_r = w_in.T.astype(BF16)
    w_up_p = jnp.concatenate(
        [w_decay_up, jnp.zeros((RANK_PAD - GLA_RANK, GLA_QK), w_decay_up.dtype)],
        axis=0).astype(BF16)

    neg_sink = -sinks.astype(F32) * LOG2_E
    sink_hi = neg_sink.astype(BF16).astype(F32)
    sink_feat = jnp.zeros((SWA_HEADS, LANES), F32)
    sink_feat = sink_feat.at[:, SWA_HEAD_DIM].set(sink_hi)
    sink_feat = sink_feat.at[:, SWA_HEAD_DIM + 1].set(neg_sink - sink_hi)

    qa, ka, va, ra, la, qh, kh, ve, vo, ga, gb = _inproj(
        h2, pos_row, invf, sink_feat, w_in_r, w_up_p, b_decay[None, :], tm=512)
    oa = _gla(qa, ka, va, ra, la, gla_norm_g[None, :], batch, seq, tt=512)
    ob = _swa(qh, kh, ve, vo, batch, seq, blocks_per_step=4)
    h1 = _merge(h2, oa, ob, ga, gb, w_branch_a.astype(BF16), w_branch_b.astype(BF16),
                w_out.astype(BF16), ln1_g[None, :], ln1_b[None, :], tm=1024)
    return _ffn(h1, w_ffn_gate.astype(BF16), w_ffn_up.astype(BF16), w_ffn_down.astype(BF16),
                ln2_g[None, :], ln2_b[None, :], tm=1024, ff_chunk=256)


def kernel(x, positions, w_in, w_decay_up, b_decay, gla_norm_g, w_branch_a, w_branch_b, sinks,
           w_out, ln1_g, ln1_b, w_ffn_gate, w_ffn_up, w_ffn_down, ln2_g, ln2_b):
    batch, seq, d = x.shape
    m = batch * seq
    h2 = x.reshape(m, d)
    pos_row = positions.reshape(1, m).astype(F32)
    invf = _rope_inv_freq()
    for layer in range(w_in.shape[0]):
        h2 = _layer(h2, pos_row, invf, batch, seq, w_in[layer], w_decay_up[layer], b_decay[layer],
                    gla_norm_g[layer], w_branch_a[layer], w_branch_b[layer], sinks[layer],
                    w_out[layer], ln1_g[layer], ln1_b[layer], w_ffn_gate[layer],
                    w_ffn_up[layer], w_ffn_down[layer], ln2_g[layer], ln2_b[layer])
    return h2.reshape(batch, seq, d)
```

```python
import functools

import math

import jax
import jax.numpy as jnp
import numpy as np
from jax import lax
from jax.experimental import pallas as pl
from jax.experimental.pallas import tpu as pltpu

D_MODEL = 1024
GLA_HEADS = 4
GLA_DK = 128
GLA_DV = 256
GLA_RANK = 16
GLA_TAU = 16.0
GLA_CHUNK = 64
SWA_HEADS = 16
SWA_KV_HEADS = 2
SWA_HEAD_DIM = 64
SWA_WINDOW = 128
ROPE_DIM = 16
ROPE_THETA = 500000.0
DEPTH = 1
DEEPNORM_ALPHA = (2.0 * DEPTH) ** 0.25
LN_EPS = 1e-5
RMS_EPS = 1e-6

GLA_QK = GLA_HEADS * GLA_DK
GLA_V = GLA_HEADS * GLA_DV
SWA_Q = SWA_HEADS * SWA_HEAD_DIM
SWA_KV = SWA_KV_HEADS * SWA_HEAD_DIM

LANES = 128
MXU_COLS = 256
RANK_PAD = LANES
VMEM_LIMIT = 56 * 1024 * 1024
MERGE_SPLIT = 4
FFN_SPLIT = 4
LOG2_E = math.log2(math.e)
MASK_BIAS = -1e30

_C_QA = 0
_C_KA = _C_QA + GLA_QK
_C_VA = _C_KA + GLA_QK
_C_RA = _C_VA + GLA_V
_C_QB = _C_RA + GLA_V
_C_KV = _C_QB + SWA_Q
_C_GA = _C_KV + 2 * SWA_KV
_C_GB = _C_GA + D_MODEL
_C_DL = _C_GB + D_MODEL
_C_END = _C_DL + RANK_PAD
_D_IN = 2 * GLA_QK + 2 * GLA_V + GLA_RANK + SWA_Q + 2 * SWA_KV + 2 * D_MODEL
_O_DL = 2 * GLA_QK + 2 * GLA_V

F32 = jnp.float32
BF16 = jnp.bfloat16


def _dot(a, b):
    return jnp.dot(a, b, preferred_element_type=F32)


def _dot_nt(a, b):
    return lax.dot_general(a, b, (((1,), (1,)), ((), ())), preferred_element_type=F32)


def _dot_tn(a, b):
    return lax.dot_general(a, b, (((0,), (0,)), ((), ())), preferred_element_type=F32)


def _resident(shape):
    return pl.BlockSpec(shape, lambda *_: (0,) * len(shape), pipeline_mode=pl.Buffered(1))


def _win_prep_kernel(wt_ref, o_ref):
    def put(col, row):
        o_ref[:, col:col + LANES] = wt_ref[row:row + LANES, :].T.astype(BF16)

    for j in range(_O_DL // LANES):
        put(j * LANES, j * LANES)
    for j in range((_C_DL - _O_DL) // LANES):
        put(_O_DL + j * LANES, _O_DL + GLA_RANK + j * LANES)
    blk = wt_ref[_O_DL:_O_DL + LANES, :].T
    lane = lax.broadcasted_iota(jnp.int32, blk.shape, 1)
    o_ref[:, _C_DL:] = jnp.where(lane < GLA_RANK, blk, 0.0).astype(BF16)


def _win_prep(w_in_t):
    cb = LANES
    assert _C_DL - _O_DL == _D_IN - _O_DL - GLA_RANK and (_C_DL - _O_DL) % LANES == 0
    return pl.pallas_call(
        _win_prep_kernel,
        grid=(D_MODEL // cb,),
        in_specs=[pl.BlockSpec((_D_IN, cb), lambda i: (0, i))],
        out_specs=pl.BlockSpec((cb, _C_END), lambda i: (i, 0)),
        out_shape=jax.ShapeDtypeStruct((D_MODEL, _C_END), BF16),
        compiler_params=pltpu.CompilerParams(dimension_semantics=("arbitrary",)),
        name="win_prep",
    )(w_in_t)


def _inproj_kernel(x_ref, pos_ref, invf_ref, spread_ref, sf_ref, w_ref, wup_ref, bdec_ref,
                   qa_ref, ka_ref, va_ref, ra_ref, la_ref,
                   qh_ref, kh_ref, ve_ref, vo_ref, ga_ref, gb_ref):
    xb = x_ref[...].astype(BF16)

    def proj(lo, hi):
        return _dot(xb, w_ref[:, lo:hi])

    tm = x_ref.shape[0]
    half = ROPE_DIM // 2
    ang = jnp.tile(invf_ref[...], (1, tm // LANES)) * pos_ref[...]
    one_row = jnp.where(lax.broadcasted_iota(jnp.int32, (2 * half, tm), 0) == 0, 1.0, 0.0)
    cs = jnp.concatenate([jnp.cos(ang), jnp.sin(ang), one_row], axis=0)
    cs_hi = cs.astype(BF16)
    cs_lo = (cs - cs_hi.astype(F32)).astype(BF16)
    tabs = _dot_tn(jnp.concatenate([cs_hi, cs_lo], axis=0), spread_ref[...])
    cos, a_tab, b_tab = tabs[:, :LANES], tabs[:, LANES:2 * LANES], tabs[:, 2 * LANES:]

    def rope(t):
        return (t * cos + pltpu.roll(t, LANES - half, 1) * a_tab
                + pltpu.roll(t, half, 1) * b_tab)

    dh = SWA_HEAD_DIM
    lane_full = lax.broadcasted_iota(jnp.int32, (tm, LANES), 1)
    low = lane_full < dh
    q_scale = (dh ** -0.5) * LOG2_E

    def swa_queries(i):
        qb = proj(_C_QB + i * MXU_COLS, _C_QB + (i + 1) * MXU_COLS)
        for c in range(2 * i, 2 * i + 2):
            t = rope(qb[:, (c - 2 * i) * LANES:(c - 2 * i + 1) * LANES]) * q_scale
            qh_ref[2 * c] = jnp.where(low, t, sf_ref[2 * c:2 * c + 1, :]).astype(BF16)
            qh_ref[2 * c + 1] = jnp.where(
                low, pltpu.roll(t, dh, 1), sf_ref[2 * c + 1:2 * c + 2, :]).astype(BF16)

    def swa_keys_values_and_decay():
        kv = proj(_C_KV, _C_GA)
        kk = rope(kv[:, :SWA_KV])
        vv = kv[:, SWA_KV:]
        kk_sw = pltpu.roll(kk, dh, 1)
        vv_sw = pltpu.roll(vv, dh, 1)
        one_k = jnp.where(lane_full < dh + 2, 1.0, 0.0)
        one_e = jnp.where(lane_full == dh, 1.0, 0.0)
        one_o = jnp.where(lane_full == 0, 1.0, 0.0)
        kh_ref[0] = jnp.where(low, kk, one_k).astype(BF16)
        kh_ref[1] = jnp.where(low, kk_sw, one_k).astype(BF16)
        ve_ref[0] = jnp.where(low, vv, one_e).astype(BF16)
        ve_ref[1] = jnp.where(low, vv_sw, one_e).astype(BF16)
        vo_ref[0] = jnp.where(low, one_o, vv_sw).astype(BF16)
        vo_ref[1] = jnp.where(low, one_o, vv).astype(BF16)
        return proj(_C_DL, _C_END).astype(BF16)

    def log_decay(d_low, h):
        cols = slice(h * GLA_DK, (h + 1) * GLA_DK)
        z = _dot(d_low, wup_ref[:, cols]) + bdec_ref[:, cols]
        log_sig = jnp.minimum(z, 0.0) - jnp.log1p(jnp.exp(-jnp.abs(z)))
        la_ref[:, cols] = log_sig * (LOG2_E / GLA_TAU)

    def plain(ref, col0, i, scale=None):
        y = proj(col0 + i * MXU_COLS, col0 + (i + 1) * MXU_COLS)
        if scale is not None:
            y = y * scale
        ref[:, i * MXU_COLS:(i + 1) * MXU_COLS] = y.astype(BF16)

    plain_tiles = (
        [(qa_ref, _C_QA, i, GLA_DK ** -0.5) for i in range(GLA_QK // MXU_COLS)]
        + [(ka_ref, _C_KA, i, None) for i in range(GLA_QK // MXU_COLS)]
        + [(va_ref, _C_VA, i, None) for i in range(GLA_V // MXU_COLS)]
        + [(ra_ref, _C_RA, i, None) for i in range(GLA_V // MXU_COLS)]
        + [(ga_ref, _C_GA, i, None) for i in range(D_MODEL // MXU_COLS)]
        + [(gb_ref, _C_GB, i, None) for i in range(D_MODEL // MXU_COLS)])
    d_low = swa_keys_values_and_decay()
    heavy = []
    for i in range(max(SWA_Q // MXU_COLS, GLA_HEADS)):
        if i < SWA_Q // MXU_COLS:
            heavy.append(functools.partial(swa_queries, i))
        if i < GLA_HEADS:
            heavy.append(functools.partial(log_decay, d_low, i))
    per_heavy = len(plain_tiles) // len(heavy)
    for n, task in enumerate(heavy):
        task()
        first = n * per_heavy
        last = len(plain_tiles) if n == len(heavy) - 1 else first + per_heavy
        for args in plain_tiles[first:last]:
            plain(*args)


def _rope_spread():
    half = ROPE_DIM // 2
    rows = 4 * half
    e = np.zeros((rows, 3 * LANES), np.float32)
    for lane in range(LANES):
        j = lane % SWA_HEAD_DIM
        if j < half:
            e[j, lane] = 1.0
            e[half + j, LANES + lane] = -1.0
        elif j < 2 * half:
            e[j - half, lane] = 1.0
            e[j, 2 * LANES + lane] = 1.0
        else:
            e[2 * half, lane] = 1.0
    return jnp.asarray(np.concatenate([e, e], axis=0), dtype=BF16)


def _inproj(x2, pos_row, invf, sink_feat, w_in_r, w_up_p, b_dec, tm):
    m = x2.shape[0]
    half = ROPE_DIM // 2
    row = lambda c: pl.BlockSpec((tm, c), lambda i: (i, 0))
    row_out = lambda c, d: (row(c), jax.ShapeDtypeStruct((m, c), d))
    slab_out = lambda n: (pl.BlockSpec((n, tm, LANES), lambda i: (0, i, 0)),
                          jax.ShapeDtypeStruct((n, m, LANES), BF16))
    outs = [row_out(GLA_QK, BF16), row_out(GLA_QK, BF16), row_out(GLA_V, BF16),
            row_out(GLA_V, BF16), row_out(GLA_QK, F32),
            slab_out(SWA_HEADS), slab_out(SWA_KV_HEADS), slab_out(SWA_KV_HEADS),
            slab_out(SWA_KV_HEADS), row_out(D_MODEL, BF16), row_out(D_MODEL, BF16)]
    return pl.pallas_call(
        _inproj_kernel,
        grid=(m // tm,),
        in_specs=[row(D_MODEL), pl.BlockSpec((1, tm), lambda i: (0, i)),
                  _resident((half, LANES)), _resident((8 * half, 3 * LANES)),
                  _resident((SWA_HEADS, LANES)),
                  _resident((D_MODEL, _C_END)), _resident((RANK_PAD, GLA_QK)),
                  _resident((1, GLA_QK))],
        out_specs=[o[0] for o in outs],
        out_shape=[o[1] for o in outs],
        compiler_params=pltpu.CompilerParams(
            dimension_semantics=("arbitrary",), vmem_limit_bytes=VMEM_LIMIT),
        name="inproj",
    )(x2, pos_row, invf, _rope_spread(), sink_feat, w_in_r, w_up_p, b_dec)


def _gla_kernel(q_ref, k_ref, v_ref, r_ref, la_ref, g_ref, o_ref, st_ref, *, n_chunks):
    @pl.when(pl.program_id(1) == 0)
    def _():
        st_ref[...] = jnp.zeros_like(st_ref)

    c_ = GLA_CHUNK
    causal = (lax.broadcasted_iota(jnp.int32, (c_, c_), 0)
              >= lax.broadcasted_iota(jnp.int32, (c_, c_), 1))
    tri = jnp.where(causal, 1.0, 0.0).astype(BF16)
    gain = g_ref[...]

    items =[(c, h) for c in range(n_chunks) for h in range(GLA_HEADS)]
    rows = lambda c: slice(c * c_, (c + 1) * c_)
    ks = lambda h: slice(h * GLA_DK, (h + 1) * GLA_DK)
    vs = lambda h: slice(h * GLA_DV, (h + 1) * GLA_DV)

    b = []
    for c in range(n_chunks):
        la = la_ref[rows(c), :]
        hi = la.astype(BF16)
        rem = la - hi.astype(F32)
        mid = rem.astype(BF16)
        lo = (rem - mid.astype(F32)).astype(BF16)
        b.append(_dot(tri, hi) + _dot(tri, mid) + _dot(tri, lo))

    q_in, scores, contrib, decay = {}, {}, {}, {}
    for c, h in items:
        bh = b[c][:, ks(h)]
        b_last = bh[c_ - 1:c_, :]
        q = q_ref[rows(c), ks(h)].astype(F32)
        k = k_ref[rows(c), ks(h)].astype(F32)
        q_in[c, h] = (q * jnp.exp2(bh)).astype(BF16)
        k_in = (k * jnp.exp2(-bh)).astype(BF16)
        k_out = (k * jnp.exp2(b_last - bh)).astype(BF16)
        scores[c, h] = _dot_nt(q_in[c, h], k_in)
        contrib[c, h] = _dot_tn(k_out, v_ref[rows(c), vs(h)])
        dec = jnp.exp2(jnp.broadcast_to(b_last, (GLA_DK, GLA_DK)).T)
        decay[c, h] = jnp.concatenate([dec] * (GLA_DV // GLA_DK), axis=1)

    o_intra = {}
    for c, h in items:
        masked = jnp.where(causal, scores[c, h], 0.0).astype(BF16)
        o_intra[c, h] = _dot(masked, v_ref[rows(c), vs(h)])

    st = [st_ref[h] for h in range(GLA_HEADS)]
    for c, h in items:
        o = o_intra[c, h] + _dot(q_in[c, h], st[h].astype(BF16))
        st[h] = st[h] * decay[c, h] + contrib[c, h]
        o = o * lax.rsqrt(jnp.mean(o * o, axis=-1, keepdims=True) + RMS_EPS)
        r = r_ref[rows(c), vs(h)].astype(F32)
        o = o * gain * (r * jax.nn.sigmoid(r))
        o_ref[rows(c), vs(h)] = o.astype(BF16)
    for h in range(GLA_HEADS):
        st_ref[h] = st[h]


def _gla(qa, ka, va, ra, la, gain, batch, seq, tt):
    m = qa.shape[0]
    nt = seq // tt
    row = lambda c: pl.BlockSpec((tt, c), lambda b, t: (b * nt + t, 0))
    return pl.pallas_call(
        functools.partial(_gla_kernel, n_chunks=tt // GLA_CHUNK),
        grid=(batch, nt),
        in_specs=[row(GLA_QK), row(GLA_QK), row(GLA_V), row(GLA_V), row(GLA_QK),
                  pl.BlockSpec((1, GLA_DV), lambda b, t: (0, 0))],
        out_specs=row(GLA_V),
        out_shape=jax.ShapeDtypeStruct((m, GLA_V), BF16),
        scratch_shapes=[pltpu.VMEM((GLA_HEADS, GLA_DK, GLA_DV), F32)],
        compiler_params=pltpu.CompilerParams(
            dimension_semantics=("arbitrary", "arbitrary"), vmem_limit_bytes=VMEM_LIMIT),
        name="gla",
    )(qa, ka, va, ra, la, gain)


def _swa_kernel(q_ref, kc_ref, kp_ref, vec_ref, vep_ref, voc_ref, vop_ref, mask_ref, o_ref,
                *, blocks_per_step):
    w = SWA_WINDOW
    dh = SWA_HEAD_DIM
    per_group = SWA_HEADS // SWA_KV_HEADS
    first = pl.program_id(1) == 0
    row = lax.broadcasted_iota(jnp.int32, (w, LANES), 0)
    lane = lax.broadcasted_iota(jnp.int32, (w, LANES), 1)
    ident = jnp.where(row == lane, 1.0, 0.0).astype(BF16)
    sink_row = row == 0
    low = lane < dh
    zero = jnp.zeros((w, LANES), BF16)
    one_e = jnp.where(lane == dh, 1.0, 0.0).astype(BF16)
    one_o = jnp.where(lane == 0, 1.0, 0.0).astype(BF16)
    order = list(range(0, per_group, 2)) + list(range(1, per_group, 2))

    scores = []
    values = []
    for qb in range(blocks_per_step):
        rows = slice(qb * w, (qb + 1) * w)
        prev_rows = slice((qb - 1) * w, qb * w)
        mask = mask_ref[1] if qb else jnp.where(first, mask_ref[0], mask_ref[1])
        for g in range(SWA_KV_HEADS):
            k_prev = kc_ref[g, prev_rows] if qb else kp_ref[g]
            ve_prev = vec_ref[g, prev_rows] if qb else vep_ref[g]
            vo_prev = voc_ref[g, prev_rows] if qb else vop_ref[g]
            k2 = jnp.concatenate([jnp.where(sink_row, zero, k_prev), kc_ref[g, rows]], axis=0)
            kx = jnp.concatenate([k2, mask], axis=1)
            ve = jnp.concatenate([jnp.where(sink_row, one_e, ve_prev), vec_ref[g, rows]], axis=0)
            vo = jnp.concatenate([jnp.where(sink_row, one_o, vo_prev), voc_ref[g, rows]], axis=0)
            for part, vv in ((order[:per_group // 2], ve), (order[per_group // 2:], vo)):
                qs = jnp.concatenate(
                    [jnp.concatenate([q_ref[g * per_group + h, rows], ident], axis=1)
                     for h in part], axis=0)
                scores.append(_dot_nt(qs, kx))
                values.append(vv)
    pv = []
    for s, vv in zip(scores, values):
        p = jnp.exp2(s - jnp.max(s, axis=-1, keepdims=True)).astype(BF16)
        pv.append(_dot(p, vv))
    for qb in range(blocks_per_step):
        rows = slice(qb * w, (qb + 1) * w)
        for g in range(SWA_KV_HEADS):
            pv_e = pv[(qb * SWA_KV_HEADS + g) * 2]
            pv_o = pv[(qb * SWA_KV_HEADS + g) * 2 + 1]
            for c in range(per_group // 2):
                e = pv_e[c * w:(c + 1) * w]
                o = pv_o[c * w:(c + 1) * w]
                out = jnp.where(low, e * (1.0 / e[:, dh:dh + 1]), o * (1.0 / o[:, 0:1]))
                col = (g * (per_group // 2) + c) * LANES
                o_ref[rows, col:col + LANES] = out.astype(BF16)


def _swa_mask():
    w = SWA_WINDOW
    j = np.arange(2 * w)[:, None]
    i = np.arange(w)[None, :]
    band = (j > i) & (j <= i + w)
    later = band | (j == 0)
    first = (band & (j >= w)) | (j == 0)
    both = np.stack([first, later])
    return jnp.asarray(np.where(both, 0.0, MASK_BIAS), dtype=BF16)


def _swa(qh, kh, ve, vo, batch, seq, blocks_per_step):
    m = kh.shape[1]
    w = SWA_WINDOW
    tq = blocks_per_step * w
    steps = seq // tq
    kv = SWA_KV_HEADS
    cur = lambda n_: pl.BlockSpec((n_, tq, LANES), lambda b, t: (0, b * steps + t, 0))
    prev = lambda n_: pl.BlockSpec(
        (n_, w, LANES),
        lambda b, t: (0, b * (seq // w) + jnp.maximum(t * blocks_per_step - 1, 0), 0))
    return pl.pallas_call(
        functools.partial(_swa_kernel, blocks_per_step=blocks_per_step),
        grid=(batch, steps),
        in_specs=[cur(SWA_HEADS), cur(kv), prev(kv), cur(kv), prev(kv), cur(kv), prev(kv),
                  _resident((2, 2 * w, w))],
        out_specs=pl.BlockSpec((tq, SWA_Q), lambda b, t: (b * steps + t, 0)),
        out_shape=jax.ShapeDtypeStruct((m, SWA_Q), BF16),
        compiler_params=pltpu.CompilerParams(
            dimension_semantics=("arbitrary", "arbitrary"), vmem_limit_bytes=VMEM_LIMIT),
        name="swa",
    )(qh, kh, kh, ve, ve, vo, vo, _swa_mask())


def _layer_norm(y, g, b):
    mu = jnp.mean(y, axis=-1, keepdims=True)
    d = y - mu
    var = jnp.mean(d * d, axis=-1, keepdims=True)
    return d * lax.rsqrt(var + LN_EPS) * g + b


def _merge_kernel(x_ref, oa_ref, ob_ref, ga_ref, gb_ref, wa_ref, wb_ref, wo_ref,
                  g_ref, b_ref, h_ref):
    tm = x_ref.shape[0]
    parts = [slice(i * (tm // MERGE_SPLIT), (i + 1) * (tm // MERGE_SPLIT))
             for i in range(MERGE_SPLIT)]
    branches = [(_dot(oa_ref[rs, :], wa_ref[...]), _dot(ob_ref[rs, :], wb_ref[...]))
                for rs in parts]
    for rs, (y_a, y_b) in zip(parts, branches):
        merged = (jax.nn.sigmoid(ga_ref[rs, :].astype(F32)) * y_a
                  + jax.nn.sigmoid(gb_ref[rs, :].astype(F32)) * y_b)
        mix = _dot(merged.astype(BF16), wo_ref[...])
        h_ref[rs, :] = _layer_norm(DEEPNORM_ALPHA * x_ref[rs, :] + mix, g_ref[...], b_ref[...])


def _merge(x2, oa, ob, ga, gb, wa, wb, wo, ln_g, ln_b, tm):
    m = x2.shape[0]
    row = lambda c: pl.BlockSpec((tm, c), lambda i: (i, 0))
    sq = _resident((D_MODEL, D_MODEL))
    vec = _resident((1, D_MODEL))
    return pl.pallas_call(
        _merge_kernel,
        grid=(m // tm,),
        in_specs=[row(D_MODEL)] * 5 + [sq, sq, sq, vec, vec],
        out_specs=row(D_MODEL),
        out_shape=jax.ShapeDtypeStruct((m, D_MODEL), F32),
        compiler_params=pltpu.CompilerParams(
            dimension_semantics=("arbitrary",), vmem_limit_bytes=VMEM_LIMIT),
        name="merge",
    )(x2, oa, ob, ga, gb, wa, wb, wo, ln_g, ln_b)


def _ffn_kernel(h_ref, wg_ref, wu_ref, wd_ref, g_ref, b_ref, o_ref, *, ff_chunk):
    d_ff = wg_ref.shape[1]
    n_chunks = d_ff // ff_chunk
    sl = lambda j: slice(j * ff_chunk, (j + 1) * ff_chunk)
    tm = h_ref.shape[0]
    for i in range(FFN_SPLIT):
        rs = slice(i * (tm // FFN_SPLIT), (i + 1) * (tm // FFN_SPLIT))
        h = h_ref[rs, :]
        hb = h.astype(BF16)

        def gate_up(j):
            return _dot(hb, wg_ref[:, sl(j)]), _dot(hb, wu_ref[:, sl(j)])

        acc = None
        gu = gate_up(0)
        for j in range(n_chunks):
            gate, up = gu
            if j + 1 < n_chunks:
                gu = gate_up(j + 1)
            act = (gate * jax.nn.sigmoid(gate) * up).astype(BF16)
            part = _dot(act, wd_ref[sl(j), :])
            acc = part if acc is None else acc + part
        o_ref[rs, :] = _layer_norm(DEEPNORM_ALPHA * h + acc, g_ref[...], b_ref[...])


def _ffn(h1, wg, wu, wd, ln_g, ln_b, tm, ff_chunk):
    m = h1.shape[0]
    d_ff = wg.shape[1]
    row = pl.BlockSpec((tm, D_MODEL), lambda i: (i, 0))
    vec = _resident((1, D_MODEL))
    return pl.pallas_call(
        functools.partial(_ffn_kernel, ff_chunk=ff_chunk),
        grid=(m // tm,),
        in_specs=[row, _resident((D_MODEL, d_ff)), _resident((D_MODEL, d_ff)),
                  _resident((d_ff, D_MODEL)), vec, vec],
        out_specs=row,
        out_shape=jax.ShapeDtypeStruct((m, D_MODEL), F32),
        compiler_params=pltpu.CompilerParams(
            dimension_semantics=("arbitrary",), vmem_limit_bytes=VMEM_LIMIT),
        name="ffn",
    )(h1, wg, wu, wd, ln_g, ln_b)


def _rope_inv_freq():
    inv_freq = ROPE_THETA ** (-jnp.arange(0, ROPE_DIM, 2, dtype=F32) / ROPE_DIM)
    return jnp.broadcast_to(inv_freq[:, None], (ROPE_DIM // 2, LANES))


def _layer(h2, pos_row, invf, batch, seq, w_in, w_decay_up, b_decay, gla_norm_g,
           w_branch_a, w_branch_b, sinks, w_out, ln1_g, ln1_b,
           w_ffn_gate, w_ffn_up, w_ffn_down, ln2_g, ln2_b):
    w_in_r = _win_prep(w_in.T)
    w_up_p = jnp.concatenate(
        [w_decay_up, jnp.zeros((RANK_PAD - GLA_RANK, GLA_QK), w_decay_up.dtype)],
        axis=0).astype(BF16)

    neg_sink = -sinks.astype(F32) * LOG2_E
    sink_hi = neg_sink.astype(BF16).astype(F32)
    sink_feat = jnp.zeros((SWA_HEADS, LANES), F32)
    sink_feat = sink_feat.at[:, SWA_HEAD_DIM].set(sink_hi)
    sink_feat = sink_feat.at[:, SWA_HEAD_DIM + 1].set(neg_sink - sink_hi)

    qa, ka, va, ra, la, qh, kh, ve, vo, ga, gb = _inproj(
        h2, pos_row, invf, sink_feat, w_in_r, w_up_p, b_decay[None, :], tm=512)
    oa = _gla(qa, ka, va, ra, la, gla_norm_g[None, :], batch, seq, tt=512)
    ob = _swa(qh, kh, ve, vo, batch, seq, blocks_per_step=4)
    h1 = _merge(h2, oa, ob, ga, gb, w_branch_a.astype(BF16), w_branch_b.astype(BF16),
                w_out.astype(BF16), ln1_g[None, :], ln1_b[None, :], tm=1024)
    return _ffn(h1, w_ffn_gate.astype(BF16), w_ffn_up.astype(BF16), w_ffn_down.astype(BF16),
                ln2_g[None, :], ln2_b[None, :], tm=1024, ff_chunk=256)


def kernel(x, positions, w_in, w_decay_up, b_decay, gla_norm_g, w_branch_a, w_branch_b, sinks,
           w_out, ln1_g, ln1_b, w_ffn_gate, w_ffn_up, w_ffn_down, ln2_g, ln2_b):
    batch, seq, d = x.shape
    m = batch * seq
    h2 = x.reshape(m, d)
    pos_row = positions.reshape(1, m).astype(F32)
    invf = _rope_inv_freq()
    for layer in range(w_in.shape[0]):
        h2 = _layer(h2, pos_row, invf, batch, seq, w_in[layer], w_decay_up[layer], b_decay[layer],
                    gla_norm_g[layer], w_branch_a[layer], w_branch_b[layer], sinks[layer],
                    w_out[layer], ln1_g[layer], ln1_b[layer], w_ffn_gate[layer],
                    w_ffn_up[layer], w_ffn_down[layer], ln2_g[layer], ln2_b[layer])
    return h2.reshape(batch, seq, d)
```

```python
import functools

import math

import jax
import jax.numpy as jnp
import numpy as np
from jax import lax
from jax.experimental import pallas as pl
from jax.experimental.pallas import tpu as pltpu

D_MODEL = 1024
GLA_HEADS = 4
GLA_DK = 128
GLA_DV = 256
GLA_RANK = 16
GLA_TAU = 16.0
GLA_CHUNK = 64
SWA_HEADS = 16
SWA_KV_HEADS = 2
SWA_HEAD_DIM = 64
SWA_WINDOW = 128
ROPE_DIM = 16
ROPE_THETA = 500000.0
DEPTH = 1
DEEPNORM_ALPHA = (2.0 * DEPTH) ** 0.25
LN_EPS = 1e-5
RMS_EPS = 1e-6

GLA_QK = GLA_HEADS * GLA_DK
GLA_V = GLA_HEADS * GLA_DV
SWA_Q = SWA_HEADS * SWA_HEAD_DIM
SWA_KV = SWA_KV_HEADS * SWA_HEAD_DIM

LANES = 128
MXU_COLS = 256
RANK_PAD = LANES
VMEM_LIMIT = 56 * 1024 * 1024
MERGE_SPLIT = 4
FFN_SPLIT = 4
LOG2_E = math.log2(math.e)
MASK_BIAS = -1e30

_C_GA = 0
_C_GB = _C_GA + D_MODEL
_C_QA = _C_GB + D_MODEL
_C_KA = _C_QA + GLA_QK
_C_VA = _C_KA + GLA_QK
_C_RA = _C_VA + GLA_V
_C_QB = _C_RA + GLA_V
_C_KV = _C_QB + SWA_Q
_C_DL = _C_KV + 2 * SWA_KV
_C_END = _C_DL + RANK_PAD
_O_DL = 2 * GLA_QK + 2 * GLA_V
_O_QB = _O_DL + GLA_RANK
_O_GA = _O_QB + SWA_Q + 2 * SWA_KV
_D_IN = _O_GA + 2 * D_MODEL

F32 = jnp.float32
BF16 = jnp.bfloat16


def _dot(a, b):
    return jnp.dot(a, b, preferred_element_type=F32)


def _dot_nt(a, b):
    return lax.dot_general(a, b, (((1,), (1,)), ((), ())), preferred_element_type=F32)


def _dot_tn(a, b):
    return lax.dot_general(a, b, (((0,), (0,)), ((), ())), preferred_element_type=F32)


def _resident(shape):
    return pl.BlockSpec(shape, lambda *_: (0,) * len(shape), pipeline_mode=pl.Buffered(1))


def _win_prep_kernel(wt_ref, o_ref):
    def put(col, row):
        o_ref[:, col:col + LANES] = wt_ref[row:row + LANES, :].T.astype(BF16)

    for j in range(2 * D_MODEL // LANES):
        put(_C_GA + j * LANES, _O_GA + j * LANES)
    for j in range(_O_DL // LANES):
        put(_C_QA + j * LANES, j * LANES)
    for j in range((_O_GA - _O_QB) // LANES):
        put(_C_QB + j * LANES, _O_QB + j * LANES)
    blk = wt_ref[_O_DL:_O_DL + LANES, :].T
    lane = lax.broadcasted_iota(jnp.int32, blk.shape, 1)
    o_ref[:, _C_DL:] = jnp.where(lane < GLA_RANK, blk, 0.0).astype(BF16)


def _win_prep(w_in_t):
    cb = LANES
    assert _C_DL == _D_IN - GLA_RANK and _O_QB % 8 == 0 and _O_GA % 8 == 0
    return pl.pallas_call(
        _win_prep_kernel,
        grid=(D_MODEL // cb,),
        in_specs=[pl.BlockSpec((_D_IN, cb), lambda i: (0, i))],
        out_specs=pl.BlockSpec((cb, _C_END), lambda i: (i, 0)),
        out_shape=jax.ShapeDtypeStruct((D_MODEL, _C_END), BF16),
        compiler_params=pltpu.CompilerParams(dimension_semantics=("arbitrary",)),
        name="win_prep",
    )(w_in_t)


def _inproj_kernel(x_ref, pos_ref, invf_ref, spread_ref, sf_ref, w_ref, wup_ref, bdec_ref,
                   qa_ref, ka_ref, va_ref, ra_ref, la_ref,
                   qh_ref, kh_ref, ve_ref, vo_ref):
    xb = x_ref[...].astype(BF16)

    def proj(lo, hi):
        return _dot(xb, w_ref[:, lo:hi])

    tm = x_ref.shape[0]
    half = ROPE_DIM // 2
    ang = jnp.tile(invf_ref[...], (1, tm // LANES)) * pos_ref[...]
    one_row = jnp.where(lax.broadcasted_iota(jnp.int32, (2 * half, tm), 0) == 0, 1.0, 0.0)
    cs = jnp.concatenate([jnp.cos(ang), jnp.sin(ang), one_row], axis=0)
    cs_hi = cs.astype(BF16)
    cs_lo = (cs - cs_hi.astype(F32)).astype(BF16)
    tabs = _dot_tn(jnp.concatenate([cs_hi, cs_lo], axis=0), spread_ref[...])
    cos, a_tab, b_tab = tabs[:, :LANES], tabs[:, LANES:2 * LANES], tabs[:, 2 * LANES:]

    def rope(t):
        return (t * cos + pltpu.roll(t, LANES - half, 1) * a_tab
                + pltpu.roll(t, half, 1) * b_tab)

    dh = SWA_HEAD_DIM
    lane_full = lax.broadcasted_iota(jnp.int32, (tm, LANES), 1)
    low = lane_full < dh
    q_scale = (dh ** -0.5) * LOG2_E

    def swa_queries(i):
        qb = proj(_C_QB + i * MXU_COLS, _C_QB + (i + 1) * MXU_COLS)
        for c in range(2 * i, 2 * i + 2):
            t = rope(qb[:, (c - 2 * i) * LANES:(c - 2 * i + 1) * LANES]) * q_scale
            qh_ref[2 * c] = jnp.where(low, t, sf_ref[2 * c:2 * c + 1, :]).astype(BF16)
            qh_ref[2 * c + 1] = jnp.where(
                low, pltpu.roll(t, dh, 1), sf_ref[2 * c + 1:2 * c + 2, :]).astype(BF16)

    def swa_keys_values_and_decay():
        kv = proj(_C_KV, _C_DL)
        kk = rope(kv[:, :SWA_KV])
        vv = kv[:, SWA_KV:]
        kk_sw = pltpu.roll(kk, dh, 1)
        vv_sw = pltpu.roll(vv, dh, 1)
        one_k = jnp.where(lane_full < dh + 2, 1.0, 0.0)
        one_e = jnp.where(lane_full == dh, 1.0, 0.0)
        one_o = jnp.where(lane_full == 0, 1.0, 0.0)
        kh_ref[0] = jnp.where(low, kk, one_k).astype(BF16)
        kh_ref[1] = jnp.where(low, kk_sw, one_k).astype(BF16)
        ve_ref[0] = jnp.where(low, vv, one_e).astype(BF16)
        ve_ref[1] = jnp.where(low, vv_sw, one_e).astype(BF16)
        vo_ref[0] = jnp.where(low, one_o, vv_sw).astype(BF16)
        vo_ref[1] = jnp.where(low, one_o, vv).astype(BF16)
        return proj(_C_DL, _C_END).astype(BF16)

    def log_decay(d_low, h):
        cols = slice(h * GLA_DK, (h + 1) * GLA_DK)
        z = _dot(d_low, wup_ref[:, cols]) + bdec_ref[:, cols]
        log_sig = jnp.minimum(z, 0.0) - jnp.log1p(jnp.exp(-jnp.abs(z)))
        la_ref[:, cols] = log_sig * (LOG2_E / GLA_TAU)

    def plain(ref, col0, i, scale=None):
        y = proj(col0 + i * MXU_COLS, col0 + (i + 1) * MXU_COLS)
        if scale is not None:
            y = y * scale
        ref[:, i * MXU_COLS:(i + 1) * MXU_COLS] = y.astype(BF16)

    plain_tiles = (
        [(qa_ref, _C_QA, i, GLA_DK ** -0.5) for i in range(GLA_QK // MXU_COLS)]
        + [(ka_ref, _C_KA, i, None) for i in range(GLA_QK // MXU_COLS)]
        + [(va_ref, _C_VA, i, None) for i in range(GLA_V // MXU_COLS)]
        + [(ra_ref, _C_RA, i, None) for i in range(GLA_V // MXU_COLS)])
    d_low = swa_keys_values_and_decay()
    heavy = []
    for i in range(max(SWA_Q // MXU_COLS, GLA_HEADS)):
        if i < SWA_Q // MXU_COLS:
            heavy.append(functools.partial(swa_queries, i))
        if i < GLA_HEADS:
            heavy.append(functools.partial(log_decay, d_low, i))
    per_heavy = len(plain_tiles) // len(heavy)
    for n, task in enumerate(heavy):
        task()
        first = n * per_heavy
        last = len(plain_tiles) if n == len(heavy) - 1 else first + per_heavy
        for args in plain_tiles[first:last]:
            plain(*args)


def _rope_spread():
    half = ROPE_DIM // 2
    rows = 4 * half
    e = np.zeros((rows, 3 * LANES), np.float32)
    for lane in range(LANES):
        j = lane % SWA_HEAD_DIM
        if j < half:
            e[j, lane] = 1.0
            e[half + j, LANES + lane] = -1.0
        elif j < 2 * half:
            e[j - half, lane] = 1.0
            e[j, 2 * LANES + lane] = 1.0
        else:
            e[2 * half, lane] = 1.0
    return jnp.asarray(np.concatenate([e, e], axis=0), dtype=BF16)


def _inproj(x2, pos_row, invf, sink_feat, w_in_r, w_up_p, b_dec, tm):
    m = x2.shape[0]
    half = ROPE_DIM // 2
    row = lambda c: pl.BlockSpec((tm, c), lambda i: (i, 0))
    row_out = lambda c, d: (row(c), jax.ShapeDtypeStruct((m, c), d))
    slab_out = lambda n: (pl.BlockSpec((n, tm, LANES), lambda i: (0, i, 0)),
                          jax.ShapeDtypeStruct((n, m, LANES), BF16))
    outs = [row_out(GLA_QK, BF16), row_out(GLA_QK, BF16), row_out(GLA_V, BF16),
            row_out(GLA_V, BF16), row_out(GLA_QK, F32),
            slab_out(SWA_HEADS), slab_out(SWA_KV_HEADS), slab_out(SWA_KV_HEADS),
            slab_out(SWA_KV_HEADS)]
    return pl.pallas_call(
        _inproj_kernel,
        grid=(m // tm,),
        in_specs=[row(D_MODEL), pl.BlockSpec((1, tm), lambda i: (0, i)),
                  _resident((half, LANES)), _resident((8 * half, 3 * LANES)),
                  _resident((SWA_HEADS, LANES)),
                  _resident((D_MODEL, _C_END)), _resident((RANK_PAD, GLA_QK)),
                  _resident((1, GLA_QK))],
        out_specs=[o[0] for o in outs],
        out_shape=[o[1] for o in outs],
        compiler_params=pltpu.CompilerParams(
            dimension_semantics=("arbitrary",), vmem_limit_bytes=VMEM_LIMIT),
        name="inproj",
    )(x2, pos_row, invf, _rope_spread(), sink_feat, w_in_r, w_up_p, b_dec)


def _gla_kernel(q_ref, k_ref, v_ref, r_ref, la_ref, g_ref, o_ref, st_ref, *, n_chunks):
    @pl.when(pl.program_id(1) == 0)
    def _():
        st_ref[...] = jnp.zeros_like(st_ref)

    c_ = GLA_CHUNK
    causal = (lax.broadcasted_iota(jnp.int32, (c_, c_), 0)
              >= lax.broadcasted_iota(jnp.int32, (c_, c_), 1))
    tri = jnp.where(causal, 1.0, 0.0).astype(BF16)
    gain = g_ref[...]

    items =[(c, h) for c in range(n_chunks) for h in range(GLA_HEADS)]
    rows = lambda c: slice(c * c_, (c + 1) * c_)
    ks = lambda h: slice(h * GLA_DK, (h + 1) * GLA_DK)
    vs = lambda h: slice(h * GLA_DV, (h + 1) * GLA_DV)

    b = []
    for c in range(n_chunks):
        la = la_ref[rows(c), :]
        hi = la.astype(BF16)
        rem = la - hi.astype(F32)
        mid = rem.astype(BF16)
        lo = (rem - mid.astype(F32)).astype(BF16)
        b.append(_dot(tri, hi) + _dot(tri, mid) + _dot(tri, lo))

    q_in, scores, contrib, decay = {}, {}, {}, {}
    for c, h in items:
        bh = b[c][:, ks(h)]
        b_last = bh[c_ - 1:c_, :]
        q = q_ref[rows(c), ks(h)].astype(F32)
        k = k_ref[rows(c), ks(h)].astype(F32)
        q_in[c, h] = (q * jnp.exp2(bh)).astype(BF16)
        k_in = (k * jnp.exp2(-bh)).astype(BF16)
        k_out = (k * jnp.exp2(b_last - bh)).astype(BF16)
        scores[c, h] = _dot_nt(q_in[c, h], k_in)
        contrib[c, h] = _dot_tn(k_out, v_ref[rows(c), vs(h)])
        dec = jnp.exp2(jnp.broadcast_to(b_last, (GLA_DK, GLA_DK)).T)
        decay[c, h] = jnp.concatenate([dec] * (GLA_DV // GLA_DK), axis=1)

    o_intra = {}
    for c, h in items:
        masked = jnp.where(causal, scores[c, h], 0.0).astype(BF16)
        o_intra[c, h] = _dot(masked, v_ref[rows(c), vs(h)])

    st = [st_ref[h] for h in range(GLA_HEADS)]
    for c, h in items:
        o = o_intra[c, h] + _dot(q_in[c, h], st[h].astype(BF16))
        st[h] = st[h] * decay[c, h] + contrib[c, h]
        o = o * lax.rsqrt(jnp.mean(o * o, axis=-1, keepdims=True) + RMS_EPS)
        r = r_ref[rows(c), vs(h)].astype(F32)
        o = o * gain * (r * jax.nn.sigmoid(r))
        o_ref[rows(c), vs(h)] = o.astype(BF16)
    for h in range(GLA_HEADS):
        st_ref[h] = st[h]


def _gla(qa, ka, va, ra, la, gain, batch, seq, tt):
    m = qa.shape[0]
    nt = seq // tt
    row = lambda c: pl.BlockSpec((tt, c), lambda b, t: (b * nt + t, 0))
    return pl.pallas_call(
        functools.partial(_gla_kernel, n_chunks=tt // GLA_CHUNK),
        grid=(batch, nt),
        in_specs=[row(GLA_QK), row(GLA_QK), row(GLA_V), row(GLA_V), row(GLA_QK),
                  pl.BlockSpec((1, GLA_DV), lambda b, t: (0, 0))],
        out_specs=row(GLA_V),
        out_shape=jax.ShapeDtypeStruct((m, GLA_V), BF16),
        scratch_shapes=[pltpu.VMEM((GLA_HEADS, GLA_DK, GLA_DV), F32)],
        compiler_params=pltpu.CompilerParams(
            dimension_semantics=("arbitrary", "arbitrary"), vmem_limit_bytes=VMEM_LIMIT),
        name="gla",
    )(qa, ka, va, ra, la, gain)


def _swa_kernel(q_ref, kc_ref, kp_ref, vec_ref, vep_ref, voc_ref, vop_ref, mask_ref, o_ref,
                *, blocks_per_step):
    w = SWA_WINDOW
    dh = SWA_HEAD_DIM
    per_group = SWA_HEADS // SWA_KV_HEADS
    first = pl.program_id(1) == 0
    row = lax.broadcasted_iota(jnp.int32, (w, LANES), 0)
    lane = lax.broadcasted_iota(jnp.int32, (w, LANES), 1)
    ident = jnp.where(row == lane, 1.0, 0.0).astype(BF16)
    sink_row = row == 0
    low = lane < dh
    zero = jnp.zeros((w, LANES), BF16)
    one_e = jnp.where(lane == dh, 1.0, 0.0).astype(BF16)
    one_o = jnp.where(lane == 0, 1.0, 0.0).astype(BF16)
    order = list(range(0, per_group, 2)) + list(range(1, per_group, 2))

    scores = []
    values = []
    for qb in range(blocks_per_step):
        rows = slice(qb * w, (qb + 1) * w)
        prev_rows = slice((qb - 1) * w, qb * w)
        mask = mask_ref[1] if qb else jnp.where(first, mask_ref[0], mask_ref[1])
        for g in range(SWA_KV_HEADS):
            k_prev = kc_ref[g, prev_rows] if qb else kp_ref[g]
            ve_prev = vec_ref[g, prev_rows] if qb else vep_ref[g]
            vo_prev = voc_ref[g, prev_rows] if qb else vop_ref[g]
            k2 = jnp.concatenate([jnp.where(sink_row, zero, k_prev), kc_ref[g, rows]], axis=0)
            kx = jnp.concatenate([k2, mask], axis=1)
            ve = jnp.concatenate([jnp.where(sink_row, one_e, ve_prev), vec_ref[g, rows]], axis=0)
            vo = jnp.concatenate([jnp.where(sink_row, one_o, vo_prev), voc_ref[g, rows]], axis=0)
            for part, vv in ((order[:per_group // 2], ve), (order[per_group // 2:], vo)):
                qs = jnp.concatenate(
                    [jnp.concatenate([q_ref[g * per_group + h, rows], ident], axis=1)
                     for h in part], axis=0)
                scores.append(_dot_nt(qs, kx))
                values.append(vv)
    pv = []
    for s, vv in zip(scores, values):
        p = jnp.exp2(s - jnp.max(s, axis=-1, keepdims=True)).astype(BF16)
        pv.append(_dot(p, vv))
    for qb in range(blocks_per_step):
        rows = slice(qb * w, (qb + 1) * w)
        for g in range(SWA_KV_HEADS):
            pv_e = pv[(qb * SWA_KV_HEADS + g) * 2]
            pv_o = pv[(qb * SWA_KV_HEADS + g) * 2 + 1]
            for c in range(per_group // 2):
                e = pv_e[c * w:(c + 1) * w]
                o = pv_o[c * w:(c + 1) * w]
                out = jnp.where(low, e * (1.0 / e[:, dh:dh + 1]), o * (1.0 / o[:, 0:1]))
                col = (g * (per_group // 2) + c) * LANES
                o_ref[rows, col:col + LANES] = out.astype(BF16)


def _swa_mask():
    w = SWA_WINDOW
    j = np.arange(2 * w)[:, None]
    i = np.arange(w)[None, :]
    band = (j > i) & (j <= i + w)
    later = band | (j == 0)
    first = (band & (j >= w)) | (j == 0)
    both = np.stack([first, later])
    return jnp.asarray(np.where(both, 0.0, MASK_BIAS), dtype=BF16)


def _swa(qh, kh, ve, vo, batch, seq, blocks_per_step):
    m = kh.shape[1]
    w = SWA_WINDOW
    tq = blocks_per_step * w
    steps = seq // tq
    kv = SWA_KV_HEADS
    cur = lambda n_: pl.BlockSpec((n_, tq, LANES), lambda b, t: (0, b * steps + t, 0))
    prev = lambda n_: pl.BlockSpec(
        (n_, w, LANES),
        lambda b, t: (0, b * (seq // w) + jnp.maximum(t * blocks_per_step - 1, 0), 0))
    return pl.pallas_call(
        functools.partial(_swa_kernel, blocks_per_step=blocks_per_step),
        grid=(batch, steps),
        in_specs=[cur(SWA_HEADS), cur(kv), prev(kv), cur(kv), prev(kv), cur(kv), prev(kv),
                  _resident((2, 2 * w, w))],
        out_specs=pl.BlockSpec((tq, SWA_Q), lambda b, t: (b * steps + t, 0)),
        out_shape=jax.ShapeDtypeStruct((m, SWA_Q), BF16),
        compiler_params=pltpu.CompilerParams(
            dimension_semantics=("arbitrary", "arbitrary"), vmem_limit_bytes=VMEM_LIMIT),
        name="swa",
    )(qh, kh, kh, ve, ve, vo, vo, _swa_mask())


def _layer_norm(y, g, b):
    mu = jnp.mean(y, axis=-1, keepdims=True)
    d = y - mu
    var = jnp.mean(d * d, axis=-1, keepdims=True)
    return d * lax.rsqrt(var + LN_EPS) * g + b


def _merge_kernel(x_ref, oa_ref, ob_ref, wg_ref, wa_ref, wb_ref, wo_ref, g_ref, b_ref, h_ref):
    tm = x_ref.shape[0]
    parts = [slice(i * (tm // MERGE_SPLIT), (i + 1) * (tm // MERGE_SPLIT))
             for i in range(MERGE_SPLIT)]
    pre = []
    for rs in parts:
        xb = x_ref[rs, :].astype(BF16)
        pre.append((_dot(oa_ref[rs, :], wa_ref[...]), _dot(ob_ref[rs, :], wb_ref[...]),
                    _dot(xb, wg_ref[:, :D_MODEL]), _dot(xb, wg_ref[:, D_MODEL:])))
    for rs, (y_a, y_b, g_a, g_b) in zip(parts, pre):
        merged = jax.nn.sigmoid(g_a) * y_a + jax.nn.sigmoid(g_b) * y_b
        mix = _dot(merged.astype(BF16), wo_ref[...])
        h_ref[rs, :] = _layer_norm(DEEPNORM_ALPHA * x_ref[rs, :] + mix, g_ref[...], b_ref[...])


def _merge(x2, oa, ob, w_in_r, wa, wb, wo, ln_g, ln_b, tm):
    m = x2.shape[0]
    row = lambda c: pl.BlockSpec((tm, c), lambda i: (i, 0))
    sq = _resident((D_MODEL, D_MODEL))
    vec = _resident((1, D_MODEL))
    assert _C_GA == 0 and _C_GB == D_MODEL
    gates = pl.BlockSpec((D_MODEL, 2 * D_MODEL), lambda i: (0, 0), pipeline_mode=pl.Buffered(1))
    return pl.pallas_call(
        _merge_kernel,
        grid=(m // tm,),
        in_specs=[row(D_MODEL)] * 3 + [gates, sq, sq, sq, vec, vec],
        out_specs=row(D_MODEL),
        out_shape=jax.ShapeDtypeStruct((m, D_MODEL), F32),
        compiler_params=pltpu.CompilerParams(
            dimension_semantics=("arbitrary",), vmem_limit_bytes=VMEM_LIMIT),
        name="merge",
    )(x2, oa, ob, w_in_r, wa, wb, wo, ln_g, ln_b)


def _ffn_kernel(h_ref, wg_ref, wu_ref, wd_ref, g_ref, b_ref, o_ref, *, ff_chunk):
    d_ff = wg_ref.shape[1]
    n_chunks = d_ff // ff_chunk
    sl = lambda j: slice(j * ff_chunk, (j + 1) * ff_chunk)
    tm = h_ref.shape[0]
    for i in range(FFN_SPLIT):
        rs = slice(i * (tm // FFN_SPLIT), (i + 1) * (tm // FFN_SPLIT))
        h = h_ref[rs, :]
        hb = h.astype(BF16)

        def gate_up(j):
            return _dot(hb, wg_ref[:, sl(j)]), _dot(hb, wu_ref[:, sl(j)])

        acc = None
        gu = gate_up(0)
        for j in range(n_chunks):
            gate, up = gu
            if j + 1 < n_chunks:
                gu = gate_up(j + 1)
            act = (gate * jax.nn.sigmoid(gate) * up).astype(BF16)
            part = _dot(act, wd_ref[sl(j), :])
            acc = part if acc is None else acc + part
        o_ref[rs, :] = _layer_norm(DEEPNORM_ALPHA * h + acc, g_ref[...], b_ref[...])


def _ffn(h1, wg, wu, wd, ln_g, ln_b, tm, ff_chunk):
    m = h1.shape[0]
    d_ff = wg.shape[1]
    row = pl.BlockSpec((tm, D_MODEL), lambda i: (i, 0))
    vec = _resident((1, D_MODEL))
    return pl.pallas_call(
        functools.partial(_ffn_kernel, ff_chunk=ff_chunk),
        grid=(m // tm,),
        in_specs=[row, _resident((D_MODEL, d_ff)), _resident((D_MODEL, d_ff)),
                  _resident((d_ff, D_MODEL)), vec, vec],
        out_specs=row,
        out_shape=jax.ShapeDtypeStruct((m, D_MODEL), F32),
        compiler_params=pltpu.CompilerParams(
            dimension_semantics=("arbitrary",), vmem_limit_bytes=VMEM_LIMIT),
        name="ffn",
    )(h1, wg, wu, wd, ln_g, ln_b)


def _rope_inv_freq():
    inv_freq = ROPE_THETA ** (-jnp.arange(0, ROPE_DIM, 2, dtype=F32) / ROPE_DIM)
    return jnp.broadcast_to(inv_freq[:, None], (ROPE_DIM // 2, LANES))


def _layer(h2, pos_row, invf, batch, seq, w_in, w_decay_up, b_decay, gla_norm_g,
           w_branch_a, w_branch_b, sinks, w_out, ln1_g, ln1_b,
           w_ffn_gate, w_ffn_up, w_ffn_down, ln2_g, ln2_b):
    w_in_r = _win_prep(w_in.T)
    w_up_p = jnp.concatenate(
        [w_decay_up, jnp.zeros((RANK_PAD - GLA_RANK, GLA_QK), w_decay_up.dtype)],
        axis=0).astype(BF16)

    neg_sink = -sinks.astype(F32) * LOG2_E
    sink_hi = neg_sink.astype(BF16).astype(F32)
    sink_feat = jnp.zeros((SWA_HEADS, LANES), F32)
    sink_feat = sink_feat.at[:, SWA_HEAD_DIM].set(sink_hi)
    sink_feat = sink_feat.at[:, SWA_HEAD_DIM + 1].set(neg_sink - sink_hi)

    qa, ka, va, ra, la, qh, kh, ve, vo = _inproj(
        h2, pos_row, invf, sink_feat, w_in_r, w_up_p, b_decay[None, :], tm=512)
    oa = _gla(qa, ka, va, ra, la, gla_norm_g[None, :], batch, seq, tt=512)
    ob = _swa(qh, kh, ve, vo, batch, seq, blocks_per_step=4)
    h1 = _merge(h2, oa, ob, w_in_r, w_branch_a.astype(BF16), w_branch_b.astype(BF16),
                w_out.astype(BF16), ln1_g[None, :], ln1_b[None, :], tm=1024)
    return _ffn(h1, w_ffn_gate.astype(BF16), w_ffn_up.astype(BF16), w_ffn_down.astype(BF16),
                ln2_g[None, :], ln2_b[None, :], tm=1024, ff_chunk=256)


def kernel(x, positions, w_in, w_decay_up, b_decay, gla_norm_g, w_branch_a, w_branch_b, sinks,
           w_out, ln1_g, ln1_b, w_ffn_gate, w_ffn_up, w_ffn_down, ln2_g, ln2_b):
    batch, seq, d = x.shape
    m = batch * seq
    h2 = x.reshape(m, d)
    pos_row = positions.reshape(1, m).astype(F32)
    invf = _rope_inv_freq()
    for layer in range(w_in.shape[0]):
        h2 = _layer(h2, pos_row, invf, batch, seq, w_in[layer], w_decay_up[layer], b_decay[layer],
                    gla_norm_g[layer], w_branch_a[layer], w_branch_b[layer], sinks[layer],
                    w_out[layer], ln1_g[layer], ln1_b[layer], w_ffn_gate[layer],
                    w_ffn_up[layer], w_ffn_down[layer], ln2_g[layer], ln2_b[layer])
    return h2.reshape(batch, seq, d)
```

```python
import functools

import math

import jax
import jax.numpy as jnp
import numpy as np
from jax import lax
from jax.experimental import pallas as pl
from jax.experimental.pallas import tpu as pltpu

D_MODEL = 1024
GLA_HEADS = 4
GLA_DK = 128
GLA_DV = 256
GLA_RANK = 16
GLA_TAU = 16.0
GLA_CHUNK = 64
GLA_BLOCK = GLA_CHUNK
SWA_HEADS = 16
SWA_KV_HEADS = 2
SWA_HEAD_DIM = 64
SWA_WINDOW = 128
ROPE_DIM = 16
ROPE_THETA = 500000.0
DEPTH = 1
DEEPNORM_ALPHA = (2.0 * DEPTH) ** 0.25
LN_EPS = 1e-5
RMS_EPS = 1e-6

GLA_QK = GLA_HEADS * GLA_DK
GLA_V = GLA_HEADS * GLA_DV
SWA_Q = SWA_HEADS * SWA_HEAD_DIM
SWA_KV = SWA_KV_HEADS * SWA_HEAD_DIM

LANES = 128
MXU_COLS = 256
RANK_PAD = LANES
VMEM_LIMIT = 56 * 1024 * 1024
TAIL_SPLIT = 2
MERGE_SPLIT = 4
FFN_SPLIT = 4
LOG2_E = math.log2(math.e)
MASK_BIAS = -1e30

_C_QA = 0
_C_KA = _C_QA + GLA_QK
_C_VA = _C_KA + GLA_QK
_C_RA = _C_VA + GLA_V
_C_QB = _C_RA + GLA_V
_C_KV = _C_QB + SWA_Q
_C_GA = _C_KV + 2 * SWA_KV
_C_GB = _C_GA + D_MODEL
_C_DL = _C_GB + D_MODEL
_C_END = _C_DL + RANK_PAD
_D_IN = 2 * GLA_QK + 2 * GLA_V + GLA_RANK + SWA_Q + 2 * SWA_KV + 2 * D_MODEL
_O_DL = 2 * GLA_QK + 2 * GLA_V

F32 = jnp.float32
BF16 = jnp.bfloat16


def _dot(a, b):
    return jnp.dot(a, b, preferred_element_type=F32)


def _dot_nt(a, b):
    return lax.dot_general(a, b, (((1,), (1,)), ((), ())), preferred_element_type=F32)


def _dot_tn(a, b):
    return lax.dot_general(a, b, (((0,), (0,)), ((), ())), preferred_element_type=F32)


def _resident(shape):
    return pl.BlockSpec(shape, lambda *_: (0,) * len(shape), pipeline_mode=pl.Buffered(1))


def _win_prep_kernel(wt_ref, o_ref):
    def put(col, row):
        o_ref[:, col:col + LANES] = wt_ref[row:row + LANES, :].T.astype(BF16)

    for j in range(_O_DL // LANES):
        put(j * LANES, j * LANES)
    for j in range((_C_DL - _O_DL) // LANES):
        put(_O_DL + j * LANES, _O_DL + GLA_RANK + j * LANES)
    blk = wt_ref[_O_DL:_O_DL + LANES, :].T
    lane = lax.broadcasted_iota(jnp.int32, blk.shape, 1)
    o_ref[:, _C_DL:] = jnp.where(lane < GLA_RANK, blk, 0.0).astype(BF16)


def _win_prep(w_in_t):
    cb = LANES
    assert _C_DL - _O_DL == _D_IN - _O_DL - GLA_RANK and (_C_DL - _O_DL) % LANES == 0
    return pl.pallas_call(
        _win_prep_kernel,
        grid=(D_MODEL // cb,),
        in_specs=[pl.BlockSpec((_D_IN, cb), lambda i: (0, i))],
        out_specs=pl.BlockSpec((cb, _C_END), lambda i: (i, 0)),
        out_shape=jax.ShapeDtypeStruct((D_MODEL, _C_END), BF16),
        compiler_params=pltpu.CompilerParams(dimension_semantics=("arbitrary",)),
        name="win_prep",
    )(w_in_t)


def _inproj_kernel(x_ref, pos_ref, invf_ref, spread_ref, sf_ref, w_ref, wup_ref, bdec_ref,
                   qa_ref, ka_ref, va_ref, ra_ref, la_ref,
                   qh_ref, kh_ref, ve_ref, vo_ref, ga_ref, gb_ref):
    xb = x_ref[...].astype(BF16)

    def proj(lo, hi):
        return _dot(xb, w_ref[:, lo:hi])

    tm = x_ref.shape[0]
    half = ROPE_DIM // 2
    ang = jnp.tile(invf_ref[...], (1, tm // LANES)) * pos_ref[...]
    one_row = jnp.where(lax.broadcasted_iota(jnp.int32, (2 * half, tm), 0) == 0, 1.0, 0.0)
    cs = jnp.concatenate([jnp.cos(ang), jnp.sin(ang), one_row], axis=0)
    cs_hi = cs.astype(BF16)
    cs_lo = (cs - cs_hi.astype(F32)).astype(BF16)
    tabs = _dot_tn(jnp.concatenate([cs_hi, cs_lo], axis=0), spread_ref[...])
    cos, a_tab, b_tab = tabs[:, :LANES], tabs[:, LANES:2 * LANES], tabs[:, 2 * LANES:]

    def rope(t):
        return (t * cos + pltpu.roll(t, LANES - half, 1) * a_tab
                + pltpu.roll(t, half, 1) * b_tab)

    dh = SWA_HEAD_DIM
    lane_full = lax.broadcasted_iota(jnp.int32, (tm, LANES), 1)
    low = lane_full < dh
    q_scale = (dh ** -0.5) * LOG2_E

    def swa_queries(i):
        qb = proj(_C_QB + i * MXU_COLS, _C_QB + (i + 1) * MXU_COLS)
        for c in range(2 * i, 2 * i + 2):
            t = rope(qb[:, (c - 2 * i) * LANES:(c - 2 * i + 1) * LANES]) * q_scale
            qh_ref[2 * c] = jnp.where(low, t, sf_ref[2 * c:2 * c + 1, :]).astype(BF16)
            qh_ref[2 * c + 1] = jnp.where(
                low, pltpu.roll(t, dh, 1), sf_ref[2 * c + 1:2 * c + 2, :]).astype(BF16)

    def swa_keys_values_and_decay():
        kv = proj(_C_KV, _C_GA)
        kk = rope(kv[:, :SWA_KV])
        vv = kv[:, SWA_KV:]
        kk_sw = pltpu.roll(kk, dh, 1)
        vv_sw = pltpu.roll(vv, dh, 1)
        one_k = jnp.where(lane_full < dh + 2, 1.0, 0.0)
        one_e = jnp.where(lane_full == dh, 1.0, 0.0)
        one_o = jnp.where(lane_full == 0, 1.0, 0.0)
        kh_ref[0] = jnp.where(low, kk, one_k).astype(BF16)
        kh_ref[1] = jnp.where(low, kk_sw, one_k).astype(BF16)
        ve_ref[0] = jnp.where(low, vv, one_e).astype(BF16)
        ve_ref[1] = jnp.where(low, vv_sw, one_e).astype(BF16)
        vo_ref[0] = jnp.where(low, one_o, vv_sw).astype(BF16)
        vo_ref[1] = jnp.where(low, one_o, vv).astype(BF16)
        return proj(_C_DL, _C_END).astype(BF16)

    def log_decay(d_low, h):
        cols = slice(h * GLA_DK, (h + 1) * GLA_DK)
        z = _dot(d_low, wup_ref[:, cols]) + bdec_ref[:, cols]
        log_sig = jnp.minimum(z, 0.0) - jnp.log1p(jnp.exp(-jnp.abs(z)))
        la_ref[:, cols] = log_sig * (LOG2_E / GLA_TAU)

    def plain(ref, col0, i, scale=None):
        y = proj(col0 + i * MXU_COLS, col0 + (i + 1) * MXU_COLS)
        if scale is not None:
            y = y * scale
        ref[:, i * MXU_COLS:(i + 1) * MXU_COLS] = y.astype(BF16)

    plain_tiles = (
        [(qa_ref, _C_QA, i, GLA_DK ** -0.5) for i in range(GLA_QK // MXU_COLS)]
        + [(ka_ref, _C_KA, i, None) for i in range(GLA_QK // MXU_COLS)]
        + [(va_ref, _C_VA, i, None) for i in range(GLA_V // MXU_COLS)]
        + [(ra_ref, _C_RA, i, None) for i in range(GLA_V // MXU_COLS)]
        + [(ga_ref, _C_GA, i, None) for i in range(D_MODEL // MXU_COLS)]
        + [(gb_ref, _C_GB, i, None) for i in range(D_MODEL // MXU_COLS)])
    d_low = swa_keys_values_and_decay()
    heavy = []
    for i in range(max(SWA_Q // MXU_COLS, GLA_HEADS)):
        if i < SWA_Q // MXU_COLS:
            heavy.append(functools.partial(swa_queries, i))
        if i < GLA_HEADS:
            heavy.append(functools.partial(log_decay, d_low, i))
    per_heavy = len(plain_tiles) // len(heavy)
    for n, task in enumerate(heavy):
        task()
        first = n * per_heavy
        last = len(plain_tiles) if n == len(heavy) - 1 else first + per_heavy
        for args in plain_tiles[first:last]:
            plain(*args)


def _rope_spread():
    half = ROPE_DIM // 2
    rows = 4 * half
    e = np.zeros((rows, 3 * LANES), np.float32)
    for lane in range(LANES):
        j = lane % SWA_HEAD_DIM
        if j < half:
            e[j, lane] = 1.0
            e[half + j, LANES + lane] = -1.0
        elif j < 2 * half:
            e[j - half, lane] = 1.0
            e[j, 2 * LANES + lane] = 1.0
        else:
            e[2 * half, lane] = 1.0
    return jnp.asarray(np.concatenate([e, e], axis=0), dtype=BF16)


def _inproj(x2, pos_row, invf, sink_feat, w_in_r, w_up_p, b_dec, tm):
    m = x2.shape[0]
    half = ROPE_DIM // 2
    row = lambda c: pl.BlockSpec((tm, c), lambda i: (i, 0))
    row_out = lambda c, d: (row(c), jax.ShapeDtypeStruct((m, c), d))
    slab_out = lambda n: (pl.BlockSpec((n, tm, LANES), lambda i: (0, i, 0)),
                          jax.ShapeDtypeStruct((n, m, LANES), BF16))
    outs = [row_out(GLA_QK, BF16), row_out(GLA_QK, BF16), row_out(GLA_V, BF16),
            row_out(GLA_V, BF16), row_out(GLA_QK, F32),
            slab_out(SWA_HEADS), slab_out(SWA_KV_HEADS), slab_out(SWA_KV_HEADS),
            slab_out(SWA_KV_HEADS), row_out(D_MODEL, BF16), row_out(D_MODEL, BF16)]
    return pl.pallas_call(
        _inproj_kernel,
        grid=(m // tm,),
        in_specs=[row(D_MODEL), pl.BlockSpec((1, tm), lambda i: (0, i)),
                  _resident((half, LANES)), _resident((8 * half, 3 * LANES)),
                  _resident((SWA_HEADS, LANES)),
                  _resident((D_MODEL, _C_END)), _resident((RANK_PAD, GLA_QK)),
                  _resident((1, GLA_QK))],
        out_specs=[o[0] for o in outs],
        out_shape=[o[1] for o in outs],
        compiler_params=pltpu.CompilerParams(
            dimension_semantics=("arbitrary",), vmem_limit_bytes=VMEM_LIMIT),
        name="inproj",
    )(x2, pos_row, invf, _rope_spread(), sink_feat, w_in_r, w_up_p, b_dec)


def _gla_kernel(q_ref, k_ref, v_ref, r_ref, la_ref, g_ref, o_ref, st_ref, *, n_blocks):
    @pl.when(pl.program_id(1) == 0)
    def _():
        st_ref[...] = jnp.zeros_like(st_ref)

    c_ = GLA_BLOCK
    causal = (lax.broadcasted_iota(jnp.int32, (c_, c_), 0)
              >= lax.broadcasted_iota(jnp.int32, (c_, c_), 1))
    tri = jnp.where(causal, 1.0, 0.0).astype(BF16)
    gain = g_ref[...]

    items = [(c, h) for c in range(n_blocks) for h in range(GLA_HEADS)]
    rows = lambda c: slice(c * c_, (c + 1) * c_)
    ks = lambda h: slice(h * GLA_DK, (h + 1) * GLA_DK)
    vs = lambda h: slice(h * GLA_DV, (h + 1) * GLA_DV)

    b = []
    for c in range(n_blocks):
        la = la_ref[rows(c), :]
        hi = la.astype(BF16)
        rem = la - hi.astype(F32)
        mid = rem.astype(BF16)
        lo = (rem - mid.astype(F32)).astype(BF16)
        b.append(_dot(tri, hi) + _dot(tri, mid) + _dot(tri, lo))

    q_in, scores, contrib, decay, o_intra = {}, {}, {}, {}, {}

    def stage_a(c):
        for h in range(GLA_HEADS):
            bh = b[c][:, ks(h)]
            b_last = bh[c_ - 1:c_, :]
            q = q_ref[rows(c), ks(h)].astype(F32)
            k = k_ref[rows(c), ks(h)].astype(F32)
            q_in[c, h] = (q * jnp.exp2(bh)).astype(BF16)
            k_in = (k * jnp.exp2(-bh)).astype(BF16)
            k_out = (k * jnp.exp2(b_last - bh)).astype(BF16)
            scores[c, h] = _dot_nt(q_in[c, h], k_in)
            contrib[c, h] = _dot_tn(k_out, v_ref[rows(c), vs(h)])
            dec = jnp.exp2(jnp.broadcast_to(b_last, (GLA_DK, GLA_DK)).T)
            decay[c, h] = jnp.concatenate([dec] * (GLA_DV // GLA_DK), axis=1)

    def stage_b(c):
        for h in range(GLA_HEADS):
            masked = jnp.where(causal, scores[c, h], 0.0).astype(BF16)
            o_intra[c, h] = _dot(masked, v_ref[rows(c), vs(h)])

    st = [st_ref[h] for h in range(GLA_HEADS)]

    def stage_c(c):
        for h in range(GLA_HEADS):
            o = o_intra[c, h] + _dot(q_in[c, h], st[h].astype(BF16))
            st[h] = st[h] * decay[c, h] + contrib[c, h]
            o = o * lax.rsqrt(jnp.mean(o * o, axis=-1, keepdims=True) + RMS_EPS)
            r = r_ref[rows(c), vs(h)].astype(F32)
            o = o * gain * (r * jax.nn.sigmoid(r))
            o_ref[rows(c), vs(h)] = o.astype(BF16)

    for stage in (stage_a, stage_b, stage_c):
        for c in range(n_blocks):
            stage(c)
    for h in range(GLA_HEADS):
        st_ref[h] = st[h]


def _gla(qa, ka, va, ra, la, gain, batch, seq, tt):
    m = qa.shape[0]
    nt = seq // tt
    row = lambda c: pl.BlockSpec((tt, c), lambda b, t: (b * nt + t, 0))
    return pl.pallas_call(
        functools.partial(_gla_kernel, n_blocks=tt // GLA_BLOCK),
        grid=(batch, nt),
        in_specs=[row(GLA_QK), row(GLA_QK), row(GLA_V), row(GLA_V), row(GLA_QK),
                  pl.BlockSpec((1, GLA_DV), lambda b, t: (0, 0))],
        out_specs=row(GLA_V),
        out_shape=jax.ShapeDtypeStruct((m, GLA_V), BF16),
        scratch_shapes=[pltpu.VMEM((GLA_HEADS, GLA_DK, GLA_DV), F32)],
        compiler_params=pltpu.CompilerParams(
            dimension_semantics=("arbitrary", "arbitrary"), vmem_limit_bytes=VMEM_LIMIT),
        name="gla",
    )(qa, ka, va, ra, la, gain)


def _swa_kernel(q_ref, kc_ref, kp_ref, vec_ref, vep_ref, voc_ref, vop_ref, mask_ref, o_ref,
                *, blocks_per_step):
    w = SWA_WINDOW
    dh = SWA_HEAD_DIM
    per_group = SWA_HEADS // SWA_KV_HEADS
    first = pl.program_id(1) == 0
    row = lax.broadcasted_iota(jnp.int32, (w, LANES), 0)
    lane = lax.broadcasted_iota(jnp.int32, (w, LANES), 1)
    ident = jnp.where(row == lane, 1.0, 0.0).astype(BF16)
    sink_row = row == 0
    low = lane < dh
    zero = jnp.zeros((w, LANES), BF16)
    one_e = jnp.where(lane == dh, 1.0, 0.0).astype(BF16)
    one_o = jnp.where(lane == 0, 1.0, 0.0).astype(BF16)
    order = list(range(0, per_group, 2)) + list(range(1, per_group, 2))

    scores = []
    values = []
    for qb in range(blocks_per_step):
        rows = slice(qb * w, (qb + 1) * w)
        prev_rows = slice((qb - 1) * w, qb * w)
        mask = mask_ref[1] if qb else jnp.where(first, mask_ref[0], mask_ref[1])
        for g in range(SWA_KV_HEADS):
            k_prev = kc_ref[g, prev_rows] if qb else kp_ref[g]
            ve_prev = vec_ref[g, prev_rows] if qb else vep_ref[g]
            vo_prev = voc_ref[g, prev_rows] if qb else vop_ref[g]
            k2 = jnp.concatenate([jnp.where(sink_row, zero, k_prev), kc_ref[g, rows]], axis=0)
            kx = jnp.concatenate([k2, mask], axis=1)
            ve = jnp.concatenate([jnp.where(sink_row, one_e, ve_prev), vec_ref[g, rows]], axis=0)
            vo = jnp.concatenate([jnp.where(sink_row, one_o, vo_prev), voc_ref[g, rows]], axis=0)
            for part, vv in ((order[:per_group // 2], ve), (order[per_group // 2:], vo)):
                qs = jnp.concatenate(
                    [jnp.concatenate([q_ref[g * per_group + h, rows], ident], axis=1)
                     for h in part], axis=0)
                scores.append(_dot_nt(qs, kx))
                values.append(vv)
    pv = []
    for s, vv in zip(scores, values):
        p = jnp.exp2(s - jnp.max(s, axis=-1, keepdims=True)).astype(BF16)
        pv.append(_dot(p, vv))
    for qb in range(blocks_per_step):
        rows = slice(qb * w, (qb + 1) * w)
        for g in range(SWA_KV_HEADS):
            pv_e = pv[(qb * SWA_KV_HEADS + g) * 2]
            pv_o = pv[(qb * SWA_KV_HEADS + g) * 2 + 1]
            for c in range(per_group // 2):
                e = pv_e[c * w:(c + 1) * w]
                o = pv_o[c * w:(c + 1) * w]
                out = jnp.where(low, e * (1.0 / e[:, dh:dh + 1]), o * (1.0 / o[:, 0:1]))
                col = (g * (per_group // 2) + c) * LANES
                o_ref[rows, col:col + LANES] = out.astype(BF16)


def _swa_mask():
    w = SWA_WINDOW
    j = np.arange(2 * w)[:, None]
    i = np.arange(w)[None, :]
    band = (j > i) & (j <= i + w)
    later = band | (j == 0)
    first = (band & (j >= w)) | (j == 0)
    both = np.stack([first, later])
    return jnp.asarray(np.where(both, 0.0, MASK_BIAS), dtype=BF16)


def _swa(qh, kh, ve, vo, batch, seq, blocks_per_step):
    m = kh.shape[1]
    w = SWA_WINDOW
    tq = blocks_per_step * w
    steps = seq // tq
    kv = SWA_KV_HEADS
    cur = lambda n_: pl.BlockSpec((n_, tq, LANES), lambda b, t: (0, b * steps + t, 0))
    prev = lambda n_: pl.BlockSpec(
        (n_, w, LANES),
        lambda b, t: (0, b * (seq // w) + jnp.maximum(t * blocks_per_step - 1, 0), 0))
    return pl.pallas_call(
        functools.partial(_swa_kernel, blocks_per_step=blocks_per_step),
        grid=(batch, steps),
        in_specs=[cur(SWA_HEADS), cur(kv), prev(kv), cur(kv), prev(kv), cur(kv), prev(kv),
                  _resident((2, 2 * w, w))],
        out_specs=pl.BlockSpec((tq, SWA_Q), lambda b, t: (b * steps + t, 0)),
        out_shape=jax.ShapeDtypeStruct((m, SWA_Q), BF16),
        compiler_params=pltpu.CompilerParams(
            dimension_semantics=("arbitrary", "arbitrary"), vmem_limit_bytes=VMEM_LIMIT),
        name="swa",
    )(qh, kh, kh, ve, ve, vo, vo, _swa_mask())


def _layer_norm(y, g, b):
    mu = jnp.mean(y, axis=-1, keepdims=True)
    d = y - mu
    var = jnp.mean(d * d, axis=-1, keepdims=True)
    return d * lax.rsqrt(var + LN_EPS) * g + b


def _merge_kernel(x_ref, oa_ref, ob_ref, ga_ref, gb_ref, wa_ref, wb_ref, wo_ref,
                  g_ref, b_ref, h_ref):
    tm = x_ref.shape[0]
    parts = [slice(i * (tm // MERGE_SPLIT), (i + 1) * (tm // MERGE_SPLIT))
             for i in range(MERGE_SPLIT)]
    branches = [(_dot(oa_ref[rs, :], wa_ref[...]), _dot(ob_ref[rs, :], wb_ref[...]))
                for rs in parts]
    for rs, (y_a, y_b) in zip(parts, branches):
        merged = (jax.nn.sigmoid(ga_ref[rs, :].astype(F32)) * y_a
                  + jax.nn.sigmoid(gb_ref[rs, :].astype(F32)) * y_b)
        mix = _dot(merged.astype(BF16), wo_ref[...])
        h_ref[rs, :] = _layer_norm(DEEPNORM_ALPHA * x_ref[rs, :] + mix, g_ref[...], b_ref[...])


def _merge(x2, oa, ob, ga, gb, wa, wb, wo, ln_g, ln_b, tm):
    m = x2.shape[0]
    row = lambda c: pl.BlockSpec((tm, c), lambda i: (i, 0))
    sq = _resident((D_MODEL, D_MODEL))
    vec = _resident((1, D_MODEL))
    return pl.pallas_call(
        _merge_kernel,
        grid=(m // tm,),
        in_specs=[row(D_MODEL)] * 5 + [sq, sq, sq, vec, vec],
        out_specs=row(D_MODEL),
        out_shape=jax.ShapeDtypeStruct((m, D_MODEL), F32),
        compiler_params=pltpu.CompilerParams(
            dimension_semantics=("arbitrary",), vmem_limit_bytes=VMEM_LIMIT),
        name="merge",
    )(x2, oa, ob, ga, gb, wa, wb, wo, ln_g, ln_b)


def _ffn_kernel(h_ref, wg_ref, wu_ref, wd_ref, g_ref, b_ref, o_ref, *, ff_chunk):
    d_ff = wg_ref.shape[1]
    n_chunks = d_ff // ff_chunk
    sl = lambda j: slice(j * ff_chunk, (j + 1) * ff_chunk)
    tm = h_ref.shape[0]
    for i in range(FFN_SPLIT):
        rs = slice(i * (tm // FFN_SPLIT), (i + 1) * (tm // FFN_SPLIT))
        h = h_ref[rs, :]
        hb = h.astype(BF16)

        def gate_up(j):
            return _dot(hb, wg_ref[:, sl(j)]), _dot(hb, wu_ref[:, sl(j)])

        acc = None
        gu = gate_up(0)
        for j in range(n_chunks):
            gate, up = gu
            if j + 1 < n_chunks:
                gu = gate_up(j + 1)
            act = (gate * jax.nn.sigmoid(gate) * up).astype(BF16)
            part = _dot(act, wd_ref[sl(j), :])
            acc = part if acc is None else acc + part
        o_ref[rs, :] = _layer_norm(DEEPNORM_ALPHA * h + acc, g_ref[...], b_ref[...])


def _ffn(h1, wg, wu, wd, ln_g, ln_b, tm, ff_chunk):
    m = h1.shape[0]
    d_ff = wg.shape[1]
    row = pl.BlockSpec((tm, D_MODEL), lambda i: (i, 0))
    vec = _resident((1, D_MODEL))
    return pl.pallas_call(
        functools.partial(_ffn_kernel, ff_chunk=ff_chunk),
        grid=(m // tm,),
        in_specs=[row, _resident((D_MODEL, d_ff)), _resident((D_MODEL, d_ff)),
                  _resident((d_ff, D_MODEL)), vec, vec],
        out_specs=row,
        out_shape=jax.ShapeDtypeStruct((m, D_MODEL), F32),
        compiler_params=pltpu.CompilerParams(
            dimension_semantics=("arbitrary",), vmem_limit_bytes=VMEM_LIMIT),
        name="ffn",
    )(h1, wg, wu, wd, ln_g, ln_b)


def _tail_kernel(x_ref, oa_ref, ob_ref, ga_ref, gb_ref, wa_ref, wb_ref, wo_ref, g1_ref, b1_ref,
                 wg_ref, wu_ref, wd_ref, g2_ref, b2_ref, o_ref, *, ff_chunk):
    d_ff = wg_ref.shape[1]
    n_chunks = d_ff // ff_chunk
    sl = lambda j: slice(j * ff_chunk, (j + 1) * ff_chunk)
    tm = x_ref.shape[0]
    parts = [slice(i * (tm // TAIL_SPLIT), (i + 1) * (tm // TAIL_SPLIT))
             for i in range(TAIL_SPLIT)]

    def merge(rs):
        y_a = _dot(oa_ref[rs, :], wa_ref[...])
        y_b = _dot(ob_ref[rs, :], wb_ref[...])
        merged = (jax.nn.sigmoid(ga_ref[rs, :].astype(F32)) * y_a
                  + jax.nn.sigmoid(gb_ref[rs, :].astype(F32)) * y_b)
        mix = _dot(merged.astype(BF16), wo_ref[...])
        return _layer_norm(DEEPNORM_ALPHA * x_ref[rs, :] + mix, g1_ref[...], b1_ref[...])

    def ffn(rs, h):
        hb = h.astype(BF16)

        def gate_up(j):
            return _dot(hb, wg_ref[:, sl(j)]), _dot(hb, wu_ref[:, sl(j)])

        acc = None
        gu = gate_up(0)
        for j in range(n_chunks):
            gate, up = gu
            if j + 1 < n_chunks:
                gu = gate_up(j + 1)
            act = (gate * jax.nn.sigmoid(gate) * up).astype(BF16)
            part = _dot(act, wd_ref[sl(j), :])
            acc = part if acc is None else acc + part
        o_ref[rs, :] = _layer_norm(DEEPNORM_ALPHA * h + acc, g2_ref[...], b2_ref[...])

    h_next = merge(parts[0])
    for i, rs in enumerate(parts):
        h = h_next
        if i + 1 < len(parts):
            h_next = merge(parts[i + 1])
        ffn(rs, h)


def _tail(x2, oa, ob, ga, gb, wa, wb, wo, ln1_g, ln1_b, wg, wu, wd, ln2_g, ln2_b, tm, ff_chunk):
    m = x2.shape[0]
    d_ff = wg.shape[1]
    row = pl.BlockSpec((tm, D_MODEL), lambda i: (i, 0))
    sq = _resident((D_MODEL, D_MODEL))
    vec = _resident((1, D_MODEL))
    return pl.pallas_call(
        functools.partial(_tail_kernel, ff_chunk=ff_chunk),
        grid=(m // tm,),
        in_specs=[row] * 5 + [sq, sq, sq, vec, vec,
                              _resident((D_MODEL, d_ff)), _resident((D_MODEL, d_ff)),
                              _resident((d_ff, D_MODEL)), vec, vec],
        out_specs=row,
        out_shape=jax.ShapeDtypeStruct((m, D_MODEL), F32),
        compiler_params=pltpu.CompilerParams(
            dimension_semantics=("arbitrary",), vmem_limit_bytes=VMEM_LIMIT),
        name="tail",
    )(x2, oa, ob, ga, gb, wa, wb, wo, ln1_g, ln1_b, wg, wu, wd, ln2_g, ln2_b)


def _rope_inv_freq():
    inv_freq = ROPE_THETA ** (-jnp.arange(0, ROPE_DIM, 2, dtype=F32) / ROPE_DIM)
    return jnp.broadcast_to(inv_freq[:, None], (ROPE_DIM // 2, LANES))


def _layer(h2, pos_row, invf, batch, seq, w_in, w_decay_up, b_decay, gla_norm_g,
           w_branch_a, w_branch_b, sinks, w_out, ln1_g, ln1_b,
           w_ffn_gate, w_ffn_up, w_ffn_down, ln2_g, ln2_b):
    w_in_r = _win_prep(w_in.T)
    w_up_p = jnp.concatenate(
        [w_decay_up, jnp.zeros((RANK_PAD - GLA_RANK, GLA_QK), w_decay_up.dtype)],
        axis=0).astype(BF16)

    neg_sink = -sinks.astype(F32) * LOG2_E
    sink_hi = neg_sink.astype(BF16).astype(F32)
    sink_feat = jnp.zeros((SWA_HEADS, LANES), F32)
    sink_feat = sink_feat.at[:, SWA_HEAD_DIM].set(sink_hi)
    sink_feat = sink_feat.at[:, SWA_HEAD_DIM + 1].set(neg_sink - sink_hi)

    qa, ka, va, ra, la, qh, kh, ve, vo, ga, gb = _inproj(
        h2, pos_row, invf, sink_feat, w_in_r, w_up_p, b_decay[None, :], tm=512)
    oa = _gla(qa, ka, va, ra, la, gla_norm_g[None, :], batch, seq, tt=512)
    ob = _swa(qh, kh, ve, vo, batch, seq, blocks_per_step=4)
    return _tail(h2, oa, ob, ga, gb, w_branch_a.astype(BF16), w_branch_b.astype(BF16),
                 w_out.astype(BF16), ln1_g[None, :], ln1_b[None, :],
                 w_ffn_gate.astype(BF16), w_ffn_up.astype(BF16), w_ffn_down.astype(BF16),
                 ln2_g[None, :], ln2_b[None, :], tm=512, ff_chunk=256)


def kernel(x, positions, w_in, w_decay_up, b_decay, gla_norm_g, w_branch_a, w_branch_b, sinks,
           w_out, ln1_g, ln1_b, w_ffn_gate, w_ffn_up, w_ffn_down, ln2_g, ln2_b):
    batch, seq, d = x.shape
    m = batch * seq
    h2 = x.reshape(m, d)
    pos_row = positions.reshape(1, m).astype(F32)
    invf = _rope_inv_freq()
    for layer in range(w_in.shape[0]):
        h2 = _layer(h2, pos_row, invf, batch, seq, w_in[layer], w_decay_up[layer], b_decay[layer],
                    gla_norm_g[layer], w_branch_a[layer], w_branch_b[layer], sinks[layer],
                    w_out[layer], ln1_g[layer], ln1_b[layer], w_ffn_gate[layer],
                    w_ffn_up[layer], w_ffn_down[layer], ln2_g[layer], ln2_b[layer])
    return h2.reshape(batch, seq, d)
```

```python
import functools

import math

import jax
import jax.numpy as jnp
import numpy as np
from jax import lax
from jax.experimental import pallas as pl
from jax.experimental.pallas import tpu as pltpu

D_MODEL = 1024
GLA_HEADS = 4
GLA_DK = 128
GLA_DV = 256
GLA_RANK = 16
GLA_TAU = 16.0
GLA_CHUNK = 64
GLA_BLOCK = GLA_CHUNK
SWA_HEADS = 16
SWA_KV_HEADS = 2
SWA_HEAD_DIM = 64
SWA_WINDOW = 128
ROPE_DIM = 16
ROPE_THETA = 500000.0
DEPTH = 1
DEEPNORM_ALPHA = (2.0 * DEPTH) ** 0.25
LN_EPS = 1e-5
RMS_EPS = 1e-6

GLA_QK = GLA_HEADS * GLA_DK
GLA_V = GLA_HEADS * GLA_DV
SWA_Q = SWA_HEADS * SWA_HEAD_DIM
SWA_KV = SWA_KV_HEADS * SWA_HEAD_DIM

LANES = 128
MXU_COLS = 256
RANK_PAD = LANES
VMEM_LIMIT = 56 * 1024 * 1024
TAIL_SPLIT = 2
LOG2_E = math.log2(math.e)
MASK_BIAS = -1e30

_C_QA = 0
_C_KA = _C_QA + GLA_QK
_C_VA = _C_KA + GLA_QK
_C_RA = _C_VA + GLA_V
_C_QB = _C_RA + GLA_V
_C_KV = _C_QB + SWA_Q
_C_GA = _C_KV + 2 * SWA_KV
_C_GB = _C_GA + D_MODEL
_C_DL = _C_GB + D_MODEL
_C_END = _C_DL + RANK_PAD
_D_IN = 2 * GLA_QK + 2 * GLA_V + GLA_RANK + SWA_Q + 2 * SWA_KV + 2 * D_MODEL
_O_DL = 2 * GLA_QK + 2 * GLA_V

F32 = jnp.float32
BF16 = jnp.bfloat16


def _dot(a, b):
    return jnp.dot(a, b, preferred_element_type=F32)


def _dot_nt(a, b):
    return lax.dot_general(a, b, (((1,), (1,)), ((), ())), preferred_element_type=F32)


def _dot_tn(a, b):
    return lax.dot_general(a, b, (((0,), (0,)), ((), ())), preferred_element_type=F32)


def _resident(shape):
    return pl.BlockSpec(shape, lambda *_: (0,) * len(shape), pipeline_mode=pl.Buffered(1))


def _win_prep_kernel(wt_ref, o_ref):
    def put(col, row):
        o_ref[:, col:col + LANES] = wt_ref[row:row + LANES, :].T.astype(BF16)

    for j in range(_O_DL // LANES):
        put(j * LANES, j * LANES)
    for j in range((_C_DL - _O_DL) // LANES):
        put(_O_DL + j * LANES, _O_DL + GLA_RANK + j * LANES)
    blk = wt_ref[_O_DL:_O_DL + LANES, :].T
    lane = lax.broadcasted_iota(jnp.int32, blk.shape, 1)
    o_ref[:, _C_DL:] = jnp.where(lane < GLA_RANK, blk, 0.0).astype(BF16)


def _win_prep(w_in_t):
    cb = LANES
    assert _C_DL - _O_DL == _D_IN - _O_DL - GLA_RANK and (_C_DL - _O_DL) % LANES == 0
    return pl.pallas_call(
        _win_prep_kernel,
        grid=(D_MODEL // cb,),
        in_specs=[pl.BlockSpec((_D_IN, cb), lambda i: (0, i))],
        out_specs=pl.BlockSpec((cb, _C_END), lambda i: (i, 0)),
        out_shape=jax.ShapeDtypeStruct((D_MODEL, _C_END), BF16),
        compiler_params=pltpu.CompilerParams(dimension_semantics=("arbitrary",)),
        name="win_prep",
    )(w_in_t)


def _inproj_kernel(x_ref, pos_ref, invf_ref, spread_ref, sf_ref, w_ref, wup_ref, bdec_ref,
                   gain_ref,
                   oa_ref, qh_ref, kh_ref, ve_ref, vo_ref, ga_ref, gb_ref,
                   qa_ref, ka_ref, va_ref, ra_ref, la_ref, st_ref, *, tiles_per_seq):
    xb = x_ref[...].astype(BF16)

    def proj(lo, hi):
        return _dot(xb, w_ref[:, lo:hi])

    tm = x_ref.shape[0]
    half = ROPE_DIM // 2
    ang = jnp.tile(invf_ref[...], (1, tm // LANES)) * pos_ref[...]
    one_row = jnp.where(lax.broadcasted_iota(jnp.int32, (2 * half, tm), 0) == 0, 1.0, 0.0)
    cs = jnp.concatenate([jnp.cos(ang), jnp.sin(ang), one_row], axis=0)
    cs_hi = cs.astype(BF16)
    cs_lo = (cs - cs_hi.astype(F32)).astype(BF16)
    tabs = _dot_tn(jnp.concatenate([cs_hi, cs_lo], axis=0), spread_ref[...])
    cos, a_tab, b_tab = tabs[:, :LANES], tabs[:, LANES:2 * LANES], tabs[:, 2 * LANES:]

    def rope(t):
        return (t * cos + pltpu.roll(t, LANES - half, 1) * a_tab
                + pltpu.roll(t, half, 1) * b_tab)

    dh = SWA_HEAD_DIM
    lane_full = lax.broadcasted_iota(jnp.int32, (tm, LANES), 1)
    low = lane_full < dh
    q_scale = (dh ** -0.5) * LOG2_E

    def swa_queries(i):
        qb = proj(_C_QB + i * MXU_COLS, _C_QB + (i + 1) * MXU_COLS)
        for c in range(2 * i, 2 * i + 2):
            t = rope(qb[:, (c - 2 * i) * LANES:(c - 2 * i + 1) * LANES]) * q_scale
            qh_ref[2 * c] = jnp.where(low, t, sf_ref[2 * c:2 * c + 1, :]).astype(BF16)
            qh_ref[2 * c + 1] = jnp.where(
                low, pltpu.roll(t, dh, 1), sf_ref[2 * c + 1:2 * c + 2, :]).astype(BF16)

    def swa_keys_values_and_decay():
        kv = proj(_C_KV, _C_GA)
        kk = rope(kv[:, :SWA_KV])
        vv = kv[:, SWA_KV:]
        kk_sw = pltpu.roll(kk, dh, 1)
        vv_sw = pltpu.roll(vv, dh, 1)
        one_k = jnp.where(lane_full < dh + 2, 1.0, 0.0)
        one_e = jnp.where(lane_full == dh, 1.0, 0.0)
        one_o = jnp.where(lane_full == 0, 1.0, 0.0)
        kh_ref[0] = jnp.where(low, kk, one_k).astype(BF16)
        kh_ref[1] = jnp.where(low, kk_sw, one_k).astype(BF16)
        ve_ref[0] = jnp.where(low, vv, one_e).astype(BF16)
        ve_ref[1] = jnp.where(low, vv_sw, one_e).astype(BF16)
        vo_ref[0] = jnp.where(low, one_o, vv_sw).astype(BF16)
        vo_ref[1] = jnp.where(low, one_o, vv).astype(BF16)
        return proj(_C_DL, _C_END).astype(BF16)

    def log_decay(d_low, h):
        cols = slice(h * GLA_DK, (h + 1) * GLA_DK)
        z = _dot(d_low, wup_ref[:, cols]) + bdec_ref[:, cols]
        log_sig = jnp.minimum(z, 0.0) - jnp.log1p(jnp.exp(-jnp.abs(z)))
        la_ref[:, cols] = log_sig * (LOG2_E / GLA_TAU)

    def plain(ref, col0, i, scale=None):
        y = proj(col0 + i * MXU_COLS, col0 + (i + 1) * MXU_COLS)
        if scale is not None:
            y = y * scale
        ref[:, i * MXU_COLS:(i + 1) * MXU_COLS] = y.astype(BF16)

    gla_tiles = (
        [(qa_ref, _C_QA, i, GLA_DK ** -0.5) for i in range(GLA_QK // MXU_COLS)]
        + [(ka_ref, _C_KA, i, None) for i in range(GLA_QK // MXU_COLS)]
        + [(va_ref, _C_VA, i, None) for i in range(GLA_V // MXU_COLS)]
        + [(ra_ref, _C_RA, i, None) for i in range(GLA_V // MXU_COLS)])
    d_low = swa_keys_values_and_decay()
    per_head = len(gla_tiles) // GLA_HEADS
    for h in range(GLA_HEADS):
        log_decay(d_low, h)
        for args in gla_tiles[h * per_head:(h + 1) * per_head]:
            plain(*args)

    first_tile = pl.program_id(0) % tiles_per_seq == 0
    gla = _gla_stages(first_tile, qa_ref, ka_ref, va_ref, ra_ref, la_ref, gain_ref, oa_ref,
                      st_ref, n_blocks=tm // GLA_BLOCK)
    rest = []
    for i in range(SWA_Q // MXU_COLS):
        rest.append(functools.partial(swa_queries, i))
        rest.append(functools.partial(plain, ga_ref, _C_GA, i))
    rest += [functools.partial(plain, gb_ref, _C_GB, i) for i in range(D_MODEL // MXU_COLS)]
    n_gla_steps = 3 * (tm // GLA_BLOCK) + 1
    slots = len(rest) - 1
    done = 0
    for n, task in enumerate(rest):
        target = min(n_gla_steps, -(-n_gla_steps * (n + 1) // slots))
        for _ in range(target - done):
            next(gla, None)
        done = target
        task()
    for _ in gla:
        pass


def _rope_spread():
    half = ROPE_DIM // 2
    rows = 4 * half
    e = np.zeros((rows, 3 * LANES), np.float32)
    for lane in range(LANES):
        j = lane % SWA_HEAD_DIM
        if j < half:
            e[j, lane] = 1.0
            e[half + j, LANES + lane] = -1.0
        elif j < 2 * half:
            e[j - half, lane] = 1.0
            e[j, 2 * LANES + lane] = 1.0
        else:
            e[2 * half, lane] = 1.0
    return jnp.asarray(np.concatenate([e, e], axis=0), dtype=BF16)


def _inproj(x2, pos_row, invf, sink_feat, w_in_r, w_up_p, b_dec, gain, seq, tm):
    m = x2.shape[0]
    half = ROPE_DIM // 2
    row = lambda c: pl.BlockSpec((tm, c), lambda i: (i, 0))
    row_out = lambda c, d: (row(c), jax.ShapeDtypeStruct((m, c), d))
    slab_out = lambda n: (pl.BlockSpec((n, tm, LANES), lambda i: (0, i, 0)),
                          jax.ShapeDtypeStruct((n, m, LANES), BF16))
    outs = [row_out(GLA_V, BF16),
            slab_out(SWA_HEADS), slab_out(SWA_KV_HEADS), slab_out(SWA_KV_HEADS),
            slab_out(SWA_KV_HEADS), row_out(D_MODEL, BF16), row_out(D_MODEL, BF16)]
    return pl.pallas_call(
        functools.partial(_inproj_kernel, tiles_per_seq=seq // tm),
        grid=(m // tm,),
        in_specs=[row(D_MODEL), pl.BlockSpec((1, tm), lambda i: (0, i)),
                  _resident((half, LANES)), _resident((8 * half, 3 * LANES)),
                  _resident((SWA_HEADS, LANES)),
                  _resident((D_MODEL, _C_END)), _resident((RANK_PAD, GLA_QK)),
                  _resident((1, GLA_QK)), _resident((1, GLA_DV))],
        out_specs=[o[0] for o in outs],
        out_shape=[o[1] for o in outs],
        scratch_shapes=[pltpu.VMEM((tm, GLA_QK), BF16), pltpu.VMEM((tm, GLA_QK), BF16),
                        pltpu.VMEM((tm, GLA_V), BF16), pltpu.VMEM((tm, GLA_V), BF16),
                        pltpu.VMEM((tm, GLA_QK), F32),
                        pltpu.VMEM((GLA_HEADS, GLA_DK, GLA_DV), F32)],
        compiler_params=pltpu.CompilerParams(
            dimension_semantics=("arbitrary",), vmem_limit_bytes=VMEM_LIMIT),
        name="inproj",
    )(x2, pos_row, invf, _rope_spread(), sink_feat, w_in_r, w_up_p, b_dec, gain)


def _gla_stages(first_tile, q_ref, k_ref, v_ref, r_ref, la_ref, g_ref, o_ref, st_ref, *,
                n_blocks):
    @pl.when(first_tile)
    def _():
        st_ref[...] = jnp.zeros_like(st_ref)

    c_ = GLA_BLOCK
    causal = (lax.broadcasted_iota(jnp.int32, (c_, c_), 0)
              >= lax.broadcasted_iota(jnp.int32, (c_, c_), 1))
    tri = jnp.where(causal, 1.0, 0.0).astype(BF16)
    gain = g_ref[...]

    items = [(c, h) for c in range(n_blocks) for h in range(GLA_HEADS)]
    rows = lambda c: slice(c * c_, (c + 1) * c_)
    ks = lambda h: slice(h * GLA_DK, (h + 1) * GLA_DK)
    vs = lambda h: slice(h * GLA_DV, (h + 1) * GLA_DV)

    b = []
    for c in range(n_blocks):
        la = la_ref[rows(c), :]
        hi = la.astype(BF16)
        rem = la - hi.astype(F32)
        mid = rem.astype(BF16)
        lo = (rem - mid.astype(F32)).astype(BF16)
        b.append(_dot(tri, hi) + _dot(tri, mid) + _dot(tri, lo))

    q_in, scores, contrib, decay, o_intra = {}, {}, {}, {}, {}

    def stage_a(c):
        for h in range(GLA_HEADS):
            bh = b[c][:, ks(h)]
            b_last = bh[c_ - 1:c_, :]
            q = q_ref[rows(c), ks(h)].astype(F32)
            k = k_ref[rows(c), ks(h)].astype(F32)
            q_in[c, h] = (q * jnp.exp2(bh)).astype(BF16)
            k_in = (k * jnp.exp2(-bh)).astype(BF16)
            k_out = (k * jnp.exp2(b_last - bh)).astype(BF16)
            scores[c, h] = _dot_nt(q_in[c, h], k_in)
            contrib[c, h] = _dot_tn(k_out, v_ref[rows(c), vs(h)])
            dec = jnp.exp2(jnp.broadcast_to(b_last, (GLA_DK, GLA_DK)).T)
            decay[c, h] = jnp.concatenate([dec] * (GLA_DV // GLA_DK), axis=1)

    def stage_b(c):
        for h in range(GLA_HEADS):
            masked = jnp.where(causal, scores[c, h], 0.0).astype(BF16)
            o_intra[c, h] = _dot(masked, v_ref[rows(c), vs(h)])

    st = [st_ref[h] for h in range(GLA_HEADS)]

    def stage_c(c):
        for h in range(GLA_HEADS):
            o = o_intra[c, h] + _dot(q_in[c, h], st[h].astype(BF16))
            st[h] = st[h] * decay[c, h] + contrib[c, h]
            o = o * lax.rsqrt(jnp.mean(o * o, axis=-1, keepdims=True) + RMS_EPS)
            r = r_ref[rows(c), vs(h)].astype(F32)
            o = o * gain * (r * jax.nn.sigmoid(r))
            o_ref[rows(c), vs(h)] = o.astype(BF16)

    for stage in (stage_a, stage_b, stage_c):
        for c in range(n_blocks):
            yield
            stage(c)
    for h in range(GLA_HEADS):
        st_ref[h] = st[h]


def _swa_kernel(q_ref, kc_ref, kp_ref, vec_ref, vep_ref, voc_ref, vop_ref, mask_ref, o_ref,
                *, blocks_per_step):
    w = SWA_WINDOW
    dh = SWA_HEAD_DIM
    per_group = SWA_HEADS // SWA_KV_HEADS
    first = pl.program_id(1) == 0
    row = lax.broadcasted_iota(jnp.int32, (w, LANES), 0)
    lane = lax.broadcasted_iota(jnp.int32, (w, LANES), 1)
    ident = jnp.where(row == lane, 1.0, 0.0).astype(BF16)
    sink_row = row == 0
    low = lane < dh
    zero = jnp.zeros((w, LANES), BF16)
    one_e = jnp.where(lane == dh, 1.0, 0.0).astype(BF16)
    one_o = jnp.where(lane == 0, 1.0, 0.0).astype(BF16)
    order = list(range(0, per_group, 2)) + list(range(1, per_group, 2))

    scores = []
    values = []
    for qb in range(blocks_per_step):
        rows = slice(qb * w, (qb + 1) * w)
        prev_rows = slice((qb - 1) * w, qb * w)
        mask = mask_ref[1] if qb else jnp.where(first, mask_ref[0], mask_ref[1])
        for g in range(SWA_KV_HEADS):
            k_prev = kc_ref[g, prev_rows] if qb else kp_ref[g]
            ve_prev = vec_ref[g, prev_rows] if qb else vep_ref[g]
            vo_prev = voc_ref[g, prev_rows] if qb else vop_ref[g]
            k2 = jnp.concatenate([jnp.where(sink_row, zero, k_prev), kc_ref[g, rows]], axis=0)
            kx = jnp.concatenate([k2, mask], axis=1)
            ve = jnp.concatenate([jnp.where(sink_row, one_e, ve_prev), vec_ref[g, rows]], axis=0)
            vo = jnp.concatenate([jnp.where(sink_row, one_o, vo_prev), voc_ref[g, rows]], axis=0)
            for part, vv in ((order[:per_group // 2], ve), (order[per_group // 2:], vo)):
                qs = jnp.concatenate(
                    [jnp.concatenate([q_ref[g * per_group + h, rows], ident], axis=1)
                     for h in part], axis=0)
                scores.append(_dot_nt(qs, kx))
                values.append(vv)
    pv = []
    for s, vv in zip(scores, values):
        p = jnp.exp2(s - jnp.max(s, axis=-1, keepdims=True)).astype(BF16)
        pv.append(_dot(p, vv))
    for qb in range(blocks_per_step):
        rows = slice(qb * w, (qb + 1) * w)
        for g in range(SWA_KV_HEADS):
            pv_e = pv[(qb * SWA_KV_HEADS + g) * 2]
            pv_o = pv[(qb * SWA_KV_HEADS + g) * 2 + 1]
            for c in range(per_group // 2):
                e = pv_e[c * w:(c + 1) * w]
                o = pv_o[c * w:(c + 1) * w]
                out = jnp.where(low, e * (1.0 / e[:, dh:dh + 1]), o * (1.0 / o[:, 0:1]))
                col = (g * (per_group // 2) + c) * LANES
                o_ref[rows, col:col + LANES] = out.astype(BF16)


def _swa_mask():
    w = SWA_WINDOW
    j = np.arange(2 * w)[:, None]
    i = np.arange(w)[None, :]
    band = (j > i) & (j <= i + w)
    later = band | (j == 0)
    first = (band & (j >= w)) | (j == 0)
    both = np.stack([first, later])
    return jnp.asarray(np.where(both, 0.0, MASK_BIAS), dtype=BF16)


def _swa(qh, kh, ve, vo, batch, seq, blocks_per_step):
    m = kh.shape[1]
    w = SWA_WINDOW
    tq = blocks_per_step * w
    steps = seq // tq
    kv = SWA_KV_HEADS
    cur = lambda n_: pl.BlockSpec((n_, tq, LANES), lambda b, t: (0, b * steps + t, 0))
    prev = lambda n_: pl.BlockSpec(
        (n_, w, LANES),
        lambda b, t: (0, b * (seq // w) + jnp.maximum(t * blocks_per_step - 1, 0), 0))
    return pl.pallas_call(
        functools.partial(_swa_kernel, blocks_per_step=blocks_per_step),
        grid=(batch, steps),
        in_specs=[cur(SWA_HEADS), cur(kv), prev(kv), cur(kv), prev(kv), cur(kv), prev(kv),
                  _resident((2, 2 * w, w))],
        out_specs=pl.BlockSpec((tq, SWA_Q), lambda b, t: (b * steps + t, 0)),
        out_shape=jax.ShapeDtypeStruct((m, SWA_Q), BF16),
        compiler_params=pltpu.CompilerParams(
            dimension_semantics=("arbitrary", "arbitrary"), vmem_limit_bytes=VMEM_LIMIT),
        name="swa",
    )(qh, kh, kh, ve, ve, vo, vo, _swa_mask())


def _layer_norm(y, g, b):
    mu = jnp.mean(y, axis=-1, keepdims=True)
    d = y - mu
    var = jnp.mean(d * d, axis=-1, keepdims=True)
    return d * lax.rsqrt(var + LN_EPS) * g + b


def _tail_kernel(x_ref, oa_ref, ob_ref, ga_ref, gb_ref, wa_ref, wb_ref, wo_ref, g1_ref, b1_ref,
                 wg_ref, wu_ref, wd_ref, g2_ref, b2_ref, o_ref, *, ff_chunk):
    d_ff = wg_ref.shape[1]
    n_chunks = d_ff // ff_chunk
    sl = lambda j: slice(j * ff_chunk, (j + 1) * ff_chunk)
    tm = x_ref.shape[0]
    parts = [slice(i * (tm // TAIL_SPLIT), (i + 1) * (tm // TAIL_SPLIT))
             for i in range(TAIL_SPLIT)]

    def merge(rs):
        y_a = _dot(oa_ref[rs, :], wa_ref[...])
        y_b = _dot(ob_ref[rs, :], wb_ref[...])
        merged = (jax.nn.sigmoid(ga_ref[rs, :].astype(F32)) * y_a
                  + jax.nn.sigmoid(gb_ref[rs, :].astype(F32)) * y_b)
        mix = _dot(merged.astype(BF16), wo_ref[...])
        return _layer_norm(DEEPNORM_ALPHA * x_ref[rs, :] + mix, g1_ref[...], b1_ref[...])

    def ffn(rs, h):
        hb = h.astype(BF16)

        def gate_up(j):
            return _dot(hb, wg_ref[:, sl(j)]), _dot(hb, wu_ref[:, sl(j)])

        acc = None
        gu = gate_up(0)
        for j in range(n_chunks):
            gate, up = gu
            if j + 1 < n_chunks:
                gu = gate_up(j + 1)
            act = (gate * jax.nn.sigmoid(gate) * up).astype(BF16)
            part = _dot(act, wd_ref[sl(j), :])
            acc = part if acc is None else acc + part
        o_ref[rs, :] = _layer_norm(DEEPNORM_ALPHA * h + acc, g2_ref[...], b2_ref[...])

    h_next = merge(parts[0])
    for i, rs in enumerate(parts):
        h = h_next
        if i + 1 < len(parts):
            h_next = merge(parts[i + 1])
        ffn(rs, h)


def _tail(x2, oa, ob, ga, gb, wa, wb, wo, ln1_g, ln1_b, wg, wu, wd, ln2_g, ln2_b, tm, ff_chunk):
    m = x2.shape[0]
    d_ff = wg.shape[1]
    row = pl.BlockSpec((tm, D_MODEL), lambda i: (i, 0))
    sq = _resident((D_MODEL, D_MODEL))
    vec = _resident((1, D_MODEL))
    return pl.pallas_call(
        functools.partial(_tail_kernel, ff_chunk=ff_chunk),
        grid=(m // tm,),
        in_specs=[row] * 5 + [sq, sq, sq, vec, vec,
                              _resident((D_MODEL, d_ff)), _resident((D_MODEL, d_ff)),
                              _resident((d_ff, D_MODEL)), vec, vec],
        out_specs=row,
        out_shape=jax.ShapeDtypeStruct((m, D_MODEL), F32),
        compiler_params=pltpu.CompilerParams(
            dimension_semantics=("arbitrary",), vmem_limit_bytes=VMEM_LIMIT),
        name="tail",
    )(x2, oa, ob, ga, gb, wa, wb, wo, ln1_g, ln1_b, wg, wu, wd, ln2_g, ln2_b)


def _rope_inv_freq():
    inv_freq = ROPE_THETA ** (-jnp.arange(0, ROPE_DIM, 2, dtype=F32) / ROPE_DIM)
    return jnp.broadcast_to(inv_freq[:, None], (ROPE_DIM // 2, LANES))


def _layer(h2, pos_row, invf, batch, seq, w_in, w_decay_up, b_decay, gla_norm_g,
           w_branch_a, w_branch_b, sinks, w_out, ln1_g, ln1_b,
           w_ffn_gate, w_ffn_up, w_ffn_down, ln2_g, ln2_b):
    w_in_r = _win_prep(w_in.T)
    w_up_p = jnp.concatenate(
        [w_decay_up, jnp.zeros((RANK_PAD - GLA_RANK, GLA_QK), w_decay_up.dtype)],
        axis=0).astype(BF16)

    neg_sink = -sinks.astype(F32) * LOG2_E
    sink_hi = neg_sink.astype(BF16).astype(F32)
    sink_feat = jnp.zeros((SWA_HEADS, LANES), F32)
    sink_feat = sink_feat.at[:, SWA_HEAD_DIM].set(sink_hi)
    sink_feat = sink_feat.at[:, SWA_HEAD_DIM + 1].set(neg_sink - sink_hi)

    oa, qh, kh, ve, vo, ga, gb = _inproj(
        h2, pos_row, invf, sink_feat, w_in_r, w_up_p, b_decay[None, :], gla_norm_g[None, :],
        seq, tm=512)
    ob = _swa(qh, kh, ve, vo, batch, seq, blocks_per_step=4)
    return _tail(h2, oa, ob, ga, gb, w_branch_a.astype(BF16), w_branch_b.astype(BF16),
                 w_out.astype(BF16), ln1_g[None, :], ln1_b[None, :],
                 w_ffn_gate.astype(BF16), w_ffn_up.astype(BF16), w_ffn_down.astype(BF16),
                 ln2_g[None, :], ln2_b[None, :], tm=512, ff_chunk=256)


def kernel(x, positions, w_in, w_decay_up, b_decay, gla_norm_g, w_branch_a, w_branch_b, sinks,
           w_out, ln1_g, ln1_b, w_ffn_gate, w_ffn_up, w_ffn_down, ln2_g, ln2_b):
    batch, seq, d = x.shape
    m = batch * seq
    h2 = x.reshape(m, d)
    pos_row = positions.reshape(1, m).astype(F32)
    invf = _rope_inv_freq()
    for layer in range(w_in.shape[0]):
        h2 = _layer(h2, pos_row, invf, batch, seq, w_in[layer], w_decay_up[layer], b_decay[layer],
                    gla_norm_g[layer], w_branch_a[layer], w_branch_b[layer], sinks[layer],
                    w_out[layer], ln1_g[layer], ln1_b[layer], w_ffn_gate[layer],
                    w_ffn_up[layer], w_ffn_down[layer], ln2_g[layer], ln2_b[layer])
    return h2.reshape(batch, seq, d)
```

```python
import functools

import math

import jax
import jax.numpy as jnp
import numpy as np
from jax import lax
from jax.experimental import pallas as pl
from jax.experimental.pallas import tpu as pltpu

D_MODEL = 1024
GLA_HEADS = 4
GLA_DK = 128
GLA_DV = 256
GLA_RANK = 16
GLA_TAU = 16.0
GLA_CHUNK = 64
GLA_BLOCK = GLA_CHUNK
SWA_HEADS = 16
SWA_KV_HEADS = 2
SWA_HEAD_DIM = 64
SWA_WINDOW = 128
ROPE_DIM = 16
ROPE_THETA = 500000.0
DEPTH = 1
DEEPNORM_ALPHA = (2.0 * DEPTH) ** 0.25
LN_EPS = 1e-5
RMS_EPS = 1e-6

GLA_QK = GLA_HEADS * GLA_DK
GLA_V = GLA_HEADS * GLA_DV
SWA_Q = SWA_HEADS * SWA_HEAD_DIM
SWA_KV = SWA_KV_HEADS * SWA_HEAD_DIM

LANES = 128
MXU_COLS = 256
RANK_PAD = LANES
VMEM_LIMIT = 56 * 1024 * 1024
TAIL_SPLIT = 2
LOG2_E = math.log2(math.e)
MASK_BIAS = -1e30

_C_QA = 0
_C_KA = _C_QA + GLA_QK
_C_VA = _C_KA + GLA_QK
_C_RA = _C_VA + GLA_V
_C_QB = _C_RA + GLA_V
_C_KV = _C_QB + SWA_Q
_C_GA = _C_KV + 2 * SWA_KV
_C_GB = _C_GA + D_MODEL
_C_DL = _C_GB + D_MODEL
_C_END = _C_DL + RANK_PAD
_D_IN = 2 * GLA_QK + 2 * GLA_V + GLA_RANK + SWA_Q + 2 * SWA_KV + 2 * D_MODEL
_O_DL = 2 * GLA_QK + 2 * GLA_V

F32 = jnp.float32
BF16 = jnp.bfloat16


def _dot(a, b):
    return jnp.dot(a, b, preferred_element_type=F32)


def _dot_nt(a, b):
    return lax.dot_general(a, b, (((1,), (1,)), ((), ())), preferred_element_type=F32)


def _dot_tn(a, b):
    return lax.dot_general(a, b, (((0,), (0,)), ((), ())), preferred_element_type=F32)


def _resident(shape):
    return pl.BlockSpec(shape, lambda *_: (0,) * len(shape), pipeline_mode=pl.Buffered(1))


def _win_prep_kernel(wt_ref, o_ref):
    def put(col, row):
        o_ref[:, col:col + LANES] = wt_ref[row:row + LANES, :].T.astype(BF16)

    for j in range(_O_DL // LANES):
        put(j * LANES, j * LANES)
    for j in range((_C_DL - _O_DL) // LANES):
        put(_O_DL + j * LANES, _O_DL + GLA_RANK + j * LANES)
    blk = wt_ref[_O_DL:_O_DL + LANES, :].T
    lane = lax.broadcasted_iota(jnp.int32, blk.shape, 1)
    o_ref[:, _C_DL:] = jnp.where(lane < GLA_RANK, blk, 0.0).astype(BF16)


def _win_prep(w_in_t):
    cb = LANES
    assert _C_DL - _O_DL == _D_IN - _O_DL - GLA_RANK and (_C_DL - _O_DL) % LANES == 0
    return pl.pallas_call(
        _win_prep_kernel,
        grid=(D_MODEL // cb,),
        in_specs=[pl.BlockSpec((_D_IN, cb), lambda i: (0, i))],
        out_specs=pl.BlockSpec((cb, _C_END), lambda i: (i, 0)),
        out_shape=jax.ShapeDtypeStruct((D_MODEL, _C_END), BF16),
        compiler_params=pltpu.CompilerParams(dimension_semantics=("arbitrary",)),
        name="win_prep",
    )(w_in_t)


def _inproj_kernel(x_ref, pos_ref, invf_ref, spread_ref, sf_ref, w_ref, wup_ref, bdec_ref,
                   gain_ref,
                   oa_ref, qh_ref, kh_ref, ve_ref, vo_ref, ga_ref, gb_ref,
                   qa_ref, ka_ref, va_ref, ra_ref, la_ref, st_ref, *, tiles_per_seq):
    xb = x_ref[...].astype(BF16)

    def proj(lo, hi):
        return _dot(xb, w_ref[:, lo:hi])

    tm = x_ref.shape[0]
    half = ROPE_DIM // 2
    ang = jnp.tile(invf_ref[...], (1, tm // LANES)) * pos_ref[...]
    one_row = jnp.where(lax.broadcasted_iota(jnp.int32, (2 * half, tm), 0) == 0, 1.0, 0.0)
    cs = jnp.concatenate([jnp.cos(ang), jnp.sin(ang), one_row], axis=0)
    cs_hi = cs.astype(BF16)
    cs_lo = (cs - cs_hi.astype(F32)).astype(BF16)
    tabs = _dot_tn(jnp.concatenate([cs_hi, cs_lo], axis=0), spread_ref[...])
    cos, a_tab, b_tab = tabs[:, :LANES], tabs[:, LANES:2 * LANES], tabs[:, 2 * LANES:]

    def rope(t):
        return (t * cos + pltpu.roll(t, LANES - half, 1) * a_tab
                + pltpu.roll(t, half, 1) * b_tab)

    dh = SWA_HEAD_DIM
    lane_full = lax.broadcasted_iota(jnp.int32, (tm, LANES), 1)
    low = lane_full < dh
    q_scale = (dh ** -0.5) * LOG2_E

    def swa_queries(i):
        qb = proj(_C_QB + i * MXU_COLS, _C_QB + (i + 1) * MXU_COLS)
        for c in range(2 * i, 2 * i + 2):
            t = rope(qb[:, (c - 2 * i) * LANES:(c - 2 * i + 1) * LANES]) * q_scale
            qh_ref[2 * c] = jnp.where(low, t, sf_ref[2 * c:2 * c + 1, :]).astype(BF16)
            qh_ref[2 * c + 1] = jnp.where(
                low, pltpu.roll(t, dh, 1), sf_ref[2 * c + 1:2 * c + 2, :]).astype(BF16)

    def swa_keys_values_and_decay():
        kv = proj(_C_KV, _C_GA)
        kk = rope(kv[:, :SWA_KV])
        vv = kv[:, SWA_KV:]
        kk_sw = pltpu.roll(kk, dh, 1)
        vv_sw = pltpu.roll(vv, dh, 1)
        one_k = jnp.where(lane_full < dh + 2, 1.0, 0.0)
        one_e = jnp.where(lane_full == dh, 1.0, 0.0)
        one_o = jnp.where(lane_full == 0, 1.0, 0.0)
        kh_ref[0] = jnp.where(low, kk, one_k).astype(BF16)
        kh_ref[1] = jnp.where(low, kk_sw, one_k).astype(BF16)
        ve_ref[0] = jnp.where(low, vv, one_e).astype(BF16)
        ve_ref[1] = jnp.where(low, vv_sw, one_e).astype(BF16)
        vo_ref[0] = jnp.where(low, one_o, vv_sw).astype(BF16)
        vo_ref[1] = jnp.where(low, one_o, vv).astype(BF16)
        return proj(_C_DL, _C_END).astype(BF16)

    def log_decay(d_low, h):
        cols = slice(h * GLA_DK, (h + 1) * GLA_DK)
        z = _dot(d_low, wup_ref[:, cols]) + bdec_ref[:, cols]
        log_sig = jnp.minimum(z, 0.0) - jnp.log1p(jnp.exp(-jnp.abs(z)))
        la_ref[:, cols] = log_sig * (LOG2_E / GLA_TAU)

    def plain(ref, col0, i, scale=None, width=MXU_COLS):
        y = proj(col0 + i * MXU_COLS, col0 + i * MXU_COLS + width)
        if scale is not None:
            y = y * scale
        ref[:, i * MXU_COLS:i * MXU_COLS + width] = y.astype(BF16)

    gla_tiles = (
        [(qa_ref, _C_QA, i, GLA_DK ** -0.5) for i in range(GLA_QK // MXU_COLS)]
        + [(ka_ref, _C_KA, i, None) for i in range(GLA_QK // MXU_COLS)]
        + [(va_ref, _C_VA, i, None) for i in range(GLA_V // MXU_COLS)]
        + [(ra_ref, _C_RA, i, None) for i in range(GLA_V // MXU_COLS)])
    d_low = swa_keys_values_and_decay()
    per_head = len(gla_tiles) // GLA_HEADS
    for h in range(GLA_HEADS):
        log_decay(d_low, h)
        for args in gla_tiles[h * per_head:(h + 1) * per_head]:
            plain(*args)

    first_tile = pl.program_id(0) % tiles_per_seq == 0
    gla = _gla_stages(first_tile, qa_ref, ka_ref, va_ref, ra_ref, la_ref, gain_ref, oa_ref,
                      st_ref, n_blocks=tm // GLA_BLOCK)
    rest = []
    for i in range(SWA_Q // MXU_COLS):
        rest.append(functools.partial(swa_queries, i))
        rest.append(functools.partial(plain, ga_ref, _C_GA, i))
    n_gb = D_MODEL // MXU_COLS
    rest += [functools.partial(plain, gb_ref, _C_GB, i) for i in range(n_gb - 2)]
    rest.append(functools.partial(plain, gb_ref, _C_GB, n_gb - 2, width=2 * MXU_COLS))
    n_gla_steps = 3 * (tm // GLA_BLOCK) + 1
    slots = len(rest) - 3
    next(gla)
    done = 1
    for n, task in enumerate(rest):
        task()
        if n >= 1:
            target = min(n_gla_steps, -(-n_gla_steps * n // slots))
            for _ in range(target - done):
                next(gla, None)
            done = target
    for _ in gla:
        pass


def _rope_spread():
    half = ROPE_DIM // 2
    rows = 4 * half
    e = np.zeros((rows, 3 * LANES), np.float32)
    for lane in range(LANES):
        j = lane % SWA_HEAD_DIM
        if j < half:
            e[j, lane] = 1.0
            e[half + j, LANES + lane] = -1.0
        elif j < 2 * half:
            e[j - half, lane] = 1.0
            e[j, 2 * LANES + lane] = 1.0
        else:
            e[2 * half, lane] = 1.0
    return jnp.asarray(np.concatenate([e, e], axis=0), dtype=BF16)


def _inproj(x2, pos_row, invf, sink_feat, w_in_r, w_up_p, b_dec, gain, seq, tm):
    m = x2.shape[0]
    half = ROPE_DIM // 2
    row = lambda c: pl.BlockSpec((tm, c), lambda i: (i, 0))
    row_out = lambda c, d: (row(c), jax.ShapeDtypeStruct((m, c), d))
    slab_out = lambda n: (pl.BlockSpec((n, tm, LANES), lambda i: (0, i, 0)),
                          jax.ShapeDtypeStruct((n, m, LANES), BF16))
    outs = [row_out(GLA_V, BF16),
            slab_out(SWA_HEADS), slab_out(SWA_KV_HEADS), slab_out(SWA_KV_HEADS),
            slab_out(SWA_KV_HEADS), row_out(D_MODEL, BF16), row_out(D_MODEL, BF16)]
    return pl.pallas_call(
        functools.partial(_inproj_kernel, tiles_per_seq=seq // tm),
        grid=(m // tm,),
        in_specs=[row(D_MODEL), pl.BlockSpec((1, tm), lambda i: (0, i)),
                  _resident((half, LANES)), _resident((8 * half, 3 * LANES)),
                  _resident((SWA_HEADS, LANES)),
                  _resident((D_MODEL, _C_END)), _resident((RANK_PAD, GLA_QK)),
                  _resident((1, GLA_QK)), _resident((1, GLA_DV))],
        out_specs=[o[0] for o in outs],
        out_shape=[o[1] for o in outs],
        scratch_shapes=[pltpu.VMEM((tm, GLA_QK), BF16), pltpu.VMEM((tm, GLA_QK), BF16),
                        pltpu.VMEM((tm, GLA_V), BF16), pltpu.VMEM((tm, GLA_V), BF16),
                        pltpu.VMEM((tm, GLA_QK), F32),
                        pltpu.VMEM((GLA_HEADS, GLA_DK, GLA_DV), F32)],
        compiler_params=pltpu.CompilerParams(
            dimension_semantics=("arbitrary",), vmem_limit_bytes=VMEM_LIMIT),
        name="inproj",
    )(x2, pos_row, invf, _rope_spread(), sink_feat, w_in_r, w_up_p, b_dec, gain)


def _gla_stages(first_tile, q_ref, k_ref, v_ref, r_ref, la_ref, g_ref, o_ref, st_ref, *,
                n_blocks):
    @pl.when(first_tile)
    def _():
        st_ref[...] = jnp.zeros_like(st_ref)

    c_ = GLA_BLOCK
    causal = (lax.broadcasted_iota(jnp.int32, (c_, c_), 0)
              >= lax.broadcasted_iota(jnp.int32, (c_, c_), 1))
    tri = jnp.where(causal, 1.0, 0.0).astype(BF16)
    gain = g_ref[...]

    items = [(c, h) for c in range(n_blocks) for h in range(GLA_HEADS)]
    rows = lambda c: slice(c * c_, (c + 1) * c_)
    ks = lambda h: slice(h * GLA_DK, (h + 1) * GLA_DK)
    vs = lambda h: slice(h * GLA_DV, (h + 1) * GLA_DV)

    row_idx = lax.broadcasted_iota(jnp.int32, (c_, GLA_QK), 0)

    def prefix_sum(x):
        shift = 1
        while shift < c_:
            x = x + jnp.where(row_idx >= shift, pltpu.roll(x, shift, 0), 0.0)
            shift *= 2
        return x

    q_in, k_in, k_out, scores, contrib, decay, o_intra = {}, {}, {}, {}, {}, {}, {}

    def operands(c):
        b_c = prefix_sum(la_ref[rows(c), :])
        for h in range(GLA_HEADS):
            bh = b_c[:, ks(h)]
            b_last = bh[c_ - 1:c_, :]
            q = q_ref[rows(c), ks(h)].astype(F32)
            k = k_ref[rows(c), ks(h)].astype(F32)
            q_in[c, h] = (q * jnp.exp2(bh)).astype(BF16)
            k_in[c, h] = (k * jnp.exp2(-bh)).astype(BF16)
            k_out[c, h] = (k * jnp.exp2(b_last - bh)).astype(BF16)
            dec = jnp.exp2(jnp.broadcast_to(b_last, (GLA_DK, GLA_DK)).T)
            decay[c, h] = jnp.concatenate([dec] * (GLA_DV // GLA_DK), axis=1)

    def stage_a(c):
        for h in range(GLA_HEADS):
            scores[c, h] = _dot_nt(q_in[c, h], k_in[c, h])
            contrib[c, h] = _dot_tn(k_out[c, h], v_ref[rows(c), vs(h)])
        if c + 1 < n_blocks:
            operands(c + 1)

    def stage_b(c):
        for h in range(GLA_HEADS):
            masked = jnp.where(causal, scores[c, h], 0.0).astype(BF16)
            o_intra[c, h] = _dot(masked, v_ref[rows(c), vs(h)])

    st = [st_ref[h] for h in range(GLA_HEADS)]

    def stage_c(c):
        for h in range(GLA_HEADS):
            o = o_intra[c, h] + _dot(q_in[c, h], st[h].astype(BF16))
            st[h] = st[h] * decay[c, h] + contrib[c, h]
            o = o * lax.rsqrt(jnp.mean(o * o, axis=-1, keepdims=True) + RMS_EPS)
            r = r_ref[rows(c), vs(h)].astype(F32)
            o = o * gain * (r * jax.nn.sigmoid(r))
            o_ref[rows(c), vs(h)] = o.astype(BF16)

    operands(0)
    for stage in (stage_a, stage_b, stage_c):
        for c in range(n_blocks):
            yield
            stage(c)
    for h in range(GLA_HEADS):
        st_ref[h] = st[h]


def _swa_kernel(q_ref, kc_ref, kp_ref, vec_ref, vep_ref, voc_ref, vop_ref, mask_ref, o_ref,
                *, blocks_per_step):
    w = SWA_WINDOW
    dh = SWA_HEAD_DIM
    per_group = SWA_HEADS // SWA_KV_HEADS
    first = pl.program_id(1) == 0
    row = lax.broadcasted_iota(jnp.int32, (w, LANES), 0)
    lane = lax.broadcasted_iota(jnp.int32, (w, LANES), 1)
    ident = jnp.where(row == lane, 1.0, 0.0).astype(BF16)
    sink_row = row == 0
    low = lane < dh
    zero = jnp.zeros((w, LANES), BF16)
    one_e = jnp.where(lane == dh, 1.0, 0.0).astype(BF16)
    one_o = jnp.where(lane == 0, 1.0, 0.0).astype(BF16)
    order = list(range(0, per_group, 2)) + list(range(1, per_group, 2))

    scores = []
    values = []
    for qb in range(blocks_per_step):
        rows = slice(qb * w, (qb + 1) * w)
        prev_rows = slice((qb - 1) * w, qb * w)
        mask = mask_ref[1] if qb else jnp.where(first, mask_ref[0], mask_ref[1])
        for g in range(SWA_KV_HEADS):
            k_prev = kc_ref[g, prev_rows] if qb else kp_ref[g]
            ve_prev = vec_ref[g, prev_rows] if qb else vep_ref[g]
            vo_prev = voc_ref[g, prev_rows] if qb else vop_ref[g]
            k2 = jnp.concatenate([jnp.where(sink_row, zero, k_prev), kc_ref[g, rows]], axis=0)
            kx = jnp.concatenate([k2, mask], axis=1)
            ve = jnp.concatenate([jnp.where(sink_row, one_e, ve_prev), vec_ref[g, rows]], axis=0)
            vo = jnp.concatenate([jnp.where(sink_row, one_o, vo_prev), voc_ref[g, rows]], axis=0)
            for part, vv in ((order[:per_group // 2], ve), (order[per_group // 2:], vo)):
                qs = jnp.concatenate(
                    [jnp.concatenate([q_ref[g * per_group + h, rows], ident], axis=1)
                     for h in part], axis=0)
                scores.append(_dot_nt(qs, kx))
                values.append(vv)
    pv = []
    for s, vv in zip(scores, values):
        p = jnp.exp2(s - jnp.max(s, axis=-1, keepdims=True)).astype(BF16)
        pv.append(_dot(p, vv))
    for qb in range(blocks_per_step):
        rows = slice(qb * w, (qb + 1) * w)
        for g in range(SWA_KV_HEADS):
            pv_e = pv[(qb * SWA_KV_HEADS + g) * 2]
            pv_o = pv[(qb * SWA_KV_HEADS + g) * 2 + 1]
            for c in range(per_group // 2):
                e = pv_e[c * w:(c + 1) * w]
                o = pv_o[c * w:(c + 1) * w]
                out = jnp.where(low, e * (1.0 / e[:, dh:dh + 1]), o * (1.0 / o[:, 0:1]))
                col = (g * (per_group // 2) + c) * LANES
                o_ref[rows, col:col + LANES] = out.astype(BF16)


def _swa_mask():
    w = SWA_WINDOW
    j = np.arange(2 * w)[:, None]
    i = np.arange(w)[None, :]
    band = (j > i) & (j <= i + w)
    later = band | (j == 0)
    first = (band & (j >= w)) | (j == 0)
    both = np.stack([first, later])
    return jnp.asarray(np.where(both, 0.0, MASK_BIAS), dtype=BF16)


def _swa(qh, kh, ve, vo, batch, seq, blocks_per_step):
    m = kh.shape[1]
    w = SWA_WINDOW
    tq = blocks_per_step * w
    steps = seq // tq
    kv = SWA_KV_HEADS
    cur = lambda n_: pl.BlockSpec((n_, tq, LANES), lambda b, t: (0, b * steps + t, 0))
    prev = lambda n_: pl.BlockSpec(
        (n_, w, LANES),
        lambda b, t: (0, b * (seq // w) + jnp.maximum(t * blocks_per_step - 1, 0), 0))
    return pl.pallas_call(
        functools.partial(_swa_kernel, blocks_per_step=blocks_per_step),
        grid=(batch, steps),
        in_specs=[cur(SWA_HEADS), cur(kv), prev(kv), cur(kv), prev(kv), cur(kv), prev(kv),
                  _resident((2, 2 * w, w))],
        out_specs=pl.BlockSpec((tq, SWA_Q), lambda b, t: (b * steps + t, 0)),
        out_shape=jax.ShapeDtypeStruct((m, SWA_Q), BF16),
        compiler_params=pltpu.CompilerParams(
            dimension_semantics=("arbitrary", "arbitrary"), vmem_limit_bytes=VMEM_LIMIT),
        name="swa",
    )(qh, kh, kh, ve, ve, vo, vo, _swa_mask())


def _layer_norm(y, g, b):
    mu = jnp.mean(y, axis=-1, keepdims=True)
    d = y - mu
    var = jnp.mean(d * d, axis=-1, keepdims=True)
    return d * lax.rsqrt(var + LN_EPS) * g + b


def _tail_kernel(x_ref, oa_ref, ob_ref, ga_ref, gb_ref, wa_ref, wb_ref, wo_ref, g1_ref, b1_ref,
                 wg_ref, wu_ref, wd_ref, g2_ref, b2_ref, o_ref, *, ff_chunk):
    d_ff = wg_ref.shape[1]
    n_chunks = d_ff // ff_chunk
    sl = lambda j: slice(j * ff_chunk, (j + 1) * ff_chunk)
    tm = x_ref.shape[0]
    parts = [slice(i * (tm // TAIL_SPLIT), (i + 1) * (tm // TAIL_SPLIT))
             for i in range(TAIL_SPLIT)]

    def merge(rs):
        y_a = _dot(oa_ref[rs, :], wa_ref[...])
        y_b = _dot(ob_ref[rs, :], wb_ref[...])
        merged = (jax.nn.sigmoid(ga_ref[rs, :].astype(F32)) * y_a
                  + jax.nn.sigmoid(gb_ref[rs, :].astype(F32)) * y_b)
        mix = _dot(merged.astype(BF16), wo_ref[...])
        return _layer_norm(DEEPNORM_ALPHA * x_ref[rs, :] + mix, g1_ref[...], b1_ref[...])

    def ffn(rs, h):
        hb = h.astype(BF16)

        def gate_up(j):
            return _dot(hb, wg_ref[:, sl(j)]), _dot(hb, wu_ref[:, sl(j)])

        acc = None
        gu = gate_up(0)
        for j in range(n_chunks):
            gate, up = gu
            if j + 1 < n_chunks:
                gu = gate_up(j + 1)
            act = (gate * jax.nn.sigmoid(gate) * up).astype(BF16)
            part = _dot(act, wd_ref[sl(j), :])
            acc = part if acc is None else acc + part
        o_ref[rs, :] = _layer_norm(DEEPNORM_ALPHA * h + acc, g2_ref[...], b2_ref[...])

    h_next = merge(parts[0])
    for i, rs in enumerate(parts):
        h = h_next
        if i + 1 < len(parts):
            h_next = merge(parts[i + 1])
        ffn(rs, h)


def _tail(x2, oa, ob, ga, gb, wa, wb, wo, ln1_g, ln1_b, wg, wu, wd, ln2_g, ln2_b, tm, ff_chunk):
    m = x2.shape[0]
    d_ff = wg.shape[1]
    row = pl.BlockSpec((tm, D_MODEL), lambda i: (i, 0))
    sq = _resident((D_MODEL, D_MODEL))
    vec = _resident((1, D_MODEL))
    return pl.pallas_call(
        functools.partial(_tail_kernel, ff_chunk=ff_chunk),
        grid=(m // tm,),
        in_specs=[row] * 5 + [sq, sq, sq, vec, vec,
                              _resident((D_MODEL, d_ff)), _resident((D_MODEL, d_ff)),
                              _resident((d_ff, D_MODEL)), vec, vec],
        out_specs=row,
        out_shape=jax.ShapeDtypeStruct((m, D_MODEL), F32),
        compiler_params=pltpu.CompilerParams(
            dimension_semantics=("arbitrary",), vmem_limit_bytes=VMEM_LIMIT),
        name="tail",
    )(x2, oa, ob, ga, gb, wa, wb, wo, ln1_g, ln1_b, wg, wu, wd, ln2_g, ln2_b)


def _rope_inv_freq():
    inv_freq = ROPE_THETA ** (-jnp.arange(0, ROPE_DIM, 2, dtype=F32) / ROPE_DIM)
    return jnp.broadcast_to(inv_freq[:, None], (ROPE_DIM // 2, LANES))


def _layer(h2, pos_row, invf, batch, seq, w_in, w_decay_up, b_decay, gla_norm_g,
           w_branch_a, w_branch_b, sinks, w_out, ln1_g, ln1_b,
           w_ffn_gate, w_ffn_up, w_ffn_down, ln2_g, ln2_b):
    w_in_r = _win_prep(w_in.T)
    w_up_p = jnp.concatenate(
        [w_decay_up, jnp.zeros((RANK_PAD - GLA_RANK, GLA_QK), w_decay_up.dtype)],
        axis=0).astype(BF16)

    neg_sink = -sinks.astype(F32) * LOG2_E
    sink_hi = neg_sink.astype(BF16).astype(F32)
    sink_feat = jnp.zeros((SWA_HEADS, LANES), F32)
    sink_feat = sink_feat.at[:, SWA_HEAD_DIM].set(sink_hi)
    sink_feat = sink_feat.at[:, SWA_HEAD_DIM + 1].set(neg_sink - sink_hi)

    oa, qh, kh, ve, vo, ga, gb = _inproj(
        h2, pos_row, invf, sink_feat, w_in_r, w_up_p, b_decay[None, :], gla_norm_g[None, :],
        seq, tm=512)
    ob = _swa(qh, kh, ve, vo, batch, seq, blocks_per_step=8)
    return _tail(h2, oa, ob, ga, gb, w_branch_a.astype(BF16), w_branch_b.astype(BF16),
                 w_out.astype(BF16), ln1_g[None, :], ln1_b[None, :],
                 w_ffn_gate.astype(BF16), w_ffn_up.astype(BF16), w_ffn_down.astype(BF16),
                 ln2_g[None, :], ln2_b[None, :], tm=512, ff_chunk=256)


def kernel(x, positions, w_in, w_decay_up, b_decay, gla_norm_g, w_branch_a, w_branch_b, sinks,
           w_out, ln1_g, ln1_b, w_ffn_gate, w_ffn_up, w_ffn_down, ln2_g, ln2_b):
    batch, seq, d = x.shape
    m = batch * seq
    h2 = x.reshape(m, d)
    pos_row = positions.reshape(1, m).astype(F32)
    invf = _rope_inv_freq()
    for layer in range(w_in.shape[0]):
        h2 = _layer(h2, pos_row, invf, batch, seq, w_in[layer], w_decay_up[layer], b_decay[layer],
                    gla_norm_g[layer], w_branch_a[layer], w_branch_b[layer], sinks[layer],
                    w_out[layer], ln1_g[layer], ln1_b[layer], w_ffn_gate[layer],
                    w_ffn_up[layer], w_ffn_down[layer], ln2_g[layer], ln2_b[layer])
    return h2.reshape(batch, seq, d)
```

```python
import functools

import math

import jax
import jax.numpy as jnp
import numpy as np
from jax import lax
from jax.experimental import pallas as pl
from jax.experimental.pallas import tpu as pltpu

D_MODEL = 1024
GLA_HEADS = 4
GLA_DK = 128
GLA_DV = 256
GLA_RANK = 16
GLA_TAU = 16.0
GLA_CHUNK = 64
GLA_BLOCK = GLA_CHUNK
SWA_HEADS = 16
SWA_KV_HEADS = 2
SWA_HEAD_DIM = 64
SWA_WINDOW = 128
ROPE_DIM = 16
ROPE_THETA = 500000.0
DEPTH = 1
DEEPNORM_ALPHA = (2.0 * DEPTH) ** 0.25
LN_EPS = 1e-5
RMS_EPS = 1e-6

GLA_QK = GLA_HEADS * GLA_DK
GLA_V = GLA_HEADS * GLA_DV
SWA_Q = SWA_HEADS * SWA_HEAD_DIM
SWA_KV = SWA_KV_HEADS * SWA_HEAD_DIM

LANES = 128
MXU_COLS = 256
RANK_PAD = LANES
VMEM_LIMIT = 56 * 1024 * 1024
TAIL_SPLIT = 2
LOG2_E = math.log2(math.e)
MASK_BIAS = -1e30

_C_QA = 0
_C_KA = _C_QA + GLA_QK
_C_VA = _C_KA + GLA_QK
_C_RA = _C_VA + GLA_V
_C_QB = _C_RA + GLA_V
_C_KV = _C_QB + SWA_Q
_C_GA = _C_KV + 2 * SWA_KV
_C_GB = _C_GA + D_MODEL
_C_DL = _C_GB + D_MODEL
_C_END = _C_DL + RANK_PAD
_D_IN = 2 * GLA_QK + 2 * GLA_V + GLA_RANK + SWA_Q + 2 * SWA_KV + 2 * D_MODEL
_O_DL = 2 * GLA_QK + 2 * GLA_V

F32 = jnp.float32
BF16 = jnp.bfloat16


def _dot(a, b):
    return jnp.dot(a, b, preferred_element_type=F32)


def _dot_nt(a, b):
    return lax.dot_general(a, b, (((1,), (1,)), ((), ())), preferred_element_type=F32)


def _dot_tn(a, b):
    return lax.dot_general(a, b, (((0,), (0,)), ((), ())), preferred_element_type=F32)


def _resident(shape):
    return pl.BlockSpec(shape, lambda *_: (0,) * len(shape), pipeline_mode=pl.Buffered(1))


def _win_prep_kernel(wt_ref, o_ref):
    def put(col, row):
        o_ref[:, col:col + LANES] = wt_ref[row:row + LANES, :].T.astype(BF16)

    for j in range(_O_DL // LANES):
        put(j * LANES, j * LANES)
    for j in range((_C_DL - _O_DL) // LANES):
        put(_O_DL + j * LANES, _O_DL + GLA_RANK + j * LANES)
    blk = wt_ref[_O_DL:_O_DL + LANES, :].T
    lane = lax.broadcasted_iota(jnp.int32, blk.shape, 1)
    o_ref[:, _C_DL:] = jnp.where(lane < GLA_RANK, blk, 0.0).astype(BF16)


def _win_prep(w_in_t):
    cb = LANES
    assert _C_DL - _O_DL == _D_IN - _O_DL - GLA_RANK and (_C_DL - _O_DL) % LANES == 0
    return pl.pallas_call(
        _win_prep_kernel,
        grid=(D_MODEL // cb,),
        in_specs=[pl.BlockSpec((_D_IN, cb), lambda i: (0, i))],
        out_specs=pl.BlockSpec((cb, _C_END), lambda i: (i, 0)),
        out_shape=jax.ShapeDtypeStruct((D_MODEL, _C_END), BF16),
        compiler_params=pltpu.CompilerParams(dimension_semantics=("arbitrary",)),
        name="win_prep",
    )(w_in_t)


def _inproj_kernel(x_ref, pos_ref, invf_ref, spread_ref, sf_ref, w_ref, wup_ref, bdec_ref,
                   gain_ref,
                   oa_ref, qh_ref, kh_ref, ve_ref, vo_ref, ga_ref, gb_ref,
                   qa_ref, ka_ref, va_ref, ra_ref, la_ref, st_ref, *, tiles_per_seq):
    xb = x_ref[...].astype(BF16)

    def proj(lo, hi):
        return _dot(xb, w_ref[:, lo:hi])

    tm = x_ref.shape[0]
    half = ROPE_DIM // 2
    ang = jnp.tile(invf_ref[...], (1, tm // LANES)) * pos_ref[...]
    one_row = jnp.where(lax.broadcasted_iota(jnp.int32, (2 * half, tm), 0) == 0, 1.0, 0.0)
    cs = jnp.concatenate([jnp.cos(ang), jnp.sin(ang), one_row], axis=0)
    cs_hi = cs.astype(BF16)
    cs_lo = (cs - cs_hi.astype(F32)).astype(BF16)
    tabs = _dot_tn(jnp.concatenate([cs_hi, cs_lo], axis=0), spread_ref[...])
    cos, a_tab, b_tab = tabs[:, :LANES], tabs[:, LANES:2 * LANES], tabs[:, 2 * LANES:]

    def rope(t):
        return (t * cos + pltpu.roll(t, LANES - half, 1) * a_tab
                + pltpu.roll(t, half, 1) * b_tab)

    dh = SWA_HEAD_DIM
    lane_full = lax.broadcasted_iota(jnp.int32, (tm, LANES), 1)
    low = lane_full < dh
    q_scale = (dh ** -0.5) * LOG2_E

    def swa_queries(i):
        qb = proj(_C_QB + i * MXU_COLS, _C_QB + (i + 1) * MXU_COLS)
        for c in range(2 * i, 2 * i + 2):
            t = rope(qb[:, (c - 2 * i) * LANES:(c - 2 * i + 1) * LANES]) * q_scale
            qh_ref[2 * c] = jnp.where(low, t, sf_ref[2 * c:2 * c + 1, :]).astype(BF16)
            qh_ref[2 * c + 1] = jnp.where(
                low, pltpu.roll(t, dh, 1), sf_ref[2 * c + 1:2 * c + 2, :]).astype(BF16)

    def swa_keys_values_and_decay():
        kv = proj(_C_KV, _C_GA)
        kk = rope(kv[:, :SWA_KV])
        vv = kv[:, SWA_KV:]
        kk_sw = pltpu.roll(kk, dh, 1)
        vv_sw = pltpu.roll(vv, dh, 1)
        one_k = jnp.where(lane_full < dh + 2, 1.0, 0.0)
        one_e = jnp.where(lane_full == dh, 1.0, 0.0)
        one_o = jnp.where(lane_full == 0, 1.0, 0.0)
        kh_ref[0] = jnp.where(low, kk, one_k).astype(BF16)
        kh_ref[1] = jnp.where(low, kk_sw, one_k).astype(BF16)
        ve_ref[0] = jnp.where(low, vv, one_e).astype(BF16)
        ve_ref[1] = jnp.where(low, vv_sw, one_e).astype(BF16)
        vo_ref[0] = jnp.where(low, one_o, vv_sw).astype(BF16)
        vo_ref[1] = jnp.where(low, one_o, vv).astype(BF16)
        return proj(_C_DL, _C_END).astype(BF16)

    def log_decay(d_low, h):
        cols = slice(h * GLA_DK, (h + 1) * GLA_DK)
        z = _dot(d_low, wup_ref[:, cols]) + bdec_ref[:, cols]
        log_sig = jnp.minimum(z, 0.0) - jnp.log1p(jnp.exp(-jnp.abs(z)))
        la_ref[:, cols] = log_sig * (LOG2_E / GLA_TAU)

    def plain(ref, col0, i, scale=None):
        y = proj(col0 + i * MXU_COLS, col0 + (i + 1) * MXU_COLS)
        if scale is not None:
            y = y * scale
        ref[:, i * MXU_COLS:(i + 1) * MXU_COLS] = y.astype(BF16)

    gla_tiles = (
        [(qa_ref, _C_QA, i, GLA_DK ** -0.5) for i in range(GLA_QK // MXU_COLS)]
        + [(ka_ref, _C_KA, i, None) for i in range(GLA_QK // MXU_COLS)]
        + [(va_ref, _C_VA, i, None) for i in range(GLA_V // MXU_COLS)]
        + [(ra_ref, _C_RA, i, None) for i in range(GLA_V // MXU_COLS)])
    d_low = swa_keys_values_and_decay()
    per_head = len(gla_tiles) // GLA_HEADS
    for h in range(GLA_HEADS):
        log_decay(d_low, h)
        for args in gla_tiles[h * per_head:(h + 1) * per_head]:
            plain(*args)

    first_tile = pl.program_id(0) % tiles_per_seq == 0
    gla = _gla_stages(first_tile, qa_ref, ka_ref, va_ref, ra_ref, la_ref, gain_ref, oa_ref,
                      st_ref, n_blocks=tm // GLA_BLOCK)
    rest = []
    for i in range(SWA_Q // MXU_COLS):
        rest.append(functools.partial(swa_queries, i))
        rest.append(functools.partial(plain, ga_ref, _C_GA, i))
    rest += [functools.partial(plain, gb_ref, _C_GB, i) for i in range(D_MODEL // MXU_COLS)]
    n_gla_steps = 3 * (tm // GLA_BLOCK) + 1
    slots = len(rest) - 1
    done = 0
    for n, task in enumerate(rest):
        target = min(n_gla_steps, -(-n_gla_steps * (n + 1) // slots))
        for _ in range(target - done):
            next(gla, None)
        done = target
        task()
    for _ in gla:
        pass


def _rope_spread():
    half = ROPE_DIM // 2
    rows = 4 * half
    e = np.zeros((rows, 3 * LANES), np.float32)
    for lane in range(LANES):
        j = lane % SWA_HEAD_DIM
        if j < half:
            e[j, lane] = 1.0
            e[half + j, LANES + lane] = -1.0
        elif j < 2 * half:
            e[j - half, lane] = 1.0
            e[j, 2 * LANES + lane] = 1.0
        else:
            e[2 * half, lane] = 1.0
    return jnp.asarray(np.concatenate([e, e], axis=0), dtype=BF16)


def _inproj(x2, pos_row, invf, sink_feat, w_in_r, w_up_p, b_dec, gain, seq, tm):
    m = x2.shape[0]
    half = ROPE_DIM // 2
    row = lambda c: pl.BlockSpec((tm, c), lambda i: (i, 0))
    row_out = lambda c, d: (row(c), jax.ShapeDtypeStruct((m, c), d))
    slab_out = lambda n: (pl.BlockSpec((n, tm, LANES), lambda i: (0, i, 0)),
                          jax.ShapeDtypeStruct((n, m, LANES), BF16))
    outs = [row_out(GLA_V, BF16),
            slab_out(SWA_HEADS), slab_out(SWA_KV_HEADS), slab_out(SWA_KV_HEADS),
            slab_out(SWA_KV_HEADS), row_out(D_MODEL, BF16), row_out(D_MODEL, BF16)]
    return pl.pallas_call(
        functools.partial(_inproj_kernel, tiles_per_seq=seq // tm),
        grid=(m // tm,),
        in_specs=[row(D_MODEL), pl.BlockSpec((1, tm), lambda i: (0, i)),
                  _resident((half, LANES)), _resident((8 * half, 3 * LANES)),
                  _resident((SWA_HEADS, LANES)),
                  _resident((D_MODEL, _C_END)), _resident((RANK_PAD, GLA_QK)),
                  _resident((1, GLA_QK)), _resident((1, GLA_DV))],
        out_specs=[o[0] for o in outs],
        out_shape=[o[1] for o in outs],
        scratch_shapes=[pltpu.VMEM((tm, GLA_QK), BF16), pltpu.VMEM((tm, GLA_QK), BF16),
                        pltpu.VMEM((tm, GLA_V), BF16), pltpu.VMEM((tm, GLA_V), BF16),
                        pltpu.VMEM((tm, GLA_QK), F32),
                        pltpu.VMEM((GLA_HEADS, GLA_DK, GLA_DV), F32)],
        compiler_params=pltpu.CompilerParams(
            dimension_semantics=("arbitrary",), vmem_limit_bytes=VMEM_LIMIT),
        name="inproj",
    )(x2, pos_row, invf, _rope_spread(), sink_feat, w_in_r, w_up_p, b_dec, gain)


def _gla_stages(first_tile, q_ref, k_ref, v_ref, r_ref, la_ref, g_ref, o_ref, st_ref, *,
                n_blocks):
    @pl.when(first_tile)
    def _():
        st_ref[...] = jnp.zeros_like(st_ref)

    c_ = GLA_BLOCK
    causal = (lax.broadcasted_iota(jnp.int32, (c_, c_), 0)
              >= lax.broadcasted_iota(jnp.int32, (c_, c_), 1))
    tri = jnp.where(causal, 1.0, 0.0).astype(BF16)
    gain = g_ref[...]

    items = [(c, h) for c in range(n_blocks) for h in range(GLA_HEADS)]
    rows = lambda c: slice(c * c_, (c + 1) * c_)
    ks = lambda h: slice(h * GLA_DK, (h + 1) * GLA_DK)
    vs = lambda h: slice(h * GLA_DV, (h + 1) * GLA_DV)

    b = []
    for c in range(n_blocks):
        la = la_ref[rows(c), :]
        hi = la.astype(BF16)
        rem = la - hi.astype(F32)
        mid = rem.astype(BF16)
        lo = (rem - mid.astype(F32)).astype(BF16)
        b.append(_dot(tri, hi) + _dot(tri, mid) + _dot(tri, lo))

    q_in, scores, contrib, decay, o_intra = {}, {}, {}, {}, {}

    def stage_a(c):
        for h in range(GLA_HEADS):
            bh = b[c][:, ks(h)]
            b_last = bh[c_ - 1:c_, :]
            q = q_ref[rows(c), ks(h)].astype(F32)
            k = k_ref[rows(c), ks(h)].astype(F32)
            q_in[c, h] = (q * jnp.exp2(bh)).astype(BF16)
            k_in = (k * jnp.exp2(-bh)).astype(BF16)
            k_out = (k * jnp.exp2(b_last - bh)).astype(BF16)
            scores[c, h] = _dot_nt(q_in[c, h], k_in)
            contrib[c, h] = _dot_tn(k_out, v_ref[rows(c), vs(h)])
            dec = jnp.exp2(jnp.broadcast_to(b_last, (GLA_DK, GLA_DK)).T)
            decay[c, h] = jnp.concatenate([dec] * (GLA_DV // GLA_DK), axis=1)

    def stage_b(c):
        for h in range(GLA_HEADS):
            masked = jnp.where(causal, scores[c, h], 0.0).astype(BF16)
            o_intra[c, h] = _dot(masked, v_ref[rows(c), vs(h)])

    st = [st_ref[h] for h in range(GLA_HEADS)]

    def stage_c(c):
        for h in range(GLA_HEADS):
            o = o_intra[c, h] + _dot(q_in[c, h], st[h].astype(BF16))
            st[h] = st[h] * decay[c, h] + contrib[c, h]
            o = o * lax.rsqrt(jnp.mean(o * o, axis=-1, keepdims=True) + RMS_EPS)
            r = r_ref[rows(c), vs(h)].astype(F32)
            o = o * gain * (r * jax.nn.sigmoid(r))
            o_ref[rows(c), vs(h)] = o.astype(BF16)

    for stage in (stage_a, stage_b, stage_c):
        for c in range(n_blocks):
            yield
            stage(c)
    for h in range(GLA_HEADS):
        st_ref[h] = st[h]


def _swa_kernel(q_ref, kc_ref, kp_ref, vec_ref, vep_ref, voc_ref, vop_ref, mask_ref, o_ref,
                *, blocks_per_step):
    w = SWA_WINDOW
    dh = SWA_HEAD_DIM
    per_group = SWA_HEADS // SWA_KV_HEADS
    first = pl.program_id(1) == 0
    row = lax.broadcasted_iota(jnp.int32, (w, LANES), 0)
    lane = lax.broadcasted_iota(jnp.int32, (w, LANES), 1)
    ident = jnp.where(row == lane, 1.0, 0.0).astype(BF16)
    sink_row = row == 0
    low = lane < dh
    zero = jnp.zeros((w, LANES), BF16)
    one_e = jnp.where(lane == dh, 1.0, 0.0).astype(BF16)
    one_o = jnp.where(lane == 0, 1.0, 0.0).astype(BF16)
    order = list(range(0, per_group, 2)) + list(range(1, per_group, 2))

    scores = []
    values = []
    for qb in range(blocks_per_step):
        rows = slice(qb * w, (qb + 1) * w)
        prev_rows = slice((qb - 1) * w, qb * w)
        mask = mask_ref[1] if qb else jnp.where(first, mask_ref[0], mask_ref[1])
        for g in range(SWA_KV_HEADS):
            k_prev = kc_ref[g, prev_rows] if qb else kp_ref[g]
            ve_prev = vec_ref[g, prev_rows] if qb else vep_ref[g]
            vo_prev = voc_ref[g, prev_rows] if qb else vop_ref[g]
            k2 = jnp.concatenate([jnp.where(sink_row, zero, k_prev), kc_ref[g, rows]], axis=0)
            kx = jnp.concatenate([k2, mask], axis=1)
            ve = jnp.concatenate([jnp.where(sink_row, one_e, ve_prev), vec_ref[g, rows]], axis=0)
            vo = jnp.concatenate([jnp.where(sink_row, one_o, vo_prev), voc_ref[g, rows]], axis=0)
            for part, vv in ((order[:per_group // 2], ve), (order[per_group // 2:], vo)):
                qs = jnp.concatenate(
                    [jnp.concatenate([q_ref[g * per_group + h, rows], ident], axis=1)
                     for h in part], axis=0)
                scores.append(_dot_nt(qs, kx))
                values.append(vv)
    pv = []
    for s, vv in zip(scores, values):
        p = jnp.exp2(s - jnp.max(s, axis=-1, keepdims=True)).astype(BF16)
        pv.append(_dot(p, vv))
    for qb in range(blocks_per_step):
        rows = slice(qb * w, (qb + 1) * w)
        for g in range(SWA_KV_HEADS):
            pv_e = pv[(qb * SWA_KV_HEADS + g) * 2]
            pv_o = pv[(qb * SWA_KV_HEADS + g) * 2 + 1]
            for c in range(per_group // 2):
                e = pv_e[c * w:(c + 1) * w]
                o = pv_o[c * w:(c + 1) * w]
                out = jnp.where(low, e * (1.0 / e[:, dh:dh + 1]), o * (1.0 / o[:, 0:1]))
                col = (g * (per_group // 2) + c) * LANES
                o_ref[rows, col:col + LANES] = out.astype(BF16)


def _swa_mask():
    w = SWA_WINDOW
    j = np.arange(2 * w)[:, None]
    i = np.arange(w)[None, :]
    band = (j > i) & (j <= i + w)
    later = band | (j == 0)
    first = (band & (j >= w)) | (j == 0)
    both = np.stack([first, later])
    return jnp.asarray(np.where(both, 0.0, MASK_BIAS), dtype=BF16)


def _swa(qh, kh, ve, vo, batch, seq, blocks_per_step):
    m = kh.shape[1]
    w = SWA_WINDOW
    tq = blocks_per_step * w
    steps = seq // tq
    kv = SWA_KV_HEADS
    cur = lambda n_: pl.BlockSpec((n_, tq, LANES), lambda b, t: (0, b * steps + t, 0))
    prev = lambda n_: pl.BlockSpec(
        (n_, w, LANES),
        lambda b, t: (0, b * (seq // w) + jnp.maximum(t * blocks_per_step - 1, 0), 0))
    return pl.pallas_call(
        functools.partial(_swa_kernel, blocks_per_step=blocks_per_step),
        grid=(batch, steps),
        in_specs=[cur(SWA_HEADS), cur(kv), prev(kv), cur(kv), prev(kv), cur(kv), prev(kv),
                  _resident((2, 2 * w, w))],
        out_specs=pl.BlockSpec((tq, SWA_Q), lambda b, t: (b * steps + t, 0)),
        out_shape=jax.ShapeDtypeStruct((m, SWA_Q), BF16),
        compiler_params=pltpu.CompilerParams(
            dimension_semantics=("arbitrary", "arbitrary"), vmem_limit_bytes=VMEM_LIMIT),
        name="swa",
    )(qh, kh, kh, ve, ve, vo, vo, _swa_mask())


def _layer_norm(y, g, b):
    mu = jnp.mean(y, axis=-1, keepdims=True)
    d = y - mu
    var = jnp.mean(d * d, axis=-1, keepdims=True)
    return d * lax.rsqrt(var + LN_EPS) * g + b


def _tail_kernel(x_ref, oa_ref, ob_ref, ga_ref, gb_ref, wa_ref, wb_ref, wo_ref, g1_ref, b1_ref,
                 wg_ref, wu_ref, wd_ref, g2_ref, b2_ref, o_ref, *, ff_chunk):
    d_ff = wg_ref.shape[1]
    n_chunks = d_ff // ff_chunk
    sl = lambda j: slice(j * ff_chunk, (j + 1) * ff_chunk)
    tm = x_ref.shape[0]
    parts = [slice(i * (tm // TAIL_SPLIT), (i + 1) * (tm // TAIL_SPLIT))
             for i in range(TAIL_SPLIT)]

    def merge(rs):
        y_a = _dot(oa_ref[rs, :], wa_ref[...])
        y_b = _dot(ob_ref[rs, :], wb_ref[...])
        merged = (jax.nn.sigmoid(ga_ref[rs, :].astype(F32)) * y_a
                  + jax.nn.sigmoid(gb_ref[rs, :].astype(F32)) * y_b)
        mix = _dot(merged.astype(BF16), wo_ref[...])
        return _layer_norm(DEEPNORM_ALPHA * x_ref[rs, :] + mix, g1_ref[...], b1_ref[...])

    def ffn(rs, h):
        hb = h.astype(BF16)

        def gate_up(j):
            return _dot(hb, wg_ref[:, sl(j)]), _dot(hb, wu_ref[:, sl(j)])

        acc = None
        gu = gate_up(0)
        for j in range(n_chunks):
            gate, up = gu
            if j + 1 < n_chunks:
                gu = gate_up(j + 1)
            act = (gate * jax.nn.sigmoid(gate) * up).astype(BF16)
            part = _dot(act, wd_ref[sl(j), :])
            acc = part if acc is None else acc + part
        o_ref[rs, :] = _layer_norm(DEEPNORM_ALPHA * h + acc, g2_ref[...], b2_ref[...])

    h_next = merge(parts[0])
    for i, rs in enumerate(parts):
        h = h_next
        if i + 1 < len(parts):
            h_next = merge(parts[i + 1])
        ffn(rs, h)


def _tail(x2, oa, ob, ga, gb, wa, wb, wo, ln1_g, ln1_b, wg, wu, wd, ln2_g, ln2_b, tm, ff_chunk):
    m = x2.shape[0]
    d_ff = wg.shape[1]
    row = pl.BlockSpec((tm, D_MODEL), lambda i: (i, 0))
    sq = _resident((D_MODEL, D_MODEL))
    vec = _resident((1, D_MODEL))
    return pl.pallas_call(
        functools.partial(_tail_kernel, ff_chunk=ff_chunk),
        grid=(m // tm,),
        in_specs=[row] * 5 + [sq, sq, sq, vec, vec,
                              _resident((D_MODEL, d_ff)), _resident((D_MODEL, d_ff)),
                              _resident((d_ff, D_MODEL)), vec, vec],
        out_specs=row,
        out_shape=jax.ShapeDtypeStruct((m, D_MODEL), F32),
        compiler_params=pltpu.CompilerParams(
            dimension_semantics=("arbitrary",), vmem_limit_bytes=VMEM_LIMIT),
        name="tail",
    )(x2, oa, ob, ga, gb, wa, wb, wo, ln1_g, ln1_b, wg, wu, wd, ln2_g, ln2_b)


def _rope_inv_freq():
    inv_freq = ROPE_THETA ** (-jnp.arange(0, ROPE_DIM, 2, dtype=F32) / ROPE_DIM)
    return jnp.broadcast_to(inv_freq[:, None], (ROPE_DIM // 2, LANES))


def _layer(h2, pos_row, invf, batch, seq, w_in, w_decay_up, b_decay, gla_norm_g,
           w_branch_a, w_branch_b, sinks, w_out, ln1_g, ln1_b,
           w_ffn_gate, w_ffn_up, w_ffn_down, ln2_g, ln2_b):
    w_in_r = _win_prep(w_in.T)
    w_up_p = jnp.concatenate(
        [w_decay_up, jnp.zeros((RANK_PAD - GLA_RANK, GLA_QK), w_decay_up.dtype)],
        axis=0).astype(BF16)

    neg_sink = -sinks.astype(F32) * LOG2_E
    sink_hi = neg_sink.astype(BF16).astype(F32)
    sink_feat = jnp.zeros((SWA_HEADS, LANES), F32)
    sink_feat = sink_feat.at[:, SWA_HEAD_DIM].set(sink_hi)
    sink_feat = sink_feat.at[:, SWA_HEAD_DIM + 1].set(neg_sink - sink_hi)

    oa, qh, kh, ve, vo, ga, gb = _inproj(
        h2, pos_row, invf, sink_feat, w_in_r, w_up_p, b_decay[None, :], gla_norm_g[None, :],
        seq, tm=512)
    ob = _swa(qh, kh, ve, vo, batch, seq, blocks_per_step=8)
    return _tail(h2, oa, ob, ga, gb, w_branch_a.astype(BF16), w_branch_b.astype(BF16),
                 w_out.astype(BF16), ln1_g[None, :], ln1_b[None, :],
                 w_ffn_gate.astype(BF16), w_ffn_up.astype(BF16), w_ffn_down.astype(BF16),
                 ln2_g[None, :], ln2_b[None, :], tm=512, ff_chunk=256)


def kernel(x, positions, w_in, w_decay_up, b_decay, gla_norm_g, w_branch_a, w_branch_b, sinks,
           w_out, ln1_g, ln1_b, w_ffn_gate, w_ffn_up, w_ffn_down, ln2_g, ln2_b):
    batch, seq, d = x.shape
    m = batch * seq
    h2 = x.reshape(m, d)
    pos_row = positions.reshape(1, m).astype(F32)
    invf = _rope_inv_freq()
    for layer in range(w_in.shape[0]):
        h2 = _layer(h2, pos_row, invf, batch, seq, w_in[layer], w_decay_up[layer], b_decay[layer],
                    gla_norm_g[layer], w_branch_a[layer], w_branch_b[layer], sinks[layer],
                    w_out[layer], ln1_g[layer], ln1_b[layer], w_ffn_gate[layer],
                    w_ffn_up[layer], w_ffn_down[layer], ln2_g[layer], ln2_b[layer])
    return h2.reshape(batch, seq, d)
```

```python
import functools

import math

import jax
import jax.numpy as jnp
import numpy as np
from jax import lax
from jax.experimental import pallas as pl
from jax.experimental.pallas import tpu as pltpu

D_MODEL = 1024
GLA_HEADS = 4
GLA_DK = 128
GLA_DV = 256
GLA_RANK = 16
GLA_TAU = 16.0
GLA_CHUNK = 64
GLA_BLOCK = GLA_CHUNK
SWA_HEADS = 16
SWA_KV_HEADS = 2
SWA_HEAD_DIM = 64
SWA_WINDOW = 128
ROPE_DIM = 16
ROPE_THETA = 500000.0
DEPTH = 1
DEEPNORM_ALPHA = (2.0 * DEPTH) ** 0.25
LN_EPS = 1e-5
RMS_EPS = 1e-6

GLA_QK = GLA_HEADS * GLA_DK
GLA_V = GLA_HEADS * GLA_DV
SWA_Q = SWA_HEADS * SWA_HEAD_DIM
SWA_KV = SWA_KV_HEADS * SWA_HEAD_DIM

LANES = 128
MXU_COLS = 256
RANK_PAD = LANES
VMEM_LIMIT = 56 * 1024 * 1024
TAIL_SPLIT = 2
LOG2_E = math.log2(math.e)
MASK_BIAS = -1e30

_C_QA = 0
_C_KA = _C_QA + GLA_QK
_C_VA = _C_KA + GLA_QK
_C_RA = _C_VA + GLA_V
_C_QB = _C_RA + GLA_V
_C_KV = _C_QB + SWA_Q
_C_GA = _C_KV + 2 * SWA_KV
_C_GB = _C_GA + D_MODEL
_C_DL = _C_GB + D_MODEL
_C_END = _C_DL + RANK_PAD
_D_IN = 2 * GLA_QK + 2 * GLA_V + GLA_RANK + SWA_Q + 2 * SWA_KV + 2 * D_MODEL
_O_DL = 2 * GLA_QK + 2 * GLA_V

F32 = jnp.float32
BF16 = jnp.bfloat16


def _dot(a, b):
    return jnp.dot(a, b, preferred_element_type=F32)


def _dot_nt(a, b):
    return lax.dot_general(a, b, (((1,), (1,)), ((), ())), preferred_element_type=F32)


def _dot_tn(a, b):
    return lax.dot_general(a, b, (((0,), (0,)), ((), ())), preferred_element_type=F32)


def _resident(shape):
    return pl.BlockSpec(shape, lambda *_: (0,) * len(shape), pipeline_mode=pl.Buffered(1))


def _win_prep_kernel(wt_ref, o_ref):
    def put(col, row):
        o_ref[:, col:col + LANES] = wt_ref[row:row + LANES, :].T.astype(BF16)

    for j in range(_O_DL // LANES):
        put(j * LANES, j * LANES)
    for j in range((_C_DL - _O_DL) // LANES):
        put(_O_DL + j * LANES, _O_DL + GLA_RANK + j * LANES)
    blk = wt_ref[_O_DL:_O_DL + LANES, :].T
    lane = lax.broadcasted_iota(jnp.int32, blk.shape, 1)
    o_ref[:, _C_DL:] = jnp.where(lane < GLA_RANK, blk, 0.0).astype(BF16)


def _win_prep(w_in_t):
    cb = LANES
    assert _C_DL - _O_DL == _D_IN - _O_DL - GLA_RANK and (_C_DL - _O_DL) % LANES == 0
    return pl.pallas_call(
        _win_prep_kernel,
        grid=(D_MODEL // cb,),
        in_specs=[pl.BlockSpec((_D_IN, cb), lambda i: (0, i))],
        out_specs=pl.BlockSpec((cb, _C_END), lambda i: (i, 0)),
        out_shape=jax.ShapeDtypeStruct((D_MODEL, _C_END), BF16),
        compiler_params=pltpu.CompilerParams(dimension_semantics=("arbitrary",)),
        name="win_prep",
    )(w_in_t)


def _inproj_kernel(x_ref, pos_ref, invf_ref, spread_ref, sf_ref, w_ref, wup_ref, bdec_ref,
                   gain_ref,
                   oa_ref, qh_ref, kh_ref, ve_ref, vo_ref, ga_ref, gb_ref,
                   qa_ref, ka_ref, va_ref, ra_ref, la_ref, st_ref, *, tiles_per_seq):
    xb = x_ref[...].astype(BF16)

    def proj(lo, hi):
        return _dot(xb, w_ref[:, lo:hi])

    tm = x_ref.shape[0]
    half = ROPE_DIM // 2
    ang = jnp.tile(invf_ref[...], (1, tm // LANES)) * pos_ref[...]
    one_row = jnp.where(lax.broadcasted_iota(jnp.int32, (2 * half, tm), 0) == 0, 1.0, 0.0)
    cs = jnp.concatenate([jnp.cos(ang), jnp.sin(ang), one_row], axis=0)
    cs_hi = cs.astype(BF16)
    cs_lo = (cs - cs_hi.astype(F32)).astype(BF16)
    tabs = _dot_tn(jnp.concatenate([cs_hi, cs_lo], axis=0), spread_ref[...])
    cos, a_tab, b_tab = tabs[:, :LANES], tabs[:, LANES:2 * LANES], tabs[:, 2 * LANES:]

    def rope(t):
        return (t * cos + pltpu.roll(t, LANES - half, 1) * a_tab
                + pltpu.roll(t, half, 1) * b_tab)

    dh = SWA_HEAD_DIM
    lane_full = lax.broadcasted_iota(jnp.int32, (tm, LANES), 1)
    low = lane_full < dh
    q_scale = (dh ** -0.5) * LOG2_E

    def swa_queries(i):
        qb = proj(_C_QB + i * MXU_COLS, _C_QB + (i + 1) * MXU_COLS)
        for c in range(2 * i, 2 * i + 2):
            t = rope(qb[:, (c - 2 * i) * LANES:(c - 2 * i + 1) * LANES]) * q_scale
            qh_ref[2 * c] = jnp.where(low, t, sf_ref[2 * c:2 * c + 1, :]).astype(BF16)
            qh_ref[2 * c + 1] = jnp.where(
                low, pltpu.roll(t, dh, 1), sf_ref[2 * c + 1:2 * c + 2, :]).astype(BF16)

    def swa_keys_values_and_decay():
        kv = proj(_C_KV, _C_GA)
        kk = rope(kv[:, :SWA_KV])
        vv = kv[:, SWA_KV:]
        kk_sw = pltpu.roll(kk, dh, 1)
        vv_sw = pltpu.roll(vv, dh, 1)
        one_k = jnp.where(lane_full < dh + 2, 1.0, 0.0)
        one_e = jnp.where(lane_full == dh, 1.0, 0.0)
        one_o = jnp.where(lane_full == 0, 1.0, 0.0)
        kh_ref[0] = jnp.where(low, kk, one_k).astype(BF16)
        kh_ref[1] = jnp.where(low, kk_sw, one_k).astype(BF16)
        ve_ref[0] = jnp.where(low, vv, one_e).astype(BF16)
        ve_ref[1] = jnp.where(low, vv_sw, one_e).astype(BF16)
        vo_ref[0] = jnp.where(low, one_o, vv_sw).astype(BF16)
        vo_ref[1] = jnp.where(low, one_o, vv).astype(BF16)
        return proj(_C_DL, _C_END).astype(BF16)

    def log_decay(d_low, h):
        cols = slice(h * GLA_DK, (h + 1) * GLA_DK)
        z = _dot(d_low, wup_ref[:, cols]) + bdec_ref[:, cols]
        log_sig = jnp.minimum(z, 0.0) - jnp.log1p(jnp.exp(-jnp.abs(z)))
        la_ref[:, cols] = log_sig * (LOG2_E / GLA_TAU)

    def plain(ref, col0, i, scale=None):
        y = proj(col0 + i * MXU_COLS, col0 + (i + 1) * MXU_COLS)
        if scale is not None:
            y = y * scale
        ref[:, i * MXU_COLS:(i + 1) * MXU_COLS] = y.astype(BF16)

    gla_tiles = (
        [(qa_ref, _C_QA, i, GLA_DK ** -0.5) for i in range(GLA_QK // MXU_COLS)]
        + [(ka_ref, _C_KA, i, None) for i in range(GLA_QK // MXU_COLS)]
        + [(va_ref, _C_VA, i, None) for i in range(GLA_V // MXU_COLS)]
        + [(ra_ref, _C_RA, i, None) for i in range(GLA_V // MXU_COLS)])
    d_low = swa_keys_values_and_decay()
    per_head = len(gla_tiles) // GLA_HEADS
    for h in range(GLA_HEADS):
        log_decay(d_low, h)
        for args in gla_tiles[h * per_head:(h + 1) * per_head]:
            plain(*args)

    first_tile = pl.program_id(0) % tiles_per_seq == 0
    gla = _gla_stages(first_tile, qa_ref, ka_ref, va_ref, ra_ref, la_ref, gain_ref, oa_ref,
                      st_ref, n_blocks=tm // GLA_BLOCK)
    rest = []
    for i in range(SWA_Q // MXU_COLS):
        rest.append(functools.partial(swa_queries, i))
        rest.append(functools.partial(plain, ga_ref, _C_GA, i))
    rest += [functools.partial(plain, gb_ref, _C_GB, i) for i in range(D_MODEL // MXU_COLS)]
    n_blocks = tm // GLA_BLOCK
    assert len(rest) == 12 and n_blocks == 8
    steps_before = [1, 2, 2, 2, 2, 1, 1, 1, 1, 2, 2, 0]
    assert sum(steps_before) == 2 * n_blocks + 1
    for task, n_steps in zip(rest, steps_before):
        for _ in range(n_steps):
            next(gla, None)
        task()
    for _ in gla:
        pass


def _rope_spread():
    half = ROPE_DIM // 2
    rows = 4 * half
    e = np.zeros((rows, 3 * LANES), np.float32)
    for lane in range(LANES):
        j = lane % SWA_HEAD_DIM
        if j < half:
            e[j, lane] = 1.0
            e[half + j, LANES + lane] = -1.0
        elif j < 2 * half:
            e[j - half, lane] = 1.0
            e[j, 2 * LANES + lane] = 1.0
        else:
            e[2 * half, lane] = 1.0
    return jnp.asarray(np.concatenate([e, e], axis=0), dtype=BF16)


def _inproj(x2, pos_row, invf, sink_feat, w_in_r, w_up_p, b_dec, gain, seq, tm):
    m = x2.shape[0]
    half = ROPE_DIM // 2
    row = lambda c: pl.BlockSpec((tm, c), lambda i: (i, 0))
    row_out = lambda c, d: (row(c), jax.ShapeDtypeStruct((m, c), d))
    slab_out = lambda n: (pl.BlockSpec((n, tm, LANES), lambda i: (0, i, 0)),
                          jax.ShapeDtypeStruct((n, m, LANES), BF16))
    outs = [row_out(GLA_V, BF16),
            slab_out(SWA_HEADS), slab_out(SWA_KV_HEADS), slab_out(SWA_KV_HEADS),
            slab_out(SWA_KV_HEADS), row_out(D_MODEL, BF16), row_out(D_MODEL, BF16)]
    return pl.pallas_call(
        functools.partial(_inproj_kernel, tiles_per_seq=seq // tm),
        grid=(m // tm,),
        in_specs=[row(D_MODEL), pl.BlockSpec((1, tm), lambda i: (0, i)),
                  _resident((half, LANES)), _resident((8 * half, 3 * LANES)),
                  _resident((SWA_HEADS, LANES)),
                  _resident((D_MODEL, _C_END)), _resident((RANK_PAD, GLA_QK)),
                  _resident((1, GLA_QK)), _resident((1, GLA_DV))],
        out_specs=[o[0] for o in outs],
        out_shape=[o[1] for o in outs],
        scratch_shapes=[pltpu.VMEM((tm, GLA_QK), BF16), pltpu.VMEM((tm, GLA_QK), BF16),
                        pltpu.VMEM((tm, GLA_V), BF16), pltpu.VMEM((tm, GLA_V), BF16),
                        pltpu.VMEM((tm, GLA_QK), F32),
                        pltpu.VMEM((GLA_HEADS, GLA_DK, GLA_DV), F32)],
        compiler_params=pltpu.CompilerParams(
            dimension_semantics=("arbitrary",), vmem_limit_bytes=VMEM_LIMIT),
        name="inproj",
    )(x2, pos_row, invf, _rope_spread(), sink_feat, w_in_r, w_up_p, b_dec, gain)


def _gla_stages(first_tile, q_ref, k_ref, v_ref, r_ref, la_ref, g_ref, o_ref, st_ref, *,
                n_blocks):
    @pl.when(first_tile)
    def _():
        st_ref[...] = jnp.zeros_like(st_ref)

    c_ = GLA_BLOCK
    causal = (lax.broadcasted_iota(jnp.int32, (c_, c_), 0)
              >= lax.broadcasted_iota(jnp.int32, (c_, c_), 1))
    tri = jnp.where(causal, 1.0, 0.0).astype(BF16)
    gain = g_ref[...]

    items = [(c, h) for c in range(n_blocks) for h in range(GLA_HEADS)]
    rows = lambda c: slice(c * c_, (c + 1) * c_)
    ks = lambda h: slice(h * GLA_DK, (h + 1) * GLA_DK)
    vs = lambda h: slice(h * GLA_DV, (h + 1) * GLA_DV)

    b = []
    for c in range(n_blocks):
        la = la_ref[rows(c), :]
        hi = la.astype(BF16)
        rem = la - hi.astype(F32)
        mid = rem.astype(BF16)
        lo = (rem - mid.astype(F32)).astype(BF16)
        b.append(_dot(tri, hi) + _dot(tri, mid) + _dot(tri, lo))

    q_in, k_out, b_end, scores = {}, {}, {}, {}

    def stage_a(c):
        for h in range(GLA_HEADS):
            bh = b[c][:, ks(h)]
            b_last = bh[c_ - 1:c_, :]
            q = q_ref[rows(c), ks(h)].astype(F32)
            k = k_ref[rows(c), ks(h)].astype(F32)
            q_in[c, h] = (q * jnp.exp2(bh)).astype(BF16)
            k_in = (k * jnp.exp2(-bh)).astype(BF16)
            k_out[c, h] = (k * jnp.exp2(b_last - bh)).astype(BF16)
            b_end[c, h] = b_last
            scores[c, h] = jnp.where(causal, _dot_nt(q_in[c, h], k_in), 0.0).astype(BF16)

    st = [st_ref[h] for h in range(GLA_HEADS)]

    def stage_b(c):
        for h in range(GLA_HEADS):
            v = v_ref[rows(c), vs(h)]
            o = _dot(scores[c, h], v) + _dot(q_in[c, h], st[h].astype(BF16))
            dec = jnp.exp2(jnp.broadcast_to(b_end[c, h], (GLA_DK, GLA_DK)).T)
            dec = jnp.concatenate([dec] * (GLA_DV // GLA_DK), axis=1)
            st[h] = st[h] * dec + _dot_tn(k_out[c, h], v)
            o = o * lax.rsqrt(jnp.mean(o * o, axis=-1, keepdims=True) + RMS_EPS)
            r = r_ref[rows(c), vs(h)].astype(F32)
            o = o * gain * (r * jax.nn.sigmoid(r))
            o_ref[rows(c), vs(h)] = o.astype(BF16)

    for stage in (stage_a, stage_b):
        for c in range(n_blocks):
            yield
            stage(c)
    for h in range(GLA_HEADS):
        st_ref[h] = st[h]


def _swa_kernel(q_ref, kc_ref, kp_ref, vec_ref, vep_ref, voc_ref, vop_ref, mask_ref, o_ref,
                *, blocks_per_step):
    w = SWA_WINDOW
    dh = SWA_HEAD_DIM
    per_group = SWA_HEADS // SWA_KV_HEADS
    first = pl.program_id(1) == 0
    row = lax.broadcasted_iota(jnp.int32, (w, LANES), 0)
    lane = lax.broadcasted_iota(jnp.int32, (w, LANES), 1)
    ident = jnp.where(row == lane, 1.0, 0.0).astype(BF16)
    sink_row = row == 0
    low = lane < dh
    zero = jnp.zeros((w, LANES), BF16)
    one_e = jnp.where(lane == dh, 1.0, 0.0).astype(BF16)
    one_o = jnp.where(lane == 0, 1.0, 0.0).astype(BF16)
    order = list(range(0, per_group, 2)) + list(range(1, per_group, 2))

    scores = []
    values = []
    for qb in range(blocks_per_step):
        rows = slice(qb * w, (qb + 1) * w)
        prev_rows = slice((qb - 1) * w, qb * w)
        mask = mask_ref[1] if qb else jnp.where(first, mask_ref[0], mask_ref[1])
        for g in range(SWA_KV_HEADS):
            k_prev = kc_ref[g, prev_rows] if qb else kp_ref[g]
            ve_prev = vec_ref[g, prev_rows] if qb else vep_ref[g]
            vo_prev = voc_ref[g, prev_rows] if qb else vop_ref[g]
            k2 = jnp.concatenate([jnp.where(sink_row, zero, k_prev), kc_ref[g, rows]], axis=0)
            kx = jnp.concatenate([k2, mask], axis=1)
            ve = jnp.concatenate([jnp.where(sink_row, one_e, ve_prev), vec_ref[g, rows]], axis=0)
            vo = jnp.concatenate([jnp.where(sink_row, one_o, vo_prev), voc_ref[g, rows]], axis=0)
            for part, vv in ((order[:per_group // 2], ve), (order[per_group // 2:], vo)):
                qs = jnp.concatenate(
                    [jnp.concatenate([q_ref[g * per_group + h, rows], ident], axis=1)
                     for h in part], axis=0)
                scores.append(_dot_nt(qs, kx))
                values.append(vv)
    pv = []
    for s, vv in zip(scores, values):
        p = jnp.exp2(s - jnp.max(s, axis=-1, keepdims=True)).astype(BF16)
        pv.append(_dot(p, vv))
    for qb in range(blocks_per_step):
        rows = slice(qb * w, (qb + 1) * w)
        for g in range(SWA_KV_HEADS):
            pv_e = pv[(qb * SWA_KV_HEADS + g) * 2]
            pv_o = pv[(qb * SWA_KV_HEADS + g) * 2 + 1]
            for c in range(per_group // 2):
                e = pv_e[c * w:(c + 1) * w]
                o = pv_o[c * w:(c + 1) * w]
                out = jnp.where(low, e * (1.0 / e[:, dh:dh + 1]), o * (1.0 / o[:, 0:1]))
                col = (g * (per_group // 2) + c) * LANES
                o_ref[rows, col:col + LANES] = out.astype(BF16)


def _swa_mask():
    w = SWA_WINDOW
    j = np.arange(2 * w)[:, None]
    i = np.arange(w)[None, :]
    band = (j > i) & (j <= i + w)
    later = band | (j == 0)
    first = (band & (j >= w)) | (j == 0)
    both = np.stack([first, later])
    return jnp.asarray(np.where(both, 0.0, MASK_BIAS), dtype=BF16)


def _swa(qh, kh, ve, vo, batch, seq, blocks_per_step):
    m = kh.shape[1]
    w = SWA_WINDOW
    tq = blocks_per_step * w
    steps = seq // tq
    kv = SWA_KV_HEADS
    cur = lambda n_: pl.BlockSpec((n_, tq, LANES), lambda b, t: (0, b * steps + t, 0))
    prev = lambda n_: pl.BlockSpec(
        (n_, w, LANES),
        lambda b, t: (0, b * (seq // w) + jnp.maximum(t * blocks_per_step - 1, 0), 0))
    return pl.pallas_call(
        functools.partial(_swa_kernel, blocks_per_step=blocks_per_step),
        grid=(batch, steps),
        in_specs=[cur(SWA_HEADS), cur(kv), prev(kv), cur(kv), prev(kv), cur(kv), prev(kv),
                  _resident((2, 2 * w, w))],
        out_specs=pl.BlockSpec((tq, SWA_Q), lambda b, t: (b * steps + t, 0)),
        out_shape=jax.ShapeDtypeStruct((m, SWA_Q), BF16),
        compiler_params=pltpu.CompilerParams(
            dimension_semantics=("arbitrary", "arbitrary"), vmem_limit_bytes=VMEM_LIMIT),
        name="swa",
    )(qh, kh, kh, ve, ve, vo, vo, _swa_mask())


def _layer_norm(y, g, b):
    mu = jnp.mean(y, axis=-1, keepdims=True)
    d = y - mu
    var = jnp.mean(d * d, axis=-1, keepdims=True)
    return d * lax.rsqrt(var + LN_EPS) * g + b


def _tail_kernel(x_ref, oa_ref, ob_ref, ga_ref, gb_ref, wa_ref, wb_ref, wo_ref, g1_ref, b1_ref,
                 wg_ref, wu_ref, wd_ref, g2_ref, b2_ref, o_ref, *, ff_chunk):
    d_ff = wg_ref.shape[1]
    n_chunks = d_ff // ff_chunk
    sl = lambda j: slice(j * ff_chunk, (j + 1) * ff_chunk)
    tm = x_ref.shape[0]
    parts = [slice(i * (tm // TAIL_SPLIT), (i + 1) * (tm // TAIL_SPLIT))
             for i in range(TAIL_SPLIT)]

    def merge(rs):
        y_a = _dot(oa_ref[rs, :], wa_ref[...])
        y_b = _dot(ob_ref[rs, :], wb_ref[...])
        merged = (jax.nn.sigmoid(ga_ref[rs, :].astype(F32)) * y_a
                  + jax.nn.sigmoid(gb_ref[rs, :].astype(F32)) * y_b)
        mix = _dot(merged.astype(BF16), wo_ref[...])
        return _layer_norm(DEEPNORM_ALPHA * x_ref[rs, :] + mix, g1_ref[...], b1_ref[...])

    def ffn(rs, h):
        hb = h.astype(BF16)

        def gate_up(j):
            return _dot(hb, wg_ref[:, sl(j)]), _dot(hb, wu_ref[:, sl(j)])

        acc = None
        gu = gate_up(0)
        for j in range(n_chunks):
            gate, up = gu
            if j + 1 < n_chunks:
                gu = gate_up(j + 1)
            act = (gate * jax.nn.sigmoid(gate) * up).astype(BF16)
            part = _dot(act, wd_ref[sl(j), :])
            acc = part if acc is None else acc + part
        o_ref[rs, :] = _layer_norm(DEEPNORM_ALPHA * h + acc, g2_ref[...], b2_ref[...])

    h_next = merge(parts[0])
    for i, rs in enumerate(parts):
        h = h_next
        if i + 1 < len(parts):
            h_next = merge(parts[i + 1])
        ffn(rs, h)


def _tail(x2, oa, ob, ga, gb, wa, wb, wo, ln1_g, ln1_b, wg, wu, wd, ln2_g, ln2_b, tm, ff_chunk):
    m = x2.shape[0]
    d_ff = wg.shape[1]
    row = pl.BlockSpec((tm, D_MODEL), lambda i: (i, 0))
    sq = _resident((D_MODEL, D_MODEL))
    vec = _resident((1, D_MODEL))
    return pl.pallas_call(
        functools.partial(_tail_kernel, ff_chunk=ff_chunk),
        grid=(m // tm,),
        in_specs=[row] * 5 + [sq, sq, sq, vec, vec,
                              _resident((D_MODEL, d_ff)), _resident((D_MODEL, d_ff)),
                              _resident((d_ff, D_MODEL)), vec, vec],
        out_specs=row,
        out_shape=jax.ShapeDtypeStruct((m, D_MODEL), F32),
        compiler_params=pltpu.CompilerParams(
            dimension_semantics=("arbitrary",), vmem_limit_bytes=VMEM_LIMIT),
        name="tail",
    )(x2, oa, ob, ga, gb, wa, wb, wo, ln1_g, ln1_b, wg, wu, wd, ln2_g, ln2_b)


def _rope_inv_freq():
    inv_freq = ROPE_THETA ** (-jnp.arange(0, ROPE_DIM, 2, dtype=F32) / ROPE_DIM)
    return jnp.broadcast_to(inv_freq[:, None], (ROPE_DIM // 2, LANES))


def _layer(h2, pos_row, invf, batch, seq, w_in, w_decay_up, b_decay, gla_norm_g,
           w_branch_a, w_branch_b, sinks, w_out, ln1_g, ln1_b,
           w_ffn_gate, w_ffn_up, w_ffn_down, ln2_g, ln2_b):
    w_in_r = _win_prep(w_in.T)
    w_up_p = jnp.concatenate(
        [w_decay_up, jnp.zeros((RANK_PAD - GLA_RANK, GLA_QK), w_decay_up.dtype)],
        axis=0).astype(BF16)

    neg_sink = -sinks.astype(F32) * LOG2_E
    sink_hi = neg_sink.astype(BF16).astype(F32)
    sink_feat = jnp.zeros((SWA_HEADS, LANES), F32)
    sink_feat = sink_feat.at[:, SWA_HEAD_DIM].set(sink_hi)
    sink_feat = sink_feat.at[:, SWA_HEAD_DIM + 1].set(neg_sink - sink_hi)

    oa, qh, kh, ve, vo, ga, gb = _inproj(
        h2, pos_row, invf, sink_feat, w_in_r, w_up_p, b_decay[None, :], gla_norm_g[None, :],
        seq, tm=512)
    ob = _swa(qh, kh, ve, vo, batch, seq, blocks_per_step=8)
    return _tail(h2, oa, ob, ga, gb, w_branch_a.astype(BF16), w_branch_b.astype(BF16),
                 w_out.astype(BF16), ln1_g[None, :], ln1_b[None, :],
                 w_ffn_gate.astype(BF16), w_ffn_up.astype(BF16), w_ffn_down.astype(BF16),
                 ln2_g[None, :], ln2_b[None, :], tm=512, ff_chunk=256)


def kernel(x, positions, w_in, w_decay_up, b_decay, gla_norm_g, w_branch_a, w_branch_b, sinks,
           w_out, ln1_g, ln1_b, w_ffn_gate, w_ffn_up, w_ffn_down, ln2_g, ln2_b):
    batch, seq, d = x.shape
    m = batch * seq
    h2 = x.reshape(m, d)
    pos_row = positions.reshape(1, m).astype(F32)
    invf = _rope_inv_freq()
    for layer in range(w_in.shape[0]):
        h2 = _layer(h2, pos_row, invf, batch, seq, w_in[layer], w_decay_up[layer], b_decay[layer],
                    gla_norm_g[layer], w_branch_a[layer], w_branch_b[layer], sinks[layer],
                    w_out[layer], ln1_g[layer], ln1_b[layer], w_ffn_gate[layer],
                    w_ffn_up[layer], w_ffn_down[layer], ln2_g[layer], ln2_b[layer])
    return h2.reshape(batch, seq, d)
```

```python
import functools

import math

import jax
import jax.numpy as jnp
import numpy as np
from jax import lax
from jax.experimental import pallas as pl
from jax.experimental.pallas import tpu as pltpu

D_MODEL = 1024
GLA_HEADS = 4
GLA_DK = 128
GLA_DV = 256
GLA_RANK = 16
GLA_TAU = 16.0
GLA_CHUNK = 64
GLA_BLOCK = GLA_CHUNK
SWA_HEADS = 16
SWA_KV_HEADS = 2
SWA_HEAD_DIM = 64
SWA_WINDOW = 128
ROPE_DIM = 16
ROPE_THETA = 500000.0
DEPTH = 1
DEEPNORM_ALPHA = (2.0 * DEPTH) ** 0.25
LN_EPS = 1e-5
RMS_EPS = 1e-6

GLA_QK = GLA_HEADS * GLA_DK
GLA_V = GLA_HEADS * GLA_DV
SWA_Q = SWA_HEADS * SWA_HEAD_DIM
SWA_KV = SWA_KV_HEADS * SWA_HEAD_DIM

LANES = 128
MXU_COLS = 256
RANK_PAD = LANES
VMEM_LIMIT = 56 * 1024 * 1024
TAIL_SPLIT = 2
LOG2_E = math.log2(math.e)
MASK_BIAS = -1e30

_C_QA = 0
_C_KA = _C_QA + GLA_QK
_C_VA = _C_KA + GLA_QK
_C_RA = _C_VA + GLA_V
_C_QB = _C_RA + GLA_V
_C_KV = _C_QB + SWA_Q
_C_GA = _C_KV + 2 * SWA_KV
_C_GB = _C_GA + D_MODEL
_C_DL = _C_GB + D_MODEL
_C_END = _C_DL + RANK_PAD
_D_IN = 2 * GLA_QK + 2 * GLA_V + GLA_RANK + SWA_Q + 2 * SWA_KV + 2 * D_MODEL
_O_DL = 2 * GLA_QK + 2 * GLA_V

F32 = jnp.float32
BF16 = jnp.bfloat16


def _dot(a, b):
    return jnp.dot(a, b, preferred_element_type=F32)


def _dot_nt(a, b):
    return lax.dot_general(a, b, (((1,), (1,)), ((), ())), preferred_element_type=F32)


def _dot_tn(a, b):
    return lax.dot_general(a, b, (((0,), (0,)), ((), ())), preferred_element_type=F32)


def _resident(shape):
    return pl.BlockSpec(shape, lambda *_: (0,) * len(shape), pipeline_mode=pl.Buffered(1))


def _win_prep_kernel(wt_ref, o_ref):
    def put(col, row):
        o_ref[:, col:col + LANES] = wt_ref[row:row + LANES, :].T.astype(BF16)

    for j in range(_O_DL // LANES):
        put(j * LANES, j * LANES)
    for j in range((_C_DL - _O_DL) // LANES):
        put(_O_DL + j * LANES, _O_DL + GLA_RANK + j * LANES)
    blk = wt_ref[_O_DL:_O_DL + LANES, :].T
    lane = lax.broadcasted_iota(jnp.int32, blk.shape, 1)
    o_ref[:, _C_DL:] = jnp.where(lane < GLA_RANK, blk, 0.0).astype(BF16)


def _win_prep(w_in_t):
    cb = LANES
    assert _C_DL - _O_DL == _D_IN - _O_DL - GLA_RANK and (_C_DL - _O_DL) % LANES == 0
    return pl.pallas_call(
        _win_prep_kernel,
        grid=(D_MODEL // cb,),
        in_specs=[pl.BlockSpec((_D_IN, cb), lambda i: (0, i))],
        out_specs=pl.BlockSpec((cb, _C_END), lambda i: (i, 0)),
        out_shape=jax.ShapeDtypeStruct((D_MODEL, _C_END), BF16),
        compiler_params=pltpu.CompilerParams(dimension_semantics=("arbitrary",)),
        name="win_prep",
    )(w_in_t)


def _inproj_kernel(x_ref, pos_ref, invf_ref, spread_ref, sf_ref, w_ref, wup_ref, bdec_ref,
                   gain_ref,
                   oa_ref, qh_ref, kh_ref, ve_ref, vo_ref, ga_ref, gb_ref,
                   qa_ref, ka_ref, va_ref, ra_ref, la_ref, st_ref, *, tiles_per_seq):
    xb = x_ref[...].astype(BF16)

    def proj(lo, hi):
        return _dot(xb, w_ref[:, lo:hi])

    tm = x_ref.shape[0]
    half = ROPE_DIM // 2
    ang = jnp.tile(invf_ref[...], (1, tm // LANES)) * pos_ref[...]
    one_row = jnp.where(lax.broadcasted_iota(jnp.int32, (2 * half, tm), 0) == 0, 1.0, 0.0)
    cs = jnp.concatenate([jnp.cos(ang), jnp.sin(ang), one_row], axis=0)
    cs_hi = cs.astype(BF16)
    cs_lo = (cs - cs_hi.astype(F32)).astype(BF16)
    tabs = _dot_tn(jnp.concatenate([cs_hi, cs_lo], axis=0), spread_ref[...])
    cos, a_tab, b_tab = tabs[:, :LANES], tabs[:, LANES:2 * LANES], tabs[:, 2 * LANES:]

    def rope(t):
        return (t * cos + pltpu.roll(t, LANES - half, 1) * a_tab
                + pltpu.roll(t, half, 1) * b_tab)

    dh = SWA_HEAD_DIM
    lane_full = lax.broadcasted_iota(jnp.int32, (tm, LANES), 1)
    low = lane_full < dh
    q_scale = (dh ** -0.5) * LOG2_E

    def swa_queries(i):
        qb = proj(_C_QB + i * MXU_COLS, _C_QB + (i + 1) * MXU_COLS)
        for c in range(2 * i, 2 * i + 2):
            t = rope(qb[:, (c - 2 * i) * LANES:(c - 2 * i + 1) * LANES]) * q_scale
            qh_ref[2 * c] = jnp.where(low, t, sf_ref[2 * c:2 * c + 1, :]).astype(BF16)
            qh_ref[2 * c + 1] = jnp.where(
                low, pltpu.roll(t, dh, 1), sf_ref[2 * c + 1:2 * c + 2, :]).astype(BF16)

    def swa_keys_values_and_decay():
        kv = proj(_C_KV, _C_GA)
        kk = rope(kv[:, :SWA_KV])
        vv = kv[:, SWA_KV:]
        kk_sw = pltpu.roll(kk, dh, 1)
        vv_sw = pltpu.roll(vv, dh, 1)
        one_k = jnp.where(lane_full < dh + 2, 1.0, 0.0)
        one_e = jnp.where(lane_full == dh, 1.0, 0.0)
        one_o = jnp.where(lane_full == 0, 1.0, 0.0)
        kh_ref[0] = jnp.where(low, kk, one_k).astype(BF16)
        kh_ref[1] = jnp.where(low, kk_sw, one_k).astype(BF16)
        ve_ref[0] = jnp.where(low, vv, one_e).astype(BF16)
        ve_ref[1] = jnp.where(low, vv_sw, one_e).astype(BF16)
        vo_ref[0] = jnp.where(low, one_o, vv_sw).astype(BF16)
        vo_ref[1] = jnp.where(low, one_o, vv).astype(BF16)
        return proj(_C_DL, _C_END).astype(BF16)

    def log_decay(d_low, h):
        cols = slice(h * GLA_DK, (h + 1) * GLA_DK)
        z = _dot(d_low, wup_ref[:, cols]) + bdec_ref[:, cols]
        log_sig = jnp.minimum(z, 0.0) - jnp.log1p(jnp.exp(-jnp.abs(z)))
        la_ref[:, cols] = log_sig * (LOG2_E / GLA_TAU)

    def plain(ref, col0, i, scale=None):
        y = proj(col0 + i * MXU_COLS, col0 + (i + 1) * MXU_COLS)
        if scale is not None:
            y = y * scale
        ref[:, i * MXU_COLS:(i + 1) * MXU_COLS] = y.astype(BF16)

    gla_tiles = (
        [(qa_ref, _C_QA, i, GLA_DK ** -0.5) for i in range(GLA_QK // MXU_COLS)]
        + [(ka_ref, _C_KA, i, None) for i in range(GLA_QK // MXU_COLS)]
        + [(va_ref, _C_VA, i, None) for i in range(GLA_V // MXU_COLS)]
        + [(ra_ref, _C_RA, i, None) for i in range(GLA_V // MXU_COLS)])
    d_low = swa_keys_values_and_decay()
    per_head = len(gla_tiles) // GLA_HEADS
    for h in range(GLA_HEADS):
        log_decay(d_low, h)
        for args in gla_tiles[h * per_head:(h + 1) * per_head]:
            plain(*args)

    first_tile = pl.program_id(0) % tiles_per_seq == 0
    gla = _gla_stages(first_tile, qa_ref, ka_ref, va_ref, ra_ref, la_ref, gain_ref, oa_ref,
                      st_ref, n_blocks=tm // GLA_BLOCK)
    rest = []
    for i in range(SWA_Q // MXU_COLS):
        rest.append(functools.partial(swa_queries, i))
        rest.append(functools.partial(plain, ga_ref, _C_GA, i))
    rest += [functools.partial(plain, gb_ref, _C_GB, i) for i in range(D_MODEL // MXU_COLS)]
    n_blocks = tm // GLA_BLOCK
    assert len(rest) == 12 and n_blocks == 8
    steps_before = [1, 1, 2, 2, 2, 2, 1, 1, 1, 2, 2, 0]
    assert sum(steps_before) == 2 * n_blocks + 1
    for task, n_steps in zip(rest, steps_before):
        for _ in range(n_steps):
            next(gla, None)
        task()
    for _ in gla:
        pass


def _rope_spread():
    half = ROPE_DIM // 2
    rows = 4 * half
    e = np.zeros((rows, 3 * LANES), np.float32)
    for lane in range(LANES):
        j = lane % SWA_HEAD_DIM
        if j < half:
            e[j, lane] = 1.0
            e[half + j, LANES + lane] = -1.0
        elif j < 2 * half:
            e[j - half, lane] = 1.0
            e[j, 2 * LANES + lane] = 1.0
        else:
            e[2 * half, lane] = 1.0
    return jnp.asarray(np.concatenate([e, e], axis=0), dtype=BF16)


def _inproj(x2, pos_row, invf, sink_feat, w_in_r, w_up_p, b_dec, gain, seq, tm):
    m = x2.shape[0]
    half = ROPE_DIM // 2
    row = lambda c: pl.BlockSpec((tm, c), lambda i: (i, 0))
    row_out = lambda c, d: (row(c), jax.ShapeDtypeStruct((m, c), d))
    slab_out = lambda n: (pl.BlockSpec((n, tm, LANES), lambda i: (0, i, 0)),
                          jax.ShapeDtypeStruct((n, m, LANES), BF16))
    outs = [row_out(GLA_V, BF16),
            slab_out(SWA_HEADS), slab_out(SWA_KV_HEADS), slab_out(SWA_KV_HEADS),
            slab_out(SWA_KV_HEADS), row_out(D_MODEL, BF16), row_out(D_MODEL, BF16)]
    return pl.pallas_call(
        functools.partial(_inproj_kernel, tiles_per_seq=seq // tm),
        grid=(m // tm,),
        in_specs=[row(D_MODEL), pl.BlockSpec((1, tm), lambda i: (0, i)),
                  _resident((half, LANES)), _resident((8 * half, 3 * LANES)),
                  _resident((SWA_HEADS, LANES)),
                  _resident((D_MODEL, _C_END)), _resident((RANK_PAD, GLA_QK)),
                  _resident((1, GLA_QK)), _resident((1, GLA_DV))],
        out_specs=[o[0] for o in outs],
        out_shape=[o[1] for o in outs],
        scratch_shapes=[pltpu.VMEM((tm, GLA_QK), BF16), pltpu.VMEM((tm, GLA_QK), BF16),
                        pltpu.VMEM((tm, GLA_V), BF16), pltpu.VMEM((tm, GLA_V), BF16),
                        pltpu.VMEM((tm, GLA_QK), F32),
                        pltpu.VMEM((GLA_HEADS, GLA_DK, GLA_DV), F32)],
        compiler_params=pltpu.CompilerParams(
            dimension_semantics=("arbitrary",), vmem_limit_bytes=VMEM_LIMIT),
        name="inproj",
    )(x2, pos_row, invf, _rope_spread(), sink_feat, w_in_r, w_up_p, b_dec, gain)


def _gla_stages(first_tile, q_ref, k_ref, v_ref, r_ref, la_ref, g_ref, o_ref, st_ref, *,
                n_blocks):
    @pl.when(first_tile)
    def _():
        st_ref[...] = jnp.zeros_like(st_ref)

    c_ = GLA_BLOCK
    causal = (lax.broadcasted_iota(jnp.int32, (c_, c_), 0)
              >= lax.broadcasted_iota(jnp.int32, (c_, c_), 1))
    tri = jnp.where(causal, 1.0, 0.0).astype(BF16)
    gain = g_ref[...]

    items = [(c, h) for c in range(n_blocks) for h in range(GLA_HEADS)]
    rows = lambda c: slice(c * c_, (c + 1) * c_)
    ks = lambda h: slice(h * GLA_DK, (h + 1) * GLA_DK)
    vs = lambda h: slice(h * GLA_DV, (h + 1) * GLA_DV)

    b = []
    for c in range(n_blocks):
        la = la_ref[rows(c), :]
        hi = la.astype(BF16)
        rem = la - hi.astype(F32)
        mid = rem.astype(BF16)
        lo = (rem - mid.astype(F32)).astype(BF16)
        b.append(_dot(tri, hi) + _dot(tri, mid) + _dot(tri, lo))

    q_in, k_out, b_end, scores = {}, {}, {}, {}

    def stage_a(c):
        for h in range(GLA_HEADS):
            bh = b[c][:, ks(h)]
            b_last = bh[c_ - 1:c_, :]
            q = q_ref[rows(c), ks(h)].astype(F32)
            k = k_ref[rows(c), ks(h)].astype(F32)
            q_in[c, h] = (q * jnp.exp2(bh)).astype(BF16)
            k_in = (k * jnp.exp2(-bh)).astype(BF16)
            k_out[c, h] = (k * jnp.exp2(b_last - bh)).astype(BF16)
            b_end[c, h] = b_last
            scores[c, h] = jnp.where(causal, _dot_nt(q_in[c, h], k_in), 0.0).astype(BF16)

    st = [st_ref[h] for h in range(GLA_HEADS)]

    def stage_b(c):
        for h in range(GLA_HEADS):
            v = v_ref[rows(c), vs(h)]
            o = _dot(scores[c, h], v) + _dot(q_in[c, h], st[h].astype(BF16))
            dec = jnp.exp2(jnp.broadcast_to(b_end[c, h], (GLA_DK, GLA_DK)).T)
            dec = jnp.concatenate([dec] * (GLA_DV // GLA_DK), axis=1)
            st[h] = st[h] * dec + _dot_tn(k_out[c, h], v)
            o = o * lax.rsqrt(jnp.mean(o * o, axis=-1, keepdims=True) + RMS_EPS)
            r = r_ref[rows(c), vs(h)].astype(F32)
            o = o * gain * (r * jax.nn.sigmoid(r))
            o_ref[rows(c), vs(h)] = o.astype(BF16)

    for stage in (stage_a, stage_b):
        for c in range(n_blocks):
            yield
            stage(c)
    for h in range(GLA_HEADS):
        st_ref[h] = st[h]


def _swa_kernel(q_ref, kc_ref, kp_ref, vec_ref, vep_ref, voc_ref, vop_ref, mask_ref, o_ref,
                *, blocks_per_step):
    w = SWA_WINDOW
    dh = SWA_HEAD_DIM
    per_group = SWA_HEADS // SWA_KV_HEADS
    first = pl.program_id(1) == 0
    row = lax.broadcasted_iota(jnp.int32, (w, LANES), 0)
    lane = lax.broadcasted_iota(jnp.int32, (w, LANES), 1)
    ident = jnp.where(row == lane, 1.0, 0.0).astype(BF16)
    sink_row = row == 0
    low = lane < dh
    zero = jnp.zeros((w, LANES), BF16)
    one_e = jnp.where(lane == dh, 1.0, 0.0).astype(BF16)
    one_o = jnp.where(lane == 0, 1.0, 0.0).astype(BF16)
    order = list(range(0, per_group, 2)) + list(range(1, per_group, 2))

    scores = []
    values = []
    for qb in range(blocks_per_step):
        rows = slice(qb * w, (qb + 1) * w)
        prev_rows = slice((qb - 1) * w, qb * w)
        mask = mask_ref[1] if qb else jnp.where(first, mask_ref[0], mask_ref[1])
        for g in range(SWA_KV_HEADS):
            k_prev = kc_ref[g, prev_rows] if qb else kp_ref[g]
            ve_prev = vec_ref[g, prev_rows] if qb else vep_ref[g]
            vo_prev = voc_ref[g, prev_rows] if qb else vop_ref[g]
            k2 = jnp.concatenate([jnp.where(sink_row, zero, k_prev), kc_ref[g, rows]], axis=0)
            kx = jnp.concatenate([k2, mask], axis=1)
            ve = jnp.concatenate([jnp.where(sink_row, one_e, ve_prev), vec_ref[g, rows]], axis=0)
            vo = jnp.concatenate([jnp.where(sink_row, one_o, vo_prev), voc_ref[g, rows]], axis=0)
            for part, vv in ((order[:per_group // 2], ve), (order[per_group // 2:], vo)):
                qs = jnp.concatenate(
                    [jnp.concatenate([q_ref[g * per_group + h, rows], ident], axis=1)
                     for h in part], axis=0)
                scores.append(_dot_nt(qs, kx))
                values.append(vv)
    pv = []
    for s, vv in zip(scores, values):
        p = jnp.exp2(s - jnp.max(s, axis=-1, keepdims=True)).astype(BF16)
        pv.append(_dot(p, vv))
    for qb in range(blocks_per_step):
        rows = slice(qb * w, (qb + 1) * w)
        for g in range(SWA_KV_HEADS):
            pv_e = pv[(qb * SWA_KV_HEADS + g) * 2]
            pv_o = pv[(qb * SWA_KV_HEADS + g) * 2 + 1]
            for c in range(per_group // 2):
                e = pv_e[c * w:(c + 1) * w]
                o = pv_o[c * w:(c + 1) * w]
                out = jnp.where(low, e * (1.0 / e[:, dh:dh + 1]), o * (1.0 / o[:, 0:1]))
                col = (g * (per_group // 2) + c) * LANES
                o_ref[rows, col:col + LANES] = out.astype(BF16)


def _swa_mask():
    w = SWA_WINDOW
    j = np.arange(2 * w)[:, None]
    i = np.arange(w)[None, :]
    band = (j > i) & (j <= i + w)
    later = band | (j == 0)
    first = (band & (j >= w)) | (j == 0)
    both = np.stack([first, later])
    return jnp.asarray(np.where(both, 0.0, MASK_BIAS), dtype=BF16)


def _swa(qh, kh, ve, vo, batch, seq, blocks_per_step):
    m = kh.shape[1]
    w = SWA_WINDOW
    tq = blocks_per_step * w
    steps = seq // tq
    kv = SWA_KV_HEADS
    cur = lambda n_: pl.BlockSpec((n_, tq, LANES), lambda b, t: (0, b * steps + t, 0))
    prev = lambda n_: pl.BlockSpec(
        (n_, w, LANES),
        lambda b, t: (0, b * (seq // w) + jnp.maximum(t * blocks_per_step - 1, 0), 0))
    return pl.pallas_call(
        functools.partial(_swa_kernel, blocks_per_step=blocks_per_step),
        grid=(batch, steps),
        in_specs=[cur(SWA_HEADS), cur(kv), prev(kv), cur(kv), prev(kv), cur(kv), prev(kv),
                  _resident((2, 2 * w, w))],
        out_specs=pl.BlockSpec((tq, SWA_Q), lambda b, t: (b * steps + t, 0)),
        out_shape=jax.ShapeDtypeStruct((m, SWA_Q), BF16),
        compiler_params=pltpu.CompilerParams(
            dimension_semantics=("arbitrary", "arbitrary"), vmem_limit_bytes=VMEM_LIMIT),
        name="swa",
    )(qh, kh, kh, ve, ve, vo, vo, _swa_mask())


def _layer_norm(y, g, b):
    mu = jnp.mean(y, axis=-1, keepdims=True)
    d = y - mu
    var = jnp.mean(d * d, axis=-1, keepdims=True)
    return d * lax.rsqrt(var + LN_EPS) * g + b


def _tail_kernel(x_ref, oa_ref, ob_ref, ga_ref, gb_ref, wa_ref, wb_ref, wo_ref, g1_ref, b1_ref,
                 wg_ref, wu_ref, wd_ref, g2_ref, b2_ref, o_ref, *, ff_chunk):
    d_ff = wg_ref.shape[1]
    n_chunks = d_ff // ff_chunk
    sl = lambda j: slice(j * ff_chunk, (j + 1) * ff_chunk)
    tm = x_ref.shape[0]
    parts = [slice(i * (tm // TAIL_SPLIT), (i + 1) * (tm // TAIL_SPLIT))
             for i in range(TAIL_SPLIT)]

    def merge(rs):
        y_a = _dot(oa_ref[rs, :], wa_ref[...])
        y_b = _dot(ob_ref[rs, :], wb_ref[...])
        merged = (jax.nn.sigmoid(ga_ref[rs, :].astype(F32)) * y_a
                  + jax.nn.sigmoid(gb_ref[rs, :].astype(F32)) * y_b)
        mix = _dot(merged.astype(BF16), wo_ref[...])
        return _layer_norm(DEEPNORM_ALPHA * x_ref[rs, :] + mix, g1_ref[...], b1_ref[...])

    def ffn(rs, h):
        hb = h.astype(BF16)

        def gate_up(j):
            return _dot(hb, wg_ref[:, sl(j)]), _dot(hb, wu_ref[:, sl(j)])

        acc = None
        gu = gate_up(0)
        for j in range(n_chunks):
            gate, up = gu
            if j + 1 < n_chunks:
                gu = gate_up(j + 1)
            act = (gate * jax.nn.sigmoid(gate) * up).astype(BF16)
            part = _dot(act, wd_ref[sl(j), :])
            acc = part if acc is None else acc + part
        o_ref[rs, :] = _layer_norm(DEEPNORM_ALPHA * h + acc, g2_ref[...], b2_ref[...])

    h_next = merge(parts[0])
    for i, rs in enumerate(parts):
        h = h_next
        if i + 1 < len(parts):
            h_next = merge(parts[i + 1])
        ffn(rs, h)


def _tail(x2, oa, ob, ga, gb, wa, wb, wo, ln1_g, ln1_b, wg, wu, wd, ln2_g, ln2_b, tm, ff_chunk):
    m = x2.shape[0]
    d_ff = wg.shape[1]
    row = pl.BlockSpec((tm, D_MODEL), lambda i: (i, 0))
    sq = _resident((D_MODEL, D_MODEL))
    vec = _resident((1, D_MODEL))
    return pl.pallas_call(
        functools.partial(_tail_kernel, ff_chunk=ff_chunk),
        grid=(m // tm,),
        in_specs=[row] * 5 + [sq, sq, sq, vec, vec,
                              _resident((D_MODEL, d_ff)), _resident((D_MODEL, d_ff)),
                              _resident((d_ff, D_MODEL)), vec, vec],
        out_specs=row,
        out_shape=jax.ShapeDtypeStruct((m, D_MODEL), F32),
        compiler_params=pltpu.CompilerParams(
            dimension_semantics=("arbitrary",), vmem_limit_bytes=VMEM_LIMIT),
        name="tail",
    )(x2, oa, ob, ga, gb, wa, wb, wo, ln1_g, ln1_b, wg, wu, wd, ln2_g, ln2_b)


def _rope_inv_freq():
    inv_freq = ROPE_THETA ** (-jnp.arange(0, ROPE_DIM, 2, dtype=F32) / ROPE_DIM)
    return jnp.broadcast_to(inv_freq[:, None], (ROPE_DIM // 2, LANES))


def _layer(h2, pos_row, invf, batch, seq, w_in, w_decay_up, b_decay, gla_norm_g,
           w_branch_a, w_branch_b, sinks, w_out, ln1_g, ln1_b,
           w_ffn_gate, w_ffn_up, w_ffn_down, ln2_g, ln2_b):
    w_in_r = _win_prep(w_in.T)
    w_up_p = jnp.concatenate(
        [w_decay_up, jnp.zeros((RANK_PAD - GLA_RANK, GLA_QK), w_decay_up.dtype)],
        axis=0).astype(BF16)

    neg_sink = -sinks.astype(F32) * LOG2_E
    sink_hi = neg_sink.astype(BF16).astype(F32)
    sink_feat = jnp.zeros((SWA_HEADS, LANES), F32)
    sink_feat = sink_feat.at[:, SWA_HEAD_DIM].set(sink_hi)
    sink_feat = sink_feat.at[:, SWA_HEAD_DIM + 1].set(neg_sink - sink_hi)

    oa, qh, kh, ve, vo, ga, gb = _inproj(
        h2, pos_row, invf, sink_feat, w_in_r, w_up_p, b_decay[None, :], gla_norm_g[None, :],
        seq, tm=512)
    ob = _swa(qh, kh, ve, vo, batch, seq, blocks_per_step=8)
    return _tail(h2, oa, ob, ga, gb, w_branch_a.astype(BF16), w_branch_b.astype(BF16),
                 w_out.astype(BF16), ln1_g[None, :], ln1_b[None, :],
                 w_ffn_gate.astype(BF16), w_ffn_up.astype(BF16), w_ffn_down.astype(BF16),
                 ln2_g[None, :], ln2_b[None, :], tm=512, ff_chunk=256)


def kernel(x, positions, w_in, w_decay_up, b_decay, gla_norm_g, w_branch_a, w_branch_b, sinks,
           w_out, ln1_g, ln1_b, w_ffn_gate, w_ffn_up, w_ffn_down, ln2_g, ln2_b):
    batch, seq, d = x.shape
    m = batch * seq
    h2 = x.reshape(m, d)
    pos_row = positions.reshape(1, m).astype(F32)
    invf = _rope_inv_freq()
    for layer in range(w_in.shape[0]):
        h2 = _layer(h2, pos_row, invf, batch, seq, w_in[layer], w_decay_up[layer], b_decay[layer],
                    gla_norm_g[layer], w_branch_a[layer], w_branch_b[layer], sinks[layer],
                    w_out[layer], ln1_g[layer], ln1_b[layer], w_ffn_gate[layer],
                    w_ffn_up[layer], w_ffn_down[layer], ln2_g[layer], ln2_b[layer])
    return h2.reshape(batch, seq, d)
```
